```python
import math
import jax, jax.numpy as jnp
from jax import lax

D_MODEL = 1024
BATCH = 16
SEQ = 2048
DEPTH = 2
DEC_BATCH = 128
DEC_SEQ = 4
PAST_LEN = 16384
PAGE_SIZE = 128

N_MIXERS = 4
GROUP_W = D_MODEL // N_MIXERS
CONV_K = 4
FFN_DIM = 2816
EPS = 1e-6
F32 = jnp.float32

GDN_HEADS = 4
GDN_DK = GROUP_W // GDN_HEADS
GDN_DV = GROUP_W // GDN_HEADS
GDN_CHUNK = 64
MLP_CHUNK = 128
MLP_GROUPS = 4
MLP_GW = GROUP_W // MLP_GROUPS
SSM_HEADS = 4
SSM_HEAD_DIM = GROUP_W // SSM_HEADS
SSM_GROUPS = 2
SSM_STATE = 128
SSM_CHUNK = 128
SSM_CONV_W = GROUP_W + 2 * SSM_GROUPS * SSM_STATE
MLA_HEADS = 4
MLA_NOPE = 64
MLA_ROPE = 32
MLA_V_DIM = GROUP_W // MLA_HEADS
MLA_Q_RANK = 256
MLA_KV_RANK = 128
MLA_SCALE = (MLA_NOPE + MLA_ROPE) ** -0.5
ROPE_THETA = 10000.0
Q_BLOCK = 128

A_COLS = 4 * GROUP_W + 2 * GDN_HEADS
B_COLS = 2 * GROUP_W
C_COLS = GROUP_W + SSM_CONV_W + SSM_HEADS
D_COLS = MLA_Q_RANK + MLA_KV_RANK + MLA_ROPE
IN_COLS = A_COLS + B_COLS + C_COLS + D_COLS

kernel_name = 'hymba_gdn_gmlp_ssd_mla_decode_step'


def rms_norm(x, g):
    xf = x.astype(F32)
    return (xf * lax.rsqrt(jnp.mean(xf * xf, axis=-1, keepdims=True) + EPS) * g.astype(F32)).astype(x.dtype)


def layer_norm(x, g, b):
    xf = x.astype(F32)
    mu = jnp.mean(xf, axis=-1, keepdims=True)
    var = jnp.mean(jnp.square(xf - mu), axis=-1, keepdims=True)
    return ((xf - mu) * lax.rsqrt(var + EPS) * g.astype(F32) + b.astype(F32)).astype(x.dtype)


def l2_norm(x):
    xf = x.astype(F32)
    return (xf * lax.rsqrt(jnp.sum(xf * xf, axis=-1, keepdims=True) + EPS)).astype(x.dtype)


def swiglu(x, w_in, w_out):
    gate, up = jnp.split(x @ w_in, 2, axis=-1)
    return (jax.nn.silu(gate) * up) @ w_out


def causal_conv(x, buf, w):
    seq_len = x.shape[1]
    xp = jnp.concatenate([buf.astype(x.dtype), x], axis=1)
    y = sum(xp[:, i:i + seq_len] * w[i] for i in range(CONV_K))
    return y, xp[:, -(CONV_K - 1):]


def rope(x, pos):
    half = MLA_ROPE // 2
    inv_freq = ROPE_THETA ** (-jnp.arange(half, dtype=F32) / half)
    ang = pos.astype(F32)[:, None] * inv_freq[None, :]
    cos, sin = jnp.cos(ang)[None, :, None, :], jnp.sin(ang)[None, :, None, :]
    xf = x.astype(F32)
    x1, x2 = xf[..., :half], xf[..., half:]
    return jnp.concatenate([x1 * cos - x2 * sin, x1 * sin + x2 * cos], axis=-1).astype(x.dtype)


def gated_delta_chunked(q, k, v, log_alpha, beta, s0, chunk):
    out_dtype = v.dtype
    bsz, nh, seq_len, dk = q.shape
    n = seq_len // chunk

    def blk(t):
        return t.astype(F32).reshape(bsz, nh, n, chunk, *t.shape[3:])

    q, k, v, beta = blk(q), blk(k), blk(v), blk(beta)
    gcum = jnp.cumsum(blk(log_alpha), axis=-1)
    causal = jnp.tril(jnp.ones((chunk, chunk), bool))
    decay = jnp.exp(jnp.where(causal, gcum[..., :, None] - gcum[..., None, :], -jnp.inf))
    kb = k * beta[..., None]
    a_strict = jnp.tril(jnp.einsum('bhnid,bhnjd->bhnij', kb, k) * decay, -1)
    eye = jnp.eye(chunk, dtype=F32)
    rhs = jnp.concatenate([kb * jnp.exp(gcum)[..., None], v * beta[..., None]], axis=-1)
    sol = lax.linalg.triangular_solve(eye + a_strict, rhs, left_side=True, lower=True, unit_diagonal=True)
    w_c, u_c = sol[..., :dk], sol[..., dk:]
    qk = jnp.einsum('bhnid,bhnjd->bhnij', q, k) * decay
    q_dec = q * jnp.exp(gcum)[..., None]
    k_dec = k * jnp.exp(gcum[..., -1:] - gcum)[..., None]
    a_last = jnp.exp(gcum[..., -1])

    def step(s, xs):
        w_i, u_i, qk_i, qd_i, kd_i, al_i = xs
        u_new = u_i - jnp.einsum('bhcd,bhde->bhce', w_i, s)
        o = jnp.einsum('bhcd,bhde->bhce', qd_i, s) + jnp.einsum('bhcj,bhje->bhce', qk_i, u_new)
        s = s * al_i[..., None, None] + jnp.einsum('bhcd,bhce->bhde', kd_i, u_new)
        return s, o

    xs = tuple(jnp.moveaxis(t, 2, 0) for t in (w_c, u_c, qk, q_dec, k_dec, a_last))
    s_fin, o = lax.scan(step, s0.astype(F32), xs)
    o = jnp.moveaxis(o, 0, 2).reshape(bsz, nh, seq_len, -1)
    return o.astype(out_dtype), s_fin.astype(s0.dtype)


def ssd_chunked(x, dt, a_neg, bm, cm, h0, chunk):
    out_dtype = x.dtype
    bsz, nh, seq_len, _ = x.shape
    n = seq_len // chunk

    def blk(t):
        return t.astype(F32).reshape(bsz, nh, n, chunk, *t.shape[3:])

    x, dt, bm, cm = blk(x), blk(dt), blk(bm), blk(cm)
    acum = jnp.cumsum(dt * a_neg.astype(F32)[:, None, None], axis=-1)
    causal = jnp.tril(jnp.ones((chunk, chunk), bool))
    decay = jnp.exp(jnp.where(causal, acum[..., :, None] - acum[..., None, :], -jnp.inf))
    xdt = x * dt[..., None]
    y_intra = jnp.einsum('bhnij,bhnjp->bhnip', jnp.einsum('bhnid,bhnjd->bhnij', cm, bm) * decay, xdt)
    c_dec = cm * jnp.exp(acum)[..., None]
    chunk_state = jnp.einsum('bhnjd,bhnjp->bhnpd', bm * jnp.exp(acum[..., -1:] - acum)[..., None], xdt)
    a_last = jnp.exp(acum[..., -1])

    def step(h, xs):
        cd_i, st_i, al_i = xs
        y = jnp.einsum('bhid,bhpd->bhip', cd_i, h)
        return h * al_i[..., None, None] + st_i, y

    xs = tuple(jnp.moveaxis(t, 2, 0) for t in (c_dec, chunk_state, a_last))
    h_fin, y_inter = lax.scan(step, h0.astype(F32), xs)
    y = y_intra + jnp.moveaxis(y_inter, 0, 2)
    return y.reshape(bsz, nh, seq_len, -1).astype(out_dtype), h_fin.astype(h0.dtype)


def mla_attend(q_lat, q_pe, q_pos, ckv, kpe, k_pos):
    s = jnp.einsum('bqhr,bkr->bhqk', q_lat, ckv) + jnp.einsum('bqhe,bke->bhqk', q_pe, kpe)
    s = jnp.where(k_pos[None, :] <= q_pos[:, None], s.astype(F32) * MLA_SCALE, -jnp.inf)
    p = jax.nn.softmax(s, axis=-1).astype(ckv.dtype)
    return jnp.einsum('bhqk,bkr->bqhr', p, ckv)


def token_mixing(h, w, gdn_conv, gdn_s, ssm_conv, ssm_h, past_ckv, past_kpe):
    bsz, seq_len, _ = h.shape
    prompt = past_ckv is None
    proj = h @ w['w_in']
    p_a, p_b, p_c, p_d = jnp.split(proj, [A_COLS, A_COLS + B_COLS, A_COLS + B_COLS + C_COLS], axis=-1)

    qkv_raw, z_a, a_a, b_a = jnp.split(p_a, [3 * GROUP_W, 4 * GROUP_W, 4 * GROUP_W + GDN_HEADS], axis=-1)
    qkv, gdn_conv_new = causal_conv(qkv_raw, gdn_conv, w['gdn_conv_w'])
    q, k, v = jnp.split(jax.nn.silu(qkv), 3, axis=-1)

    def heads(t):
        return t.reshape(bsz, seq_len, GDN_HEADS, -1).transpose(0, 2, 1, 3)

    q = l2_norm(heads(q)) * GDN_DK ** -0.5
    k = l2_norm(heads(k))
    log_alpha = -jnp.exp(w['gdn_a_log'].astype(F32)) * jax.nn.softplus(a_a.astype(F32) + w['gdn_dt_bias'].astype(F32))
    beta = jax.nn.sigmoid(b_a.astype(F32))
    o_a, gdn_s_new = gated_delta_chunked(q, k, heads(v), log_alpha.transpose(0, 2, 1), beta.transpose(0, 2, 1),
                                         gdn_s, GDN_CHUNK if prompt else seq_len)
    o_a = rms_norm(o_a.transpose(0, 2, 1, 3), w['gdn_norm_g']) * jax.nn.silu(z_a.reshape(bsz, seq_len, GDN_HEADS, GDN_DV))
    out_a = o_a.reshape(bsz, seq_len, GROUP_W)

    u, v_b = jnp.split(jax.nn.gelu(p_b), 2, axis=-1)
    v_b = layer_norm(v_b, w['mlp_ln_g'], w['mlp_ln_b'])
    cm = MLP_CHUNK if prompt else seq_len
    ws = jnp.where(jnp.tril(jnp.ones((MLP_CHUNK, MLP_CHUNK), bool)), w['mlp_ws'], 0)[:, :cm, :cm]
    vr = v_b.reshape(bsz, seq_len // cm, cm, MLP_GROUPS, MLP_GW)
    mixed = jnp.einsum('gpq,bnqgc->bnpgc', ws, vr) + w['mlp_bs'][:, :cm].T[:, :, None]
    out_b = u * mixed.reshape(bsz, seq_len, GROUP_W)

    z_c, xbc_raw, dt_raw = jnp.split(p_c, [GROUP_W, GROUP_W + SSM_CONV_W], axis=-1)
    xbc, ssm_conv_new = causal_conv(xbc_raw, ssm_conv, w['ssm_conv_w'])
    xbc = jax.nn.silu(xbc + w['ssm_conv_b'])
    xs_c, b_c, c_c = jnp.split(xbc, [GROUP_W, GROUP_W + SSM_GROUPS * SSM_STATE], axis=-1)

    def to_heads(t):
        t = t.reshape(bsz, seq_len, SSM_GROUPS, SSM_STATE)
        return jnp.repeat(t, SSM_HEADS // SSM_GROUPS, axis=2).transpose(0, 2, 1, 3)

    xs_h = xs_c.reshape(bsz, seq_len, SSM_HEADS, SSM_HEAD_DIM).transpose(0, 2, 1, 3)
    dt = jax.nn.softplus(dt_raw.astype(F32) + w['ssm_dt_bias'].astype(F32)).transpose(0, 2, 1)
    y_c, ssm_h_new = ssd_chunked(xs_h, dt, -jnp.exp(w['ssm_a_log'].astype(F32)), to_heads(b_c), to_heads(c_c),
                                 ssm_h, SSM_CHUNK if prompt else seq_len)
    y_c = (y_c + w['ssm_d'][:, None, None] * xs_h).transpose(0, 2, 1, 3).reshape(bsz, seq_len, GROUP_W)
    out_c = rms_norm(y_c * jax.nn.silu(z_c), w['ssm_norm_g'])

    c_q, c_kv, k_pe = jnp.split(p_d, [MLA_Q_RANK, MLA_Q_RANK + MLA_KV_RANK], axis=-1)
    c_q = rms_norm(c_q, w['mla_q_norm_g'])
    c_kv = rms_norm(c_kv, w['mla_kv_norm_g'])
    pos0 = 0 if prompt else past_ckv.shape[1]
    q_pos = pos0 + jnp.arange(seq_len)
    q_full = (c_q @ w['mla_w_uq']).reshape(bsz, seq_len, MLA_HEADS, MLA_NOPE + MLA_ROPE)
    q_nope, q_pe = jnp.split(q_full, [MLA_NOPE], axis=-1)
    q_pe = rope(q_pe, q_pos)
    k_pe = rope(k_pe[:, :, None, :], q_pos)[:, :, 0, :]
    q_lat = jnp.einsum('blhd,hrd->blhr', q_nope, w['mla_w_uk'])
    if prompt:
        keys_lat, keys_pe, k_pos = c_kv, k_pe, q_pos
    else:
        keys_lat = jnp.concatenate([past_ckv.astype(c_kv.dtype), c_kv], axis=1)
        keys_pe = jnp.concatenate([past_kpe.astype(k_pe.dtype), k_pe], axis=1)
        k_pos = jnp.arange(pos0 + seq_len)
    blk = Q_BLOCK if prompt else seq_len
    nb = seq_len // blk

    def qblocks(t):
        return t.reshape(bsz, nb, blk, *t.shape[2:]).swapaxes(0, 1)

    o_lat = lax.map(lambda t: mla_attend(t[0], t[1], t[2], keys_lat, keys_pe, k_pos),
                    (qblocks(q_lat), qblocks(q_pe), q_pos.reshape(nb, blk)))
    o_lat = o_lat.swapaxes(0, 1).reshape(bsz, seq_len, MLA_HEADS, MLA_KV_RANK)
    out_d = jnp.einsum('blhr,hrd->blhd', o_lat, w['mla_w_uv']).reshape(bsz, seq_len, GROUP_W)

    mix = jnp.concatenate([out_a, out_b, out_c, out_d], axis=-1) @ w['w_out']
    return mix, (c_kv, k_pe, gdn_s_new, gdn_conv_new, ssm_h_new, ssm_conv_new, v_b)


def trunk_layer(x, w, gdn_conv, gdn_s, ssm_conv, ssm_h, past_ckv, past_kpe):
    g = w['norm_g']
    x = x + 0.5 * rms_norm(swiglu(rms_norm(x, g[0]), w['ffn_w_in'][0], w['ffn_w_out'][0]), g[1])
    mix, new_state = token_mixing(rms_norm(x, g[2]), w, gdn_conv, gdn_s, ssm_conv, ssm_h, past_ckv, past_kpe)
    x = x + rms_norm(mix, g[3])
    x = x + 0.5 * rms_norm(swiglu(rms_norm(x, g[4]), w['ffn_w_in'][1], w['ffn_w_out'][1]), g[5])
    return x, new_state


def setup_inputs(seed: int = 0) -> dict:
    key = jax.random.key(seed)
    keys = jax.random.split(key, 48)
    count = [0]

    def nk():
        count[0] += 1
        return keys[count[0] - 1]

    def normal(shape, scale):
        return scale * jax.random.normal(nk(), shape, F32)

    def gain(shape):
        return 1.0 + normal(shape, 0.02)

    def a_log(shape):
        return jnp.log(jax.random.uniform(nk(), shape, F32, 1.0, 16.0))

    def dt_bias(shape):
        dt = jnp.exp(jax.random.uniform(nk(), shape, F32, math.log(1e-3), math.log(1e-1)))
        return dt + jnp.log(-jnp.expm1(-dt))

    n_pages = PAST_LEN // PAGE_SIZE
    n_phys = (DEC_BATCH * n_pages * 5) // 4
    page_table = jax.random.permutation(nk(), n_phys)[:DEC_BATCH * n_pages].reshape(DEC_BATCH, n_pages).astype(jnp.int32)
    return {
        'x_prompt': normal((BATCH, SEQ, D_MODEL), 1.0),
        'x_sample': normal((DEC_BATCH, DEC_SEQ, D_MODEL), 1.0),
        'cache_ckv': normal((DEPTH, n_phys, PAGE_SIZE, MLA_KV_RANK), 1.0),
        'cache_kpe': normal((DEPTH, n_phys, PAGE_SIZE, MLA_ROPE), 1.0),
        'page_table': page_table,
        'state_gdn_s': normal((DEPTH, DEC_BATCH, GDN_HEADS, GDN_DK, GDN_DV), 0.1),
        'state_gdn_conv': normal((DEPTH, DEC_BATCH, CONV_K - 1, 3 * GROUP_W), 1.0),
        'state_ssm_h': normal((DEPTH, DEC_BATCH, SSM_HEADS, SSM_HEAD_DIM, SSM_STATE), 0.1),
        'state_ssm_conv': normal((DEPTH, DEC_BATCH, CONV_K - 1, SSM_CONV_W), 1.0),
        'norm_g': gain((DEPTH, 6, D_MODEL)),
        'ffn_w_in': normal((DEPTH, 2, D_MODEL, 2 * FFN_DIM), D_MODEL ** -0.5),
        'ffn_w_out': normal((DEPTH, 2, FFN_DIM, D_MODEL), FFN_DIM ** -0.5),
        'w_in': normal((DEPTH, D_MODEL, IN_COLS), D_MODEL ** -0.5),
        'w_out': normal((DEPTH, N_MIXERS * GROUP_W, D_MODEL), (N_MIXERS * GROUP_W) ** -0.5),
        'gdn_conv_w': normal((DEPTH, CONV_K, 3 * GROUP_W), CONV_K ** -0.5),
        'gdn_a_log': a_log((DEPTH, GDN_HEADS)),
        'gdn_dt_bias': dt_bias((DEPTH, GDN_HEADS)),
        'gdn_norm_g': gain((DEPTH, GDN_DV)),
        'mlp_ln_g': gain((DEPTH, GROUP_W)),
        'mlp_ln_b': normal((DEPTH, GROUP_W), 0.02),
        'mlp_ws': normal((DEPTH, MLP_GROUPS, MLP_CHUNK, MLP_CHUNK), MLP_CHUNK ** -0.5),
        'mlp_bs': normal((DEPTH, MLP_GROUPS, MLP_CHUNK), 0.02),
        'ssm_conv_w': normal((DEPTH, CONV_K, SSM_CONV_W), CONV_K ** -0.5),
        'ssm_conv_b': normal((DEPTH, SSM_CONV_W), 0.02),
        'ssm_a_log': a_log((DEPTH, SSM_HEADS)),
        'ssm_dt_bias': dt_bias((DEPTH, SSM_HEADS)),
        'ssm_d': gain((DEPTH, SSM_HEADS)),
        'ssm_norm_g': gain((DEPTH, GROUP_W)),
        'mla_q_norm_g': gain((DEPTH, MLA_Q_RANK)),
        'mla_w_uq': normal((DEPTH, MLA_Q_RANK, MLA_HEADS * (MLA_NOPE + MLA_ROPE)), MLA_Q_RANK ** -0.5),
        'mla_kv_norm_g': gain((DEPTH, MLA_KV_RANK)),
        'mla_w_uk': normal((DEPTH, MLA_HEADS, MLA_KV_RANK, MLA_NOPE), MLA_KV_RANK ** -0.5),
        'mla_w_uv': normal((DEPTH, MLA_HEADS, MLA_KV_RANK, MLA_V_DIM), MLA_KV_RANK ** -0.5),
    }


def reference(x_prompt, x_sample, cache_ckv, cache_kpe, page_table, state_gdn_s, state_gdn_conv,
              state_ssm_h, state_ssm_conv, norm_g, ffn_w_in, ffn_w_out, w_in, w_out,
              gdn_conv_w, gdn_a_log, gdn_dt_bias, gdn_norm_g, mlp_ln_g, mlp_ln_b, mlp_ws, mlp_bs,
              ssm_conv_w, ssm_conv_b, ssm_a_log, ssm_dt_bias, ssm_d, ssm_norm_g,
              mla_q_norm_g, mla_w_uq, mla_kv_norm_g, mla_w_uk, mla_w_uv):
    weights = dict(norm_g=norm_g, ffn_w_in=ffn_w_in, ffn_w_out=ffn_w_out, w_in=w_in, w_out=w_out,
                   gdn_conv_w=gdn_conv_w, gdn_a_log=gdn_a_log, gdn_dt_bias=gdn_dt_bias, gdn_norm_g=gdn_norm_g,
                   mlp_ln_g=mlp_ln_g, mlp_ln_b=mlp_ln_b, mlp_ws=mlp_ws, mlp_bs=mlp_bs,
                   ssm_conv_w=ssm_conv_w, ssm_conv_b=ssm_conv_b, ssm_a_log=ssm_a_log,
                   ssm_dt_bias=ssm_dt_bias, ssm_d=ssm_d, ssm_norm_g=ssm_norm_g,
                   mla_q_norm_g=mla_q_norm_g, mla_w_uq=mla_w_uq, mla_kv_norm_g=mla_kv_norm_g,
                   mla_w_uk=mla_w_uk, mla_w_uv=mla_w_uv)
    bp = x_prompt.shape[0]
    n_seq, n_pages = page_table.shape
    dtype = x_prompt.dtype
    xp, xs = x_prompt, x_sample
    st_p, st_s = [], []
    for l in range(DEPTH):
        wl = {name: arr[l] for name, arr in weights.items()}
        xp, st = trunk_layer(xp, wl,
                             jnp.zeros((bp, CONV_K - 1, 3 * GROUP_W), dtype),
                             jnp.zeros((bp, GDN_HEADS, GDN_DK, GDN_DV), dtype),
                             jnp.zeros((bp, CONV_K - 1, SSM_CONV_W), dtype),
                             jnp.zeros((bp, SSM_HEADS, SSM_HEAD_DIM, SSM_STATE), dtype),
                             None, None)
        st_p.append(st)
        past_ckv = cache_ckv[l, page_table].reshape(n_seq, n_pages * PAGE_SIZE, MLA_KV_RANK)
        past_kpe = cache_kpe[l, page_table].reshape(n_seq, n_pages * PAGE_SIZE, MLA_ROPE)
        xs, st = trunk_layer(xs, wl, state_gdn_conv[l], state_gdn_s[l], state_ssm_conv[l], state_ssm_h[l],
                             past_ckv, past_kpe)
        st_s.append(st)

    def stack(states, i):
        return jnp.stack([s[i] for s in states])

    return (xp, xs,
            stack(st_p, 0), stack(st_p, 1), stack(st_p, 2), stack(st_p, 3), stack(st_p, 4), stack(st_p, 5),
            stack(st_s, 0), stack(st_s, 1), stack(st_s, 2), stack(st_s, 3), stack(st_s, 4), stack(st_s, 5),
            stack(st_s, 6))
```

```python
import functools
import math

import numpy as np
import jax
import jax.numpy as jnp
from jax import lax
from jax.experimental import pallas as pl
from jax.experimental.pallas import tpu as pltpu

F32 = jnp.float32
BF16 = jnp.bfloat16

D_MODEL = 1024
DEPTH = 2
PAGE_SIZE = 128
GROUP_W = 256
CONV_K = 4
FFN_DIM = 2816
EPS = 1e-6

GDN_HEADS = 4
GDN_DK = 64
GDN_DV = 64
GDN_CHUNK = 64
MLP_CHUNK = 128
MLP_GROUPS = 4
MLP_GW = 64
SSM_HEADS = 4
SSM_HEAD_DIM = 64
SSM_GROUPS = 2
SSM_STATE = 128
SSM_CHUNK = 128
SSM_CONV_W = 768
MLA_HEADS = 4
MLA_NOPE = 64
MLA_ROPE = 32
MLA_V_DIM = 64
MLA_Q_RANK = 256
MLA_KV_RANK = 128
MLA_SCALE = (MLA_NOPE + MLA_ROPE) ** -0.5
ROPE_THETA = 10000.0
Q_BLOCK = 128

V7X_LANES = 128
V7X_MXU_DIM = 256
V7X_VMEM_LIMIT_BYTES = 56 * 1024 * 1024

_SRC_COLS = dict(
    qkv=(0, 768), z_a=(768, 1024), a_a=(1024, 1028), b_a=(1028, 1032),
    u=(1032, 1288), v=(1288, 1544),
    z_c=(1544, 1800), xbc=(1800, 2568), dt=(2568, 2572),
    c_q=(2572, 2828), c_kv=(2828, 2956), k_pe=(2956, 2988))
_DST_ORDER = ('qkv', 'z_a', 'u', 'v', 'z_c', 'xbc', 'c_q', 'c_kv', 'k_pe', 'a_a', 'b_a', 'dt')
_DST_COLS = {}
_off = 0
for _name in _DST_ORDER:
    _w = _SRC_COLS[_name][1] - _SRC_COLS[_name][0]
    _DST_COLS[_name] = (_off, _off + _w)
    _off += _w
PROJ_USED = _off
PROJ_COLS = -(-PROJ_USED // V7X_LANES) * V7X_LANES
_PERM = np.concatenate([np.arange(*_SRC_COLS[n]) for n in _DST_ORDER])

FFN_CHUNK = V7X_MXU_DIM
TOKEN_TILE = 512


def _rms(x, g):
    return x * lax.rsqrt(jnp.mean(x * x, axis=-1, keepdims=True) + EPS) * g


def _ffn_body(x, gpre_ref, gpost_ref, win_ref, wout_ref, o_ref, acc_ref):
    xn = _rms(x, gpre_ref[...]).astype(BF16)
    n_chunks = FFN_DIM // FFN_CHUNK
    for c in range(n_chunks):
        lo = c * FFN_CHUNK
        gate = jnp.dot(xn, win_ref[:, lo:lo + FFN_CHUNK], preferred_element_type=F32)
        up = jnp.dot(xn, win_ref[:, FFN_DIM + lo:FFN_DIM + lo + FFN_CHUNK], preferred_element_type=F32)
        h = (gate * jax.nn.sigmoid(gate) * up).astype(BF16)
        part = jnp.dot(h, wout_ref[lo:lo + FFN_CHUNK, :], preferred_element_type=F32)
        if c == 0:
            acc_ref[...] = part
        else:
            acc_ref[...] += part
    o_ref[...] = x + 0.5 * _rms(acc_ref[...], gpost_ref[...])


def _ffn_kernel(x_ref, gpre_ref, gpost_ref, win_ref, wout_ref, o_ref, acc_ref):
    _ffn_body(x_ref[...], gpre_ref, gpost_ref, win_ref, wout_ref, o_ref, acc_ref)


def _mix_ffn_kernel(x_ref, m_ref, wo_ref, gmix_ref, gpre_ref, gpost_ref, win_ref, wout_ref, o_ref, acc_ref):
    y = jnp.dot(m_ref[...].astype(BF16), wo_ref[...], preferred_element_type=F32)
    x = x_ref[...] + _rms(y, gmix_ref[...])
    _ffn_body(x, gpre_ref, gpost_ref, win_ref, wout_ref, o_ref, acc_ref)


def _resident(shape):
    return pl.BlockSpec(shape, lambda i: (0,) * len(shape), pipeline_mode=pl.Buffered(1))


def _ffn(x, g_pre, g_post, w_in, w_out, mix=None, w_o=None, g_mix=None):
    t = x.shape[0]
    tm = min(TOKEN_TILE, t)
    assert t % tm == 0
    tok = pl.BlockSpec((tm, D_MODEL), lambda i: (i, 0))
    vec = _resident((1, D_MODEL))
    w_specs = [_resident((D_MODEL, 2 * FFN_DIM)), _resident((FFN_DIM, D_MODEL))]
    if mix is None:
        kern, args = _ffn_kernel, (x, g_pre, g_post, w_in, w_out)
        in_specs = [tok, vec, vec] + w_specs
    else:
        kern, args = _mix_ffn_kernel, (x, mix, w_o, g_mix, g_pre, g_post, w_in, w_out)
        in_specs = [tok, tok, _resident((D_MODEL, D_MODEL)), vec, vec, vec] + w_specs
    return pl.pallas_call(
        kern,
        grid=(t // tm,),
        in_specs=in_specs,
        out_specs=tok,
        out_shape=jax.ShapeDtypeStruct((t, D_MODEL), F32),
        scratch_shapes=[pltpu.VMEM((tm, D_MODEL), F32)],
        compiler_params=pltpu.CompilerParams(
            dimension_semantics=("arbitrary",), vmem_limit_bytes=V7X_VMEM_LIMIT_BYTES),
        name="mix_ffn" if mix is not None else "ffn",
    )(*args)


def _proj_kernel(x_ref, g_ref, w_ref, o_ref):
    xn = _rms(x_ref[...], g_ref[...]).astype(BF16)
    o_ref[...] = jnp.dot(xn, w_ref[...], preferred_element_type=F32)


def _proj(x, g, w):
    t = x.shape[0]
    tm = min(TOKEN_TILE, t)
    assert t % tm == 0
    return pl.pallas_call(
        _proj_kernel,
        grid=(t // tm,),
        in_specs=[pl.BlockSpec((tm, D_MODEL), lambda i: (i, 0)),
                  _resident((1, D_MODEL)), _resident((D_MODEL, PROJ_COLS))],
        out_specs=pl.BlockSpec((tm, PROJ_COLS), lambda i: (i, 0)),
        out_shape=jax.ShapeDtypeStruct((t, PROJ_COLS), F32),
        compiler_params=pltpu.CompilerParams(
            dimension_semantics=("arbitrary",), vmem_limit_bytes=V7X_VMEM_LIMIT_BYTES),
        name="in_proj",
    )(x, g, w)


def _l2(x):
    return x * lax.rsqrt(jnp.sum(x * x, axis=-1, keepdims=True) + EPS)


def _layer_norm(x, g, b):
    mu = jnp.mean(x, axis=-1, keepdims=True)
    var = jnp.mean(jnp.square(x - mu), axis=-1, keepdims=True)
    return (x - mu) * lax.rsqrt(var + EPS) * g + b


def _conv(x, buf, w):
    seq_len = x.shape[1]
    xp = jnp.concatenate([buf, x], axis=1)
    y = sum(xp[:, i:i + seq_len] * w[i] for i in range(CONV_K))
    return y, xp[:, -(CONV_K - 1):]


def _rope(x, pos):
    half = MLA_ROPE // 2
    inv_freq = ROPE_THETA ** (-jnp.arange(half, dtype=F32) / half)
    ang = pos.astype(F32)[:, None] * inv_freq[None, :]
    cos, sin = jnp.cos(ang)[None, :, None, :], jnp.sin(ang)[None, :, None, :]
    x1, x2 = x[..., :half], x[..., half:]
    return jnp.concatenate([x1 * cos - x2 * sin, x1 * sin + x2 * cos], axis=-1)


def _gdn_chunked(q, k, v, log_alpha, beta, s0, chunk):
    bsz, nh, seq_len, dk = q.shape
    n = seq_len // chunk

    def blk(t):
        return t.reshape(bsz, nh, n, chunk, *t.shape[3:])

    q, k, v, beta = blk(q), blk(k), blk(v), blk(beta)
    gcum = jnp.cumsum(blk(log_alpha), axis=-1)
    causal = jnp.tril(jnp.ones((chunk, chunk), bool))
    decay = jnp.exp(jnp.where(causal, gcum[..., :, None] - gcum[..., None, :], -jnp.inf))
    kb = k * beta[..., None]
    a_strict = jnp.tril(jnp.einsum('bhnid,bhnjd->bhnij', kb, k) * decay, -1)
    eye = jnp.eye(chunk, dtype=F32)
    rhs = jnp.concatenate([kb * jnp.exp(gcum)[..., None], v * beta[..., None]], axis=-1)
    sol = lax.linalg.triangular_solve(eye + a_strict, rhs, left_side=True, lower=True, unit_diagonal=True)
    w_c, u_c = sol[..., :dk], sol[..., dk:]
    qk = jnp.einsum('bhnid,bhnjd->bhnij', q, k) * decay
    q_dec = q * jnp.exp(gcum)[..., None]
    k_dec = k * jnp.exp(gcum[..., -1:] - gcum)[..., None]
    a_last = jnp.exp(gcum[..., -1])

    def step(s, xs):
        w_i, u_i, qk_i, qd_i, kd_i, al_i = xs
        u_new = u_i - jnp.einsum('bhcd,bhde->bhce', w_i, s)
        o = jnp.einsum('bhcd,bhde->bhce', qd_i, s) + jnp.einsum('bhcj,bhje->bhce', qk_i, u_new)
        s = s * al_i[..., None, None] + jnp.einsum('bhcd,bhce->bhde', kd_i, u_new)
        return s, o

    xs = tuple(jnp.moveaxis(t, 2, 0) for t in (w_c, u_c, qk, q_dec, k_dec, a_last))
    s_fin, o = lax.scan(step, s0, xs)
    o = jnp.moveaxis(o, 0, 2).reshape(bsz, nh, seq_len, -1)
    return o, s_fin


def _ssd_chunked(x, dt, a_neg, bm, cm, h0, chunk):
    bsz, nh, seq_len, _ = x.shape
    n = seq_len // chunk

    def blk(t):
        return t.reshape(bsz, nh, n, chunk, *t.shape[3:])

    x, dt, bm, cm = blk(x), blk(dt), blk(bm), blk(cm)
    acum = jnp.cumsum(dt * a_neg[:, None, None], axis=-1)
    causal = jnp.tril(jnp.ones((chunk, chunk), bool))
    decay = jnp.exp(jnp.where(causal, acum[..., :, None] - acum[..., None, :], -jnp.inf))
    xdt = x * dt[..., None]
    y_intra = jnp.einsum('bhnij,bhnjp->bhnip', jnp.einsum('bhnid,bhnjd->bhnij', cm, bm) * decay, xdt)
    c_dec = cm * jnp.exp(acum)[..., None]
    chunk_state = jnp.einsum('bhnjd,bhnjp->bhnpd', bm * jnp.exp(acum[..., -1:] - acum)[..., None], xdt)
    a_last = jnp.exp(acum[..., -1])

    def step(h, xs):
        cd_i, st_i, al_i = xs
        y = jnp.einsum('bhid,bhpd->bhip', cd_i, h)
        return h * al_i[..., None, None] + st_i, y

    xs = tuple(jnp.moveaxis(t, 2, 0) for t in (c_dec, chunk_state, a_last))
    h_fin, y_inter = lax.scan(step, h0, xs)
    y = y_intra + jnp.moveaxis(y_inter, 0, 2)
    return y.reshape(bsz, nh, seq_len, -1), h_fin


def _mla_attend(q_lat, q_pe, q_pos, ckv, kpe, k_pos):
    s = jnp.einsum('bqhr,bkr->bhqk', q_lat, ckv) + jnp.einsum('bqhe,bke->bhqk', q_pe, kpe)
    s = jnp.where(k_pos[None, :] <= q_pos[:, None], s * MLA_SCALE, -jnp.inf)
    p = jax.nn.softmax(s, axis=-1)
    return jnp.einsum('bhqk,bkr->bqhr', p, ckv)


def _piece(p, name):
    lo, hi = _DST_COLS[name]
    return p[..., lo:hi]


def _token_mixing(p, w, gdn_conv, gdn_s, ssm_conv, ssm_h, past_ckv, past_kpe):
    bsz, seq_len, _ = p.shape
    prompt = past_ckv is None

    qkv_raw, z_a, a_a, b_a = (_piece(p, n) for n in ('qkv', 'z_a', 'a_a', 'b_a'))
    qkv, gdn_conv_new = _conv(qkv_raw, gdn_conv, w['gdn_conv_w'])
    q, k, v = jnp.split(jax.nn.silu(qkv), 3, axis=-1)

    def heads(t):
        return t.reshape(bsz, seq_len, GDN_HEADS, -1).transpose(0, 2, 1, 3)

    q = _l2(heads(q)) * GDN_DK ** -0.5
    k = _l2(heads(k))
    log_alpha = -jnp.exp(w['gdn_a_log']) * jax.nn.softplus(a_a + w['gdn_dt_bias'])
    beta = jax.nn.sigmoid(b_a)
    o_a, gdn_s_new = _gdn_chunked(q, k, heads(v), log_alpha.transpose(0, 2, 1), beta.transpose(0, 2, 1),
                                  gdn_s, GDN_CHUNK if prompt else seq_len)
    o_a = _rms(o_a.transpose(0, 2, 1, 3), w['gdn_norm_g']) * jax.nn.silu(z_a.reshape(bsz, seq_len, GDN_HEADS, GDN_DV))
    out_a = o_a.reshape(bsz, seq_len, GROUP_W)

    u = jax.nn.gelu(_piece(p, 'u'))
    v_b = _layer_norm(jax.nn.gelu(_piece(p, 'v')), w['mlp_ln_g'], w['mlp_ln_b'])
    cm = MLP_CHUNK if prompt else seq_len
    ws = jnp.where(jnp.tril(jnp.ones((MLP_CHUNK, MLP_CHUNK), bool)), w['mlp_ws'], 0)[:, :cm, :cm]
    vr = v_b.reshape(bsz, seq_len // cm, cm, MLP_GROUPS, MLP_GW)
    mixed = jnp.einsum('gpq,bnqgc->bnpgc', ws, vr) + w['mlp_bs'][:, :cm].T[:, :, None]
    out_b = u * mixed.reshape(bsz, seq_len, GROUP_W)

    z_c, xbc_raw, dt_raw = (_piece(p, n) for n in ('z_c', 'xbc', 'dt'))
    xbc, ssm_conv_new = _conv(xbc_raw, ssm_conv, w['ssm_conv_w'])
    xbc = jax.nn.silu(xbc + w['ssm_conv_b'])
    xs_c, b_c, c_c = jnp.split(xbc, [GROUP_W, GROUP_W + SSM_GROUPS * SSM_STATE], axis=-1)

    def to_heads(t):
        t = t.reshape(bsz, seq_len, SSM_GROUPS, SSM_STATE)
        return jnp.repeat(t, SSM_HEADS // SSM_GROUPS, axis=2).transpose(0, 2, 1, 3)

    xs_h = xs_c.reshape(bsz, seq_len, SSM_HEADS, SSM_HEAD_DIM).transpose(0, 2, 1, 3)
    dt = jax.nn.softplus(dt_raw + w['ssm_dt_bias']).transpose(0, 2, 1)
    y_c, ssm_h_new = _ssd_chunked(xs_h, dt, -jnp.exp(w['ssm_a_log']), to_heads(b_c), to_heads(c_c),
                                  ssm_h, SSM_CHUNK if prompt else seq_len)
    y_c = (y_c + w['ssm_d'][:, None, None] * xs_h).transpose(0, 2, 1, 3).reshape(bsz, seq_len, GROUP_W)
    out_c = _rms(y_c * jax.nn.silu(z_c), w['ssm_norm_g'])

    c_q = _rms(_piece(p, 'c_q'), w['mla_q_norm_g'])
    c_kv = _rms(_piece(p, 'c_kv'), w['mla_kv_norm_g'])
    k_pe = _piece(p, 'k_pe')
    pos0 = 0 if prompt else past_ckv.shape[1]
    q_pos = pos0 + jnp.arange(seq_len)
    q_full = (c_q @ w['mla_w_uq']).reshape(bsz, seq_len, MLA_HEADS, MLA_NOPE + MLA_ROPE)
    q_nope, q_pe = jnp.split(q_full, [MLA_NOPE], axis=-1)
    q_pe = _rope(q_pe, q_pos)
    k_pe = _rope(k_pe[:, :, None, :], q_pos)[:, :, 0, :]
    q_lat = jnp.einsum('blhd,hrd->blhr', q_nope, w['mla_w_uk'])
    if prompt:
        keys_lat, keys_pe, k_pos = c_kv, k_pe, q_pos
    else:
        keys_lat = jnp.concatenate([past_ckv, c_kv], axis=1)
        keys_pe = jnp.concatenate([past_kpe, k_pe], axis=1)
        k_pos = jnp.arange(pos0 + seq_len)
    blk = Q_BLOCK if prompt else seq_len
    nb = seq_len // blk

    def qblocks(t):
        return t.reshape(bsz, nb, blk, *t.shape[2:]).swapaxes(0, 1)

    o_lat = lax.map(lambda t: _mla_attend(t[0], t[1], t[2], keys_lat, keys_pe, k_pos),
                    (qblocks(q_lat), qblocks(q_pe), q_pos.reshape(nb, blk)))
    o_lat = o_lat.swapaxes(0, 1).reshape(bsz, seq_len, MLA_HEADS, MLA_KV_RANK)
    out_d = jnp.einsum('blhr,hrd->blhd', o_lat, w['mla_w_uv']).reshape(bsz, seq_len, GROUP_W)

    mix = jnp.concatenate([out_a, out_b, out_c, out_d], axis=-1)
    return mix, (c_kv, k_pe, gdn_s_new, gdn_conv_new, ssm_h_new, ssm_conv_new, v_b)


def _trunk_layer(x, w, gdn_conv, gdn_s, ssm_conv, ssm_h, past_ckv, past_kpe):
    bsz, seq_len, _ = x.shape
    g = w['norm_g']
    x2 = x.reshape(bsz * seq_len, D_MODEL)
    x2 = _ffn(x2, g[0:1], g[1:2], w['ffn_w_in'][0], w['ffn_w_out'][0])
    p = _proj(x2, g[2:3], w['w_in']).reshape(bsz, seq_len, PROJ_COLS)
    mix, new_state = _token_mixing(p, w, gdn_conv, gdn_s, ssm_conv, ssm_h, past_ckv, past_kpe)
    x2 = _ffn(x2, g[4:5], g[5:6], w['ffn_w_in'][1], w['ffn_w_out'][1],
              mix=mix.reshape(bsz * seq_len, D_MODEL), w_o=w['w_out'], g_mix=g[3:4])
    return x2.reshape(bsz, seq_len, D_MODEL), new_state


def kernel(x_prompt, x_sample, cache_ckv, cache_kpe, page_table, state_gdn_s, state_gdn_conv,
           state_ssm_h, state_ssm_conv, norm_g, ffn_w_in, ffn_w_out, w_in, w_out,
           gdn_conv_w, gdn_a_log, gdn_dt_bias, gdn_norm_g, mlp_ln_g, mlp_ln_b, mlp_ws, mlp_bs,
           ssm_conv_w, ssm_conv_b, ssm_a_log, ssm_dt_bias, ssm_d, ssm_norm_g,
           mla_q_norm_g, mla_w_uq, mla_kv_norm_g, mla_w_uk, mla_w_uv):
    w_in_p = jnp.pad(w_in[:, :, _PERM], ((0, 0), (0, 0), (0, PROJ_COLS - PROJ_USED)))
    weights = dict(norm_g=norm_g, ffn_w_in=ffn_w_in.astype(BF16), ffn_w_out=ffn_w_out.astype(BF16),
                   w_in=w_in_p.astype(BF16), w_out=w_out.astype(BF16),
                   gdn_conv_w=gdn_conv_w, gdn_a_log=gdn_a_log, gdn_dt_bias=gdn_dt_bias, gdn_norm_g=gdn_norm_g,
                   mlp_ln_g=mlp_ln_g, mlp_ln_b=mlp_ln_b, mlp_ws=mlp_ws, mlp_bs=mlp_bs,
                   ssm_conv_w=ssm_conv_w, ssm_conv_b=ssm_conv_b, ssm_a_log=ssm_a_log,
                   ssm_dt_bias=ssm_dt_bias, ssm_d=ssm_d, ssm_norm_g=ssm_norm_g,
                   mla_q_norm_g=mla_q_norm_g, mla_w_uq=mla_w_uq, mla_kv_norm_g=mla_kv_norm_g,
                   mla_w_uk=mla_w_uk, mla_w_uv=mla_w_uv)
    bp = x_prompt.shape[0]
    n_seq, n_pages = page_table.shape
    xp, xs = x_prompt, x_sample
    st_p, st_s = [], []
    for l in range(DEPTH):
        wl = {name: arr[l] for name, arr in weights.items()}
        xp, st = _trunk_layer(xp, wl,
                              jnp.zeros((bp, CONV_K - 1, 3 * GROUP_W), F32),
                              jnp.zeros((bp, GDN_HEADS, GDN_DK, GDN_DV), F32),
                              jnp.zeros((bp, CONV_K - 1, SSM_CONV_W), F32),
                              jnp.zeros((bp, SSM_HEADS, SSM_HEAD_DIM, SSM_STATE), F32),
                              None, None)
        st_p.append(st)
        past_ckv = cache_ckv[l, page_table].reshape(n_seq, n_pages * PAGE_SIZE, MLA_KV_RANK)
        past_kpe = cache_kpe[l, page_table].reshape(n_seq, n_pages * PAGE_SIZE, MLA_ROPE)
        xs, st = _trunk_layer(xs, wl, state_gdn_conv[l], state_gdn_s[l], state_ssm_conv[l], state_ssm_h[l],
                              past_ckv, past_kpe)
        st_s.append(st)

    def stack(states, i):
        return jnp.stack([s[i] for s in states])

    return (xp, xs,
            stack(st_p, 0), stack(st_p, 1), stack(st_p, 2), stack(st_p, 3), stack(st_p, 4), stack(st_p, 5),
            stack(st_s, 0), stack(st_s, 1), stack(st_s, 2), stack(st_s, 3), stack(st_s, 4), stack(st_s, 5),
            stack(st_s, 6))
```

```python
import functools

import numpy as np
import jax
import jax.numpy as jnp
from jax import lax
from jax.experimental import pallas as pl
from jax.experimental.pallas import tpu as pltpu

F32 = jnp.float32
BF16 = jnp.bfloat16

D_MODEL = 1024
DEPTH = 2
PAGE_SIZE = 128
GROUP_W = 256
CONV_K = 4
FFN_DIM = 2816
EPS = 1e-6

GDN_HEADS = 4
GDN_DK = 64
GDN_DV = 64
MLP_CHUNK = 128
MLP_GROUPS = 4
MLP_GW = 64
SSM_HEADS = 4
SSM_HEAD_DIM = 64
SSM_GROUPS = 2
SSM_STATE = 128
SSM_CONV_W = 768
MLA_HEADS = 4
MLA_NOPE = 64
MLA_ROPE = 32
MLA_V_DIM = 64
MLA_Q_RANK = 256
MLA_KV_RANK = 128
MLA_SCALE = (MLA_NOPE + MLA_ROPE) ** -0.5
ROPE_THETA = 10000.0

V7X_LANES = 128
V7X_SUBLANES = 8
V7X_MXU_DIM = 256
V7X_VMEM_LIMIT_BYTES = 56 * 1024 * 1024

_SRC_COLS = dict(
    qkv=(0, 768), z_a=(768, 1024), a_a=(1024, 1028), b_a=(1028, 1032),
    u=(1032, 1288), v=(1288, 1544),
    z_c=(1544, 1800), xbc=(1800, 2568), dt=(2568, 2572),
    c_q=(2572, 2828), c_kv=(2828, 2956), k_pe=(2956, 2988))
_DST_ORDER = ('qkv', 'xbc', 'z_a', 'u', 'v', 'z_c', 'c_q', 'c_kv', 'k_pe', 'a_a', 'b_a', 'dt')
_DST_COLS = {}
_off = 0
for _name in _DST_ORDER:
    _w = _SRC_COLS[_name][1] - _SRC_COLS[_name][0]
    _DST_COLS[_name] = (_off, _off + _w)
    _off += _w
PROJ_USED = _off
PROJ_COLS = -(-PROJ_USED // V7X_LANES) * V7X_LANES
_PERM = np.concatenate([np.arange(*_SRC_COLS[n]) for n in _DST_ORDER])
SMALL_BLOCK = _DST_COLS['k_pe'][0] // V7X_LANES
LANE_A = _DST_COLS['a_a'][0] % V7X_LANES
LANE_B = _DST_COLS['b_a'][0] % V7X_LANES
LANE_DT = _DST_COLS['dt'][0] % V7X_LANES


def _col_block(name):
    lo, hi = _DST_COLS[name]
    assert lo % (hi - lo) == 0
    return lo // (hi - lo)


FFN_CHUNK = V7X_MXU_DIM
TOKEN_TILE = 512
SEQ_TILE = 256
SAMPLE_ROWS = V7X_SUBLANES
NEG_BIG = -1e30


def _cparams(*sem):
    return pltpu.CompilerParams(dimension_semantics=sem, vmem_limit_bytes=V7X_VMEM_LIMIT_BYTES)


def _rms(x, g):
    return x * lax.rsqrt(jnp.mean(x * x, axis=-1, keepdims=True) + EPS) * g


def _silu(x):
    return x * jax.nn.sigmoid(x)


def _softplus(x):
    return jnp.maximum(x, 0.0) + jnp.log(1.0 + jnp.exp(-jnp.abs(x)))


_NN = (((1,), (0,)), ((), ()))
_NT = (((1,), (1,)), ((), ()))
_TN = (((0,), (0,)), ((), ()))


def _mm(a, b, dims=_NN):
    return lax.dot_general(a.astype(BF16), b.astype(BF16), dims, preferred_element_type=F32)


def _split3(x):
    x0 = x.astype(BF16)
    r1 = x - x0.astype(F32)
    x1 = r1.astype(BF16)
    x2 = (r1 - x1.astype(F32)).astype(BF16)
    return x0, x1, x2


def _mm_exact_lhs(a_bf16, b):
    b0, b1, b2 = _split3(b)
    dot = lambda y: lax.dot_general(a_bf16, y, _NN, preferred_element_type=F32)
    return dot(b0) + (dot(b1) + dot(b2))


def _iota2(n, m, axis):
    return lax.broadcasted_iota(jnp.int32, (n, m), axis)


def _resident(shape):
    return pl.BlockSpec(shape, lambda *_: (0,) * len(shape), pipeline_mode=pl.Buffered(1))


def _ffn_body(x, gpre_ref, gpost_ref, win_ref, wout_ref, o_ref, acc_ref):
    xn = _rms(x, gpre_ref[...]).astype(BF16)
    n_chunks = FFN_DIM // FFN_CHUNK
    for c in range(n_chunks):
        lo = c * FFN_CHUNK
        gate = jnp.dot(xn, win_ref[:, lo:lo + FFN_CHUNK], preferred_element_type=F32)
        up = jnp.dot(xn, win_ref[:, FFN_DIM + lo:FFN_DIM + lo + FFN_CHUNK], preferred_element_type=F32)
        h = (_silu(gate) * up).astype(BF16)
        part = jnp.dot(h, wout_ref[lo:lo + FFN_CHUNK, :], preferred_element_type=F32)
        if c == 0:
            acc_ref[...] = part
        else:
            acc_ref[...] += part
    o_ref[...] = x + 0.5 * _rms(acc_ref[...], gpost_ref[...])


def _ffn_kernel(x_ref, gpre_ref, gpost_ref, win_ref, wout_ref, o_ref, acc_ref):
    _ffn_body(x_ref[...], gpre_ref, gpost_ref, win_ref, wout_ref, o_ref, acc_ref)


def _mix_ffn_kernel(x_ref, ma_ref, mb_ref, mc_ref, md_ref, wo_ref, gmix_ref, gpre_ref, gpost_ref,
                    win_ref, wout_ref, o_ref, acc_ref):
    y = None
    for i, m_ref in enumerate((ma_ref, mb_ref, mc_ref, md_ref)):
        part = jnp.dot(m_ref[...].astype(BF16), wo_ref[i * GROUP_W:(i + 1) * GROUP_W, :],
                       preferred_element_type=F32)
        y = part if y is None else y + part
    x = x_ref[...] + _rms(y, gmix_ref[...])
    _ffn_body(x, gpre_ref, gpost_ref, win_ref, wout_ref, o_ref, acc_ref)


def _ffn(x, g_pre, g_post, w_in, w_out, mix=None, w_o=None, g_mix=None):
    t = x.shape[0]
    tm = min(TOKEN_TILE, t)
    assert t % tm == 0
    tok = pl.BlockSpec((tm, D_MODEL), lambda i: (i, 0))
    vec = _resident((1, D_MODEL))
    w_specs = [_resident((D_MODEL, 2 * FFN_DIM)), _resident((FFN_DIM, D_MODEL))]
    if mix is None:
        kern, args = _ffn_kernel, (x, g_pre, g_post, w_in, w_out)
        in_specs = [tok, vec, vec] + w_specs
    else:
        kern, args = _mix_ffn_kernel, (x, *mix, w_o, g_mix, g_pre, g_post, w_in, w_out)
        part = pl.BlockSpec((tm, GROUP_W), lambda i: (i, 0))
        in_specs = [tok, part, part, part, part, _resident((D_MODEL, D_MODEL)), vec, vec, vec] + w_specs
    return pl.pallas_call(
        kern,
        grid=(t // tm,),
        in_specs=in_specs,
        out_specs=tok,
        out_shape=jax.ShapeDtypeStruct((t, D_MODEL), F32),
        scratch_shapes=[pltpu.VMEM((tm, D_MODEL), F32)],
        compiler_params=_cparams("arbitrary"),
        name="mix_ffn" if mix is not None else "ffn",
    )(*args)


def _proj_kernel(x_ref, g_ref, w_ref, o_ref):
    xn = _rms(x_ref[...], g_ref[...]).astype(BF16)
    o_ref[...] = jnp.dot(xn, w_ref[...], preferred_element_type=F32)


def _proj(x, g, w):
    t = x.shape[0]
    tm = min(TOKEN_TILE, t)
    assert t % tm == 0
    return pl.pallas_call(
        _proj_kernel,
        grid=(t // tm,),
        in_specs=[pl.BlockSpec((tm, D_MODEL), lambda i: (i, 0)),
                  _resident((1, D_MODEL)), _resident((D_MODEL, PROJ_COLS))],
        out_specs=pl.BlockSpec((tm, PROJ_COLS), lambda i: (i, 0)),
        out_shape=jax.ShapeDtypeStruct((t, PROJ_COLS), F32),
        compiler_params=_cparams("arbitrary"),
        name="in_proj",
    )(x, g, w)


def _seq_spec(rows, name, n_steps):
    lo, hi = _DST_COLS[name]
    return pl.BlockSpec((rows, hi - lo), lambda b, n, c=_col_block(name): (b * n_steps + n, c))


def _small_spec(rows, n_steps):
    return pl.BlockSpec((rows, V7X_LANES), lambda b, n: (b * n_steps + n, SMALL_BLOCK))


def _out_spec(rows, width, n_steps):
    return pl.BlockSpec((rows, width), lambda b, n: (b * n_steps + n, 0))


def _causal_conv(x, halo_ref, conv0_ref, w_ref, rows):
    width = x.shape[1]
    first = V7X_SUBLANES - (CONV_K - 1)

    @pl.when(pl.program_id(1) == 0)
    def _():
        halo_ref[0:V7X_SUBLANES, :] = jnp.zeros((V7X_SUBLANES, width), F32)
        halo_ref[first:V7X_SUBLANES, :] = conv0_ref[0]

    halo_ref[V7X_SUBLANES:V7X_SUBLANES + rows, :] = x
    y = w_ref[0:1, :] * halo_ref[first:first + rows, :]
    for i in range(1, CONV_K):
        y = y + w_ref[i:i + 1, :] * halo_ref[first + i:first + i + rows, :]
    halo_ref[0:V7X_SUBLANES, :] = x[rows - V7X_SUBLANES:rows, :]
    return y


def _level_masks(rows):
    i = np.arange(rows)[:, None]
    j = np.arange(rows)[None, :]
    out = []
    s = 1
    while s < rows:
        out.append(((i // (2 * s) == j // (2 * s)) & (i % (2 * s) >= s) & (j % (2 * s) < s)).astype(np.float32))
        s *= 2
    return np.stack(out)


def _unit_lower_inverse(a_strict, lv_ref, eye):
    t = eye - a_strict * lv_ref[0]
    for lv in range(1, lv_ref.shape[0]):
        a_s = a_strict * lv_ref[lv]
        t = t - _mm(_mm(t, a_s), t)
    return t


def _gdn_kernel(qkv_ref, z_ref, sm_ref, conv0_ref, s0_ref, cw_ref, lane_ref, g_ref, lv_ref,
                o_ref, sfin_ref, halo_ref, s_ref, *, rows, n_valid):
    @pl.when(pl.program_id(1) == 0)
    def _():
        s_ref[...] = s0_ref[0]

    act = _silu(_causal_conv(qkv_ref[...], halo_ref, conv0_ref, cw_ref, rows))
    sm = sm_ref[...]
    row = _iota2(rows, rows, 0)
    col = _iota2(rows, rows, 1)
    tril = col <= row
    eye = (col == row).astype(F32)
    valid = _iota2(rows, V7X_LANES, 0) < n_valid
    log_alpha = jnp.where(valid, -jnp.exp(lane_ref[0:1, :]) * _softplus(sm + lane_ref[1:2, :]), 0.0)
    beta = jnp.where(valid, jax.nn.sigmoid(sm), 0.0)
    gcum = _mm_exact_lhs(tril.astype(BF16), log_alpha)
    gcum_t = gcum.T
    z = z_ref[...]
    for h in range(GDN_HEADS):
        sl = slice(h * GDN_DK, (h + 1) * GDN_DK)
        q = act[:, sl]
        k = act[:, GROUP_W + h * GDN_DK:GROUP_W + (h + 1) * GDN_DK]
        v = act[:, 2 * GROUP_W + h * GDN_DV:2 * GROUP_W + (h + 1) * GDN_DV]
        q = q * lax.rsqrt(jnp.sum(q * q, axis=-1, keepdims=True) + EPS) * GDN_DK ** -0.5
        k = k * lax.rsqrt(jnp.sum(k * k, axis=-1, keepdims=True) + EPS)
        gcol = gcum[:, LANE_A + h:LANE_A + h + 1]
        grow = gcum_t[LANE_A + h:LANE_A + h + 1, :]
        bcol = beta[:, LANE_B + h:LANE_B + h + 1]
        decay = jnp.exp(jnp.where(tril, gcol - grow, NEG_BIG))
        kb = k * bcol
        a_strict = jnp.where(col < row, _mm(kb, k, _NT) * decay, 0.0)
        t_inv = _unit_lower_inverse(a_strict, lv_ref, eye)
        eg = jnp.exp(gcol)
        sol = _mm(t_inv, jnp.concatenate([kb * eg, v * bcol], axis=1))
        w_c, u_c = sol[:, :GDN_DK], sol[:, GDN_DK:]
        s = s_ref[h]
        u_new = u_c - _mm(w_c, s)
        qk = _mm(q, k, _NT) * decay
        o = _mm(q * eg, s) + _mm(qk, u_new)
        glast = gcum[rows - 1:rows, LANE_A + h:LANE_A + h + 1]
        s_new = s * jnp.exp(glast) + _mm(k * jnp.exp(glast - gcol), u_new, _TN)
        s_ref[h] = s_new
        sfin_ref[0, h] = s_new
        o_ref[:, sl] = _rms(o, g_ref[...]) * _silu(z[:, sl])


def _gdn(p, conv0, s0, conv_w, a_log, dt_bias, norm_g, n_seq, seq_len, rows, n_valid):
    n_steps = seq_len // rows
    lane = jnp.zeros((2, V7X_LANES), F32)
    lane = lane.at[0, LANE_A:LANE_A + GDN_HEADS].set(a_log).at[1, LANE_A:LANE_A + GDN_HEADS].set(dt_bias)
    levels = jnp.asarray(_level_masks(rows))
    return pl.pallas_call(
        functools.partial(_gdn_kernel, rows=rows, n_valid=n_valid),
        grid=(n_seq, n_steps),
        in_specs=[_seq_spec(rows, 'qkv', n_steps), _seq_spec(rows, 'z_a', n_steps), _small_spec(rows, n_steps),
                  pl.BlockSpec((1, CONV_K - 1, 3 * GROUP_W), lambda b, n: (b, 0, 0)),
                  pl.BlockSpec((1, GDN_HEADS, GDN_DK, GDN_DV), lambda b, n: (b, 0, 0, 0)),
                  _resident((CONV_K, 3 * GROUP_W)), _resident((2, V7X_LANES)), _resident((1, GDN_DV)),
                  _resident(levels.shape)],
        out_specs=[_out_spec(rows, GROUP_W, n_steps),
                   pl.BlockSpec((1, GDN_HEADS, GDN_DK, GDN_DV), lambda b, n: (b, 0, 0, 0))],
        out_shape=[jax.ShapeDtypeStruct((n_seq * seq_len, GROUP_W), F32),
                   jax.ShapeDtypeStruct((n_seq, GDN_HEADS, GDN_DK, GDN_DV), F32)],
        scratch_shapes=[pltpu.VMEM((V7X_SUBLANES + rows, 3 * GROUP_W), F32),
                        pltpu.VMEM((GDN_HEADS, GDN_DK, GDN_DV), F32)],
        compiler_params=_cparams("arbitrary", "arbitrary"),
        name="gdn",
    )(p, p, p, conv0, s0, conv_w, lane, norm_g.reshape(1, GDN_DV), levels)


def _gmlp_kernel(u_ref, v_ref, lng_ref, lnb_ref, ws_ref, bias_ref, o_ref, vb_ref, *, rows, chunk):
    u = jax.nn.gelu(u_ref[...], approximate=True)
    v = jax.nn.gelu(v_ref[...], approximate=True)
    mu = jnp.mean(v, axis=-1, keepdims=True)
    var = jnp.mean(jnp.square(v - mu), axis=-1, keepdims=True)
    v = (v - mu) * lax.rsqrt(var + EPS) * lng_ref[...] + lnb_ref[...]
    vb_ref[...] = v
    tril = _iota2(chunk, chunk, 1) <= _iota2(chunk, chunk, 0)
    ws = [jnp.where(tril, ws_ref[g], 0.0).astype(BF16) for g in range(MLP_GROUPS)]
    for c in range(rows // chunk):
        rs = slice(c * chunk, (c + 1) * chunk)
        for g in range(MLP_GROUPS):
            ls = slice(g * MLP_GW, (g + 1) * MLP_GW)
            mixed = _mm(ws[g], v[rs, ls]) + bias_ref[:, ls]
            o_ref[rs, ls] = u[rs, ls] * mixed


def _gmlp(p, ln_g, ln_b, ws, bs, n_seq, seq_len, rows, chunk):
    n_steps = seq_len // rows
    bias = jnp.repeat(bs[:, :chunk].T, MLP_GW, axis=1)
    return pl.pallas_call(
        functools.partial(_gmlp_kernel, rows=rows, chunk=chunk),
        grid=(n_seq, n_steps),
        in_specs=[_seq_spec(rows, 'u', n_steps), _seq_spec(rows, 'v', n_steps),
                  _resident((1, GROUP_W)), _resident((1, GROUP_W)),
                  _resident((MLP_GROUPS, chunk, chunk)), _resident((chunk, GROUP_W))],
        out_specs=[_out_spec(rows, GROUP_W, n_steps), _out_spec(rows, GROUP_W, n_steps)],
        out_shape=[jax.ShapeDtypeStruct((n_seq * seq_len, GROUP_W), F32)] * 2,
        compiler_params=_cparams("arbitrary", "arbitrary"),
        name="gmlp",
    )(p, p, ln_g.reshape(1, GROUP_W), ln_b.reshape(1, GROUP_W), ws[:, :chunk, :chunk], bias)


def _ssd_kernel(xbc_ref, z_ref, sm_ref, conv0_ref, h0_ref, cw_ref, cb_ref, lane_ref, g_ref,
                o_ref, hfin_ref, halo_ref, h_ref, y_ref, *, rows, n_valid):
    @pl.when(pl.program_id(1) == 0)
    def _():
        h_ref[...] = h0_ref[0]

    act = _silu(_causal_conv(xbc_ref[...], halo_ref, conv0_ref, cw_ref, rows) + cb_ref[...])
    sm = sm_ref[...]
    tril = _iota2(rows, rows, 1) <= _iota2(rows, rows, 0)
    valid = _iota2(rows, V7X_LANES, 0) < n_valid
    dt = jnp.where(valid, _softplus(sm + lane_ref[1:2, :]), 0.0)
    acum = _mm_exact_lhs(tril.astype(BF16), dt * -jnp.exp(lane_ref[0:1, :]))
    acum_t = acum.T
    heads_per_group = SSM_HEADS // SSM_GROUPS
    for grp in range(SSM_GROUPS):
        b_g = act[:, GROUP_W + grp * SSM_STATE:GROUP_W + (grp + 1) * SSM_STATE]
        c_g = act[:, GROUP_W + (SSM_GROUPS + grp) * SSM_STATE:GROUP_W + (SSM_GROUPS + grp + 1) * SSM_STATE]
        cb = _mm(c_g, b_g, _NT)
        for hh in range(heads_per_group):
            h = grp * heads_per_group + hh
            lane = LANE_DT + h
            sl = slice(h * SSM_HEAD_DIM, (h + 1) * SSM_HEAD_DIM)
            acol = acum[:, lane:lane + 1]
            arow = acum_t[lane:lane + 1, :]
            decay = jnp.exp(jnp.where(tril, acol - arow, NEG_BIG))
            x = act[:, sl]
            xdt = x * dt[:, lane:lane + 1]
            state = h_ref[h]
            y = _mm(cb * decay, xdt) + _mm(c_g * jnp.exp(acol), state, _NT)
            alast = acum[rows - 1:rows, lane:lane + 1]
            h_new = state * jnp.exp(alast) + _mm(xdt, b_g * jnp.exp(alast - acol), _TN)
            h_ref[h] = h_new
            hfin_ref[0, h] = h_new
            y_ref[:, sl] = y + lane_ref[2:3, lane:lane + 1] * x
    o_ref[...] = _rms(y_ref[...] * _silu(z_ref[...]), g_ref[...])


def _ssd(p, conv0, h0, conv_w, conv_b, a_log, dt_bias, d_skip, norm_g, n_seq, seq_len, rows, n_valid):
    n_steps = seq_len // rows
    lane = jnp.zeros((3, V7X_LANES), F32)
    lane = (lane.at[0, LANE_DT:LANE_DT + SSM_HEADS].set(a_log)
            .at[1, LANE_DT:LANE_DT + SSM_HEADS].set(dt_bias)
            .at[2, LANE_DT:LANE_DT + SSM_HEADS].set(d_skip))
    return pl.pallas_call(
        functools.partial(_ssd_kernel, rows=rows, n_valid=n_valid),
        grid=(n_seq, n_steps),
        in_specs=[_seq_spec(rows, 'xbc', n_steps), _seq_spec(rows, 'z_c', n_steps), _small_spec(rows, n_steps),
                  pl.BlockSpec((1, CONV_K - 1, SSM_CONV_W), lambda b, n: (b, 0, 0)),
                  pl.BlockSpec((1, SSM_HEADS, SSM_HEAD_DIM, SSM_STATE), lambda b, n: (b, 0, 0, 0)),
                  _resident((CONV_K, SSM_CONV_W)), _resident((1, SSM_CONV_W)), _resident((3, V7X_LANES)),
                  _resident((1, GROUP_W))],
        out_specs=[_out_spec(rows, GROUP_W, n_steps),
                   pl.BlockSpec((1, SSM_HEADS, SSM_HEAD_DIM, SSM_STATE), lambda b, n: (b, 0, 0, 0))],
        out_shape=[jax.ShapeDtypeStruct((n_seq * seq_len, GROUP_W), F32),
                   jax.ShapeDtypeStruct((n_seq, SSM_HEADS, SSM_HEAD_DIM, SSM_STATE), F32)],
        scratch_shapes=[pltpu.VMEM((V7X_SUBLANES + rows, SSM_CONV_W), F32),
                        pltpu.VMEM((SSM_HEADS, SSM_HEAD_DIM, SSM_STATE), F32),
                        pltpu.VMEM((rows, GROUP_W), F32)],
        compiler_params=_cparams("arbitrary", "arbitrary"),
        name="ssd",
    )(p, p, p, conv0, h0, conv_w, conv_b.reshape(1, SSM_CONV_W), lane, norm_g.reshape(1, GROUP_W))


def _rope_lanes(x, cos, sin_signed):
    lane = _iota2(x.shape[0], V7X_LANES, 1)
    half = MLA_ROPE // 2
    swapped = jnp.where(lane % MLA_ROPE < half,
                        pltpu.roll(x, V7X_LANES - half, axis=1), pltpu.roll(x, half, axis=1))
    return x * cos + swapped * sin_signed


def _rope_tables(pos):
    half = MLA_ROPE // 2
    inv_freq = ROPE_THETA ** (-jnp.arange(half, dtype=F32) / half)
    ang = pos.astype(F32)[:, None] * inv_freq[None, :]
    cos, sin = jnp.cos(ang), jnp.sin(ang)
    reps = V7X_LANES // MLA_ROPE
    return jnp.tile(jnp.concatenate([cos, cos], axis=1), (1, reps)), jnp.tile(jnp.concatenate([-sin, sin], axis=1), (1, reps))


def _mla_project(cq_ref, ckv_ref, sm_ref, cos_ref, sin_ref, gq_ref, gkv_ref, wn_ref, wpe_ref, wuk_ref):
    c_q = _rms(cq_ref[...], gq_ref[...]).astype(BF16)
    c_kv = _rms(ckv_ref[...], gkv_ref[...])
    cos, sin = cos_ref[...], sin_ref[...]
    lane = _iota2(c_kv.shape[0], V7X_LANES, 1)
    k_pe = _rope_lanes(jnp.where(lane < MLA_ROPE, sm_ref[...], 0.0), cos, sin)
    q_nope = jnp.dot(c_q, wn_ref[...], preferred_element_type=F32)
    q_heads = []
    for h in range(MLA_HEADS):
        q_lat = _mm(q_nope[:, h * MLA_NOPE:(h + 1) * MLA_NOPE], wuk_ref[h], _NT)
        q_pe = _rope_lanes(jnp.dot(c_q, wpe_ref[h], preferred_element_type=F32), cos, sin)
        q_heads.append((jnp.concatenate([q_lat, q_pe], axis=1) * MLA_SCALE).astype(BF16))
    return c_kv, k_pe, q_heads


def _softmax_step(q, keys, m, l, acc, mask=None):
    s = lax.dot_general(q, keys, _NT, preferred_element_type=F32)
    if mask is not None:
        s = jnp.where(mask, s, NEG_BIG)
    m_new = jnp.maximum(m, jnp.max(s, axis=-1, keepdims=True))
    p = jnp.exp(s - m_new)
    alpha = jnp.exp(m - m_new)
    l = alpha * l + jnp.sum(p, axis=-1, keepdims=True)
    acc = alpha * acc + jnp.dot(p.astype(BF16), keys[:, :MLA_KV_RANK], preferred_element_type=F32)
    return m_new, l, acc


def _mla_prompt_kernel(cq_ref, ckv_ref, sm_ref, cos_ref, sin_ref, gq_ref, gkv_ref, wn_ref, wpe_ref, wuk_ref,
                       wuv_ref, o_ref, ckv_out_ref, kpe_out_ref, keys_ref, *, rows):
    n = pl.program_id(1)
    c_kv, k_pe, q_heads = _mla_project(cq_ref, ckv_ref, sm_ref, cos_ref, sin_ref, gq_ref, gkv_ref,
                                       wn_ref, wpe_ref, wuk_ref)
    ckv_out_ref[0] = c_kv
    kpe_out_ref[0] = k_pe[:, :MLA_ROPE]
    base = pl.multiple_of(n * rows, rows)
    keys_ref[pl.ds(base, rows), :] = jnp.concatenate([c_kv, k_pe], axis=1).astype(BF16)
    causal = _iota2(rows, rows, 1) <= _iota2(rows, rows, 0)
    for h in range(MLA_HEADS):
        q = q_heads[h]

        def body(j, carry, q=q):
            keys = keys_ref[pl.ds(pl.multiple_of(j * rows, rows), rows), :]
            return _softmax_step(q, keys, *carry)

        init = (jnp.full((rows, 1), NEG_BIG, F32), jnp.zeros((rows, 1), F32), jnp.zeros((rows, MLA_KV_RANK), F32))
        m, l, acc = lax.fori_loop(0, n, body, init)
        m, l, acc = _softmax_step(q, keys_ref[pl.ds(base, rows), :], m, l, acc, mask=causal)
        o_ref[:, h * MLA_V_DIM:(h + 1) * MLA_V_DIM] = _mm(acc / l, wuv_ref[h])


def _mla_weights(w_uq, w_uk, w_uv):
    w4 = w_uq.reshape(MLA_Q_RANK, MLA_HEADS, MLA_NOPE + MLA_ROPE)
    w_nope = w4[:, :, :MLA_NOPE].reshape(MLA_Q_RANK, MLA_HEADS * MLA_NOPE).astype(BF16)
    w_pe = jnp.pad(w4[:, :, MLA_NOPE:].transpose(1, 0, 2),
                   ((0, 0), (0, 0), (0, V7X_LANES - MLA_ROPE))).astype(BF16)
    return w_nope, w_pe, w_uk.astype(BF16), w_uv.astype(BF16)


def _mla_prompt(p, q_norm_g, kv_norm_g, w_nope, w_pe, w_uk, w_uv, n_seq, seq_len, rows):
    n_steps = seq_len // rows
    cos, sin = _rope_tables(jnp.arange(seq_len))
    tab = pl.BlockSpec((rows, V7X_LANES), lambda b, n: (n, 0))
    return pl.pallas_call(
        functools.partial(_mla_prompt_kernel, rows=rows),
        grid=(n_seq, n_steps),
        in_specs=[_seq_spec(rows, 'c_q', n_steps), _seq_spec(rows, 'c_kv', n_steps), _small_spec(rows, n_steps),
                  tab, tab, _resident((1, MLA_Q_RANK)), _resident((1, MLA_KV_RANK)),
                  _resident(w_nope.shape), _resident(w_pe.shape), _resident(w_uk.shape), _resident(w_uv.shape)],
        out_specs=[_out_spec(rows, GROUP_W, n_steps),
                   pl.BlockSpec((1, rows, MLA_KV_RANK), lambda b, n: (b, n, 0)),
                   pl.BlockSpec((1, rows, MLA_ROPE), lambda b, n: (b, n, 0))],
        out_shape=[jax.ShapeDtypeStruct((n_seq * seq_len, GROUP_W), F32),
                   jax.ShapeDtypeStruct((n_seq, seq_len, MLA_KV_RANK), F32),
                   jax.ShapeDtypeStruct((n_seq, seq_len, MLA_ROPE), F32)],
        scratch_shapes=[pltpu.VMEM((seq_len, 2 * V7X_LANES), BF16)],
        compiler_params=_cparams("arbitrary", "arbitrary"),
        name="mla_prompt",
    )(p, p, p, cos, sin, q_norm_g.reshape(1, MLA_Q_RANK), kv_norm_g.reshape(1, MLA_KV_RANK),
      w_nope, w_pe, w_uk, w_uv)


def _mla_decode_kernel(pt_ref, cq_ref, ckv_ref, sm_ref, cos_ref, sin_ref, gq_ref, gkv_ref, wn_ref, wpe_ref,
                       wuk_ref, wuv_ref, *rest, rows, pages_per_step):
    page_refs = rest[:2 * pages_per_step]
    o_ref, ckv_out_ref, kpe_out_ref, q_ref, knew_ref, s_ref, lat_ref = rest[2 * pages_per_step:]
    j = pl.program_id(1)
    qrows = MLA_HEADS * rows
    past_len = s_ref.shape[1]

    @pl.when(j == 0)
    def _():
        c_kv, k_pe, q_heads = _mla_project(cq_ref, ckv_ref, sm_ref, cos_ref, sin_ref, gq_ref, gkv_ref,
                                           wn_ref, wpe_ref, wuk_ref)
        ckv_out_ref[0] = c_kv
        kpe_out_ref[0] = k_pe[:, :MLA_ROPE]
        knew_ref[...] = jnp.concatenate([c_kv, k_pe], axis=1).astype(BF16)
        q_ref[...] = jnp.concatenate(q_heads, axis=0)

    q = q_ref[...]
    q_lat, q_pe = q[:, :MLA_KV_RANK], q[:, MLA_KV_RANK:MLA_KV_RANK + MLA_ROPE]
    for g in range(pages_per_step):
        lat = page_refs[g][...].astype(BF16)
        s = (lax.dot_general(q_lat, lat, _NT, preferred_element_type=F32)
             + lax.dot_general(q_pe, page_refs[pages_per_step + g][...].astype(BF16), _NT,
                               preferred_element_type=F32))
        base = pl.multiple_of((j * pages_per_step + g) * PAGE_SIZE, PAGE_SIZE)
        s_ref[:, pl.ds(base, PAGE_SIZE)] = s
        lat_ref[pl.ds(base, PAGE_SIZE), :] = lat

    @pl.when(j == pl.num_programs(1) - 1)
    def _():
        knew = knew_ref[...]
        causal = _iota2(qrows, rows, 1) <= _iota2(qrows, rows, 0) % rows
        s_new = jnp.where(causal, lax.dot_general(q, knew, _NT, preferred_element_type=F32), NEG_BIG)
        chunk = min(past_len, 16 * PAGE_SIZE)
        m = jnp.max(s_new, axis=-1, keepdims=True)
        for c in range(past_len // chunk):
            m = jnp.maximum(m, jnp.max(s_ref[:, c * chunk:(c + 1) * chunk], axis=-1, keepdims=True))
        p_new = jnp.exp(s_new - m)
        l = jnp.sum(p_new, axis=-1, keepdims=True)
        acc = jnp.dot(p_new.astype(BF16), knew[:, :MLA_KV_RANK], preferred_element_type=F32)
        for c in range(past_len // chunk):
            pr = jnp.exp(s_ref[:, c * chunk:(c + 1) * chunk] - m)
            l = l + jnp.sum(pr, axis=-1, keepdims=True)
            acc = acc + jnp.dot(pr.astype(BF16), lat_ref[c * chunk:(c + 1) * chunk, :],
                                preferred_element_type=F32)
        o_lat = acc / l
        for h in range(MLA_HEADS):
            o_ref[:, h * MLA_V_DIM:(h + 1) * MLA_V_DIM] = _mm(o_lat[h * rows:(h + 1) * rows], wuv_ref[h])


def _mla_decode(p, cache_ckv, cache_kpe, layer, page_table, q_norm_g, kv_norm_g, w_nope, w_pe, w_uk, w_uv,
                n_seq, rows, pages_per_step=32):
    n_pages = page_table.shape[1]
    assert n_pages % pages_per_step == 0
    n_steps = n_pages // pages_per_step
    past_len = n_pages * PAGE_SIZE
    cos, sin = _rope_tables(past_len + jnp.arange(rows))

    def const(shape):
        return pl.BlockSpec(shape, lambda s, j, pt: (0,) * len(shape))

    def seq(name):
        lo, hi = _DST_COLS[name]
        return pl.BlockSpec((rows, hi - lo), lambda s, j, pt, c=_col_block(name): (s, c))

    def page(width, g):
        return pl.BlockSpec((None, None, PAGE_SIZE, width),
                            lambda s, j, pt, g=g: (layer, pt[s, j * pages_per_step + g], 0, 0))

    in_specs = ([seq('c_q'), seq('c_kv'), pl.BlockSpec((rows, V7X_LANES), lambda s, j, pt: (s, SMALL_BLOCK)),
                 const((rows, V7X_LANES)), const((rows, V7X_LANES)),
                 const((1, MLA_Q_RANK)), const((1, MLA_KV_RANK)),
                 const(w_nope.shape), const(w_pe.shape), const(w_uk.shape), const(w_uv.shape)]
                + [page(MLA_KV_RANK, g) for g in range(pages_per_step)]
                + [page(MLA_ROPE, g) for g in range(pages_per_step)])
    qrows = MLA_HEADS * rows
    return pl.pallas_call(
        functools.partial(_mla_decode_kernel, rows=rows, pages_per_step=pages_per_step),
        grid_spec=pltpu.PrefetchScalarGridSpec(
            num_scalar_prefetch=1,
            grid=(n_seq, n_steps),
            in_specs=in_specs,
            out_specs=[pl.BlockSpec((rows, GROUP_W), lambda s, j, pt: (s, 0)),
                       pl.BlockSpec((1, rows, MLA_KV_RANK), lambda s, j, pt: (s, 0, 0)),
                       pl.BlockSpec((1, rows, MLA_ROPE), lambda s, j, pt: (s, 0, 0))],
            scratch_shapes=[pltpu.VMEM((qrows, 2 * V7X_LANES), BF16),
                            pltpu.VMEM((rows, 2 * V7X_LANES), BF16),
                            pltpu.VMEM((qrows, past_len), F32),
                            pltpu.VMEM((past_len, MLA_KV_RANK), BF16)]),
        out_shape=[jax.ShapeDtypeStruct((n_seq * rows, GROUP_W), F32),
                   jax.ShapeDtypeStruct((n_seq, rows, MLA_KV_RANK), F32),
                   jax.ShapeDtypeStruct((n_seq, rows, MLA_ROPE), F32)],
        compiler_params=_cparams("arbitrary", "arbitrary"),
        name="mla_decode",
    )(page_table, p, p, p, cos, sin, q_norm_g.reshape(1, MLA_Q_RANK), kv_norm_g.reshape(1, MLA_KV_RANK),
      w_nope, w_pe, w_uk, w_uv, *([cache_ckv] * pages_per_step), *([cache_kpe] * pages_per_step))


def _last_rows(p, n_seq, seq_len, n_valid, name):
    lo, hi = _DST_COLS[name]
    return p.reshape(n_seq, seq_len, PROJ_COLS)[:, n_valid - (CONV_K - 1):n_valid, lo:hi]


def _trunk_layer(x, w, layer, n_seq, seq_len, rows, n_valid, gdn_conv, gdn_s, ssm_conv, ssm_h, paged):
    g = w['norm_g']
    assert n_valid == seq_len or seq_len == rows
    step_valid = min(n_valid, rows)
    x = _ffn(x, g[0:1], g[1:2], w['ffn_w_in'][0], w['ffn_w_out'][0])
    p = _proj(x, g[2:3], w['w_in'])
    out_a, gdn_s_new = _gdn(p, gdn_conv, gdn_s, w['gdn_conv_w'], w['gdn_a_log'], w['gdn_dt_bias'],
                            w['gdn_norm_g'], n_seq, seq_len, rows, step_valid)
    chunk = min(MLP_CHUNK, rows)
    out_b, v_b = _gmlp(p, w['mlp_ln_g'], w['mlp_ln_b'], w['mlp_ws'], w['mlp_bs'], n_seq, seq_len, rows, chunk)
    out_c, ssm_h_new = _ssd(p, ssm_conv, ssm_h, w['ssm_conv_w'], w['ssm_conv_b'], w['ssm_a_log'],
                            w['ssm_dt_bias'], w['ssm_d'], w['ssm_norm_g'], n_seq, seq_len, rows, step_valid)
    mla_w = (w['mla_q_norm_g'], w['mla_kv_norm_g'], w['mla_w_nope'], w['mla_w_pe'], w['mla_w_uk'], w['mla_w_uv'])
    if paged is None:
        out_d, c_kv, k_pe = _mla_prompt(p, *mla_w, n_seq, seq_len, rows)
    else:
        cache_ckv, cache_kpe, page_table = paged
        out_d, c_kv, k_pe = _mla_decode(p, cache_ckv, cache_kpe, layer, page_table, *mla_w, n_seq, rows)
    x = _ffn(x, g[4:5], g[5:6], w['ffn_w_in'][1], w['ffn_w_out'][1],
             mix=(out_a, out_b, out_c, out_d), w_o=w['w_out'], g_mix=g[3:4])
    gdn_conv_new = _last_rows(p, n_seq, seq_len, n_valid, 'qkv')
    ssm_conv_new = _last_rows(p, n_seq, seq_len, n_valid, 'xbc')
    v_b = v_b.reshape(n_seq, seq_len, GROUP_W)[:, :n_valid]
    return x, (c_kv[:, :n_valid], k_pe[:, :n_valid], gdn_s_new, gdn_conv_new, ssm_h_new, ssm_conv_new, v_b)


def kernel(x_prompt, x_sample, cache_ckv, cache_kpe, page_table, state_gdn_s, state_gdn_conv,
           state_ssm_h, state_ssm_conv, norm_g, ffn_w_in, ffn_w_out, w_in, w_out,
           gdn_conv_w, gdn_a_log, gdn_dt_bias, gdn_norm_g, mlp_ln_g, mlp_ln_b, mlp_ws, mlp_bs,
           ssm_conv_w, ssm_conv_b, ssm_a_log, ssm_dt_bias, ssm_d, ssm_norm_g,
           mla_q_norm_g, mla_w_uq, mla_kv_norm_g, mla_w_uk, mla_w_uv):
    bp, lp, _ = x_prompt.shape
    bs, ls, _ = x_sample.shape
    assert lp % SEQ_TILE == 0 and CONV_K - 1 <= ls <= SAMPLE_ROWS
    w_in_p = jnp.pad(w_in[:, :, _PERM], ((0, 0), (0, 0), (0, PROJ_COLS - PROJ_USED)))
    weights = dict(norm_g=norm_g, ffn_w_in=ffn_w_in.astype(BF16), ffn_w_out=ffn_w_out.astype(BF16),
                   w_in=w_in_p.astype(BF16), w_out=w_out.astype(BF16),
                   gdn_conv_w=gdn_conv_w, gdn_a_log=gdn_a_log, gdn_dt_bias=gdn_dt_bias, gdn_norm_g=gdn_norm_g,
                   mlp_ln_g=mlp_ln_g, mlp_ln_b=mlp_ln_b, mlp_ws=mlp_ws, mlp_bs=mlp_bs,
                   ssm_conv_w=ssm_conv_w, ssm_conv_b=ssm_conv_b, ssm_a_log=ssm_a_log,
                   ssm_dt_bias=ssm_dt_bias, ssm_d=ssm_d, ssm_norm_g=ssm_norm_g,
                   mla_q_norm_g=mla_q_norm_g, mla_kv_norm_g=mla_kv_norm_g)
    xp = x_prompt.reshape(bp * lp, D_MODEL)
    xs = jnp.pad(x_sample, ((0, 0), (0, SAMPLE_ROWS - ls), (0, 0))).reshape(bs * SAMPLE_ROWS, D_MODEL)
    zeros_p = (jnp.zeros((bp, CONV_K - 1, 3 * GROUP_W), F32), jnp.zeros((bp, GDN_HEADS, GDN_DK, GDN_DV), F32),
               jnp.zeros((bp, CONV_K - 1, SSM_CONV_W), F32), jnp.zeros((bp, SSM_HEADS, SSM_HEAD_DIM, SSM_STATE), F32))
    st_p, st_s = [], []
    for l in range(DEPTH):
        wl = {name: arr[l] for name, arr in weights.items()}
        wl['mla_w_nope'], wl['mla_w_pe'], wl['mla_w_uk'], wl['mla_w_uv'] = _mla_weights(
            mla_w_uq[l], mla_w_uk[l], mla_w_uv[l])
        xp, st = _trunk_layer(xp, wl, l, bp, lp, SEQ_TILE, lp, *zeros_p, None)
        st_p.append(st)
        xs, st = _trunk_layer(xs, wl, l, bs, SAMPLE_ROWS, SAMPLE_ROWS, ls,
                              state_gdn_conv[l], state_gdn_s[l], state_ssm_conv[l], state_ssm_h[l],
                              (cache_ckv, cache_kpe, page_table))
        st_s.append(st)

    def stack(states, i):
        return jnp.stack([s[i] for s in states])

    y_prompt = xp.reshape(bp, lp, D_MODEL)
    y_sample = xs.reshape(bs, SAMPLE_ROWS, D_MODEL)[:, :ls]
    return (y_prompt, y_sample,
            stack(st_p, 0), stack(st_p, 1), stack(st_p, 2), stack(st_p, 3), stack(st_p, 4), stack(st_p, 5),
            stack(st_s, 0), stack(st_s, 1), stack(st_s, 2), stack(st_s, 3), stack(st_s, 4), stack(st_s, 5),
            stack(st_s, 6))
```

```python
import functools

import numpy as np
import jax
import jax.numpy as jnp
from jax import lax
from jax.experimental import pallas as pl
from jax.experimental.pallas import tpu as pltpu

F32 = jnp.float32
BF16 = jnp.bfloat16

D_MODEL = 1024
DEPTH = 2
PAGE_SIZE = 128
GROUP_W = 256
CONV_K = 4
FFN_DIM = 2816
EPS = 1e-6

GDN_HEADS = 4
GDN_DK = 64
GDN_DV = 64
MLP_CHUNK = 128
MLP_GROUPS = 4
MLP_GW = 64
SSM_HEADS = 4
SSM_HEAD_DIM = 64
SSM_GROUPS = 2
SSM_STATE = 128
SSM_CONV_W = 768
MLA_HEADS = 4
MLA_NOPE = 64
MLA_ROPE = 32
MLA_V_DIM = 64
MLA_Q_RANK = 256
MLA_KV_RANK = 128
MLA_SCALE = (MLA_NOPE + MLA_ROPE) ** -0.5
ROPE_THETA = 10000.0

V7X_LANES = 128
V7X_SUBLANES = 8
V7X_MXU_DIM = 256
V7X_VMEM_LIMIT_BYTES = 56 * 1024 * 1024

_SRC_COLS = dict(
    qkv=(0, 768), z_a=(768, 1024), a_a=(1024, 1028), b_a=(1028, 1032),
    u=(1032, 1288), v=(1288, 1544),
    z_c=(1544, 1800), xbc=(1800, 2568), dt=(2568, 2572),
    c_q=(2572, 2828), c_kv=(2828, 2956), k_pe=(2956, 2988))
_DST_ORDER = ('qkv', 'xbc', 'z_a', 'u', 'v', 'z_c', 'c_q', 'c_kv', 'k_pe', 'a_a', 'b_a', 'dt')
_DST_COLS = {}
_off = 0
for _name in _DST_ORDER:
    _w = _SRC_COLS[_name][1] - _SRC_COLS[_name][0]
    _DST_COLS[_name] = (_off, _off + _w)
    _off += _w
PROJ_USED = _off
PROJ_COLS = -(-PROJ_USED // V7X_LANES) * V7X_LANES
_PERM = np.concatenate([np.arange(*_SRC_COLS[n]) for n in _DST_ORDER])
SMALL_BLOCK = _DST_COLS['k_pe'][0] // V7X_LANES
LANE_A = _DST_COLS['a_a'][0] % V7X_LANES
LANE_B = _DST_COLS['b_a'][0] % V7X_LANES
LANE_DT = _DST_COLS['dt'][0] % V7X_LANES


def _col_block(name):
    lo, hi = _DST_COLS[name]
    assert lo % (hi - lo) == 0
    return lo // (hi - lo)


FFN_CHUNK = V7X_MXU_DIM
TOKEN_TILE = 512
SEQ_TILE = 256
SAMPLE_ROWS = V7X_SUBLANES
SAMPLE_SEQS_PER_STEP = 8
GDN_CHUNK = 64
NEG_BIG = -1e30


def _cparams(*sem):
    return pltpu.CompilerParams(dimension_semantics=sem, vmem_limit_bytes=V7X_VMEM_LIMIT_BYTES)


def _rms(x, g):
    return x * lax.rsqrt(jnp.mean(x * x, axis=-1, keepdims=True) + EPS) * g


def _silu(x):
    return x * jax.nn.sigmoid(x)


def _softplus(x):
    return jnp.maximum(x, 0.0) + jnp.log(1.0 + jnp.exp(-jnp.abs(x)))


_NN = (((1,), (0,)), ((), ()))
_NT = (((1,), (1,)), ((), ()))
_TN = (((0,), (0,)), ((), ()))


def _mm(a, b, dims=_NN):
    return lax.dot_general(a.astype(BF16), b.astype(BF16), dims, preferred_element_type=F32)


def _split3(x):
    x0 = x.astype(BF16)
    r1 = x - x0.astype(F32)
    x1 = r1.astype(BF16)
    x2 = (r1 - x1.astype(F32)).astype(BF16)
    return x0, x1, x2


def _mm_exact_lhs(a_bf16, b):
    b0, b1, b2 = _split3(b)
    dot = lambda y: lax.dot_general(a_bf16, y, _NN, preferred_element_type=F32)
    return dot(b0) + (dot(b1) + dot(b2))


def _iota2(n, m, axis):
    return lax.broadcasted_iota(jnp.int32, (n, m), axis)


def _resident(shape):
    return pl.BlockSpec(shape, lambda *_: (0,) * len(shape), pipeline_mode=pl.Buffered(1))


def _ffn_body(x, gpre_ref, gpost_ref, win_ref, wout_ref, o_ref, acc_ref):
    xn = _rms(x, gpre_ref[...]).astype(BF16)
    n_chunks = FFN_DIM // FFN_CHUNK
    for c in range(n_chunks):
        lo = c * FFN_CHUNK
        gate = jnp.dot(xn, win_ref[:, lo:lo + FFN_CHUNK], preferred_element_type=F32)
        up = jnp.dot(xn, win_ref[:, FFN_DIM + lo:FFN_DIM + lo + FFN_CHUNK], preferred_element_type=F32)
        h = (_silu(gate) * up).astype(BF16)
        part = jnp.dot(h, wout_ref[lo:lo + FFN_CHUNK, :], preferred_element_type=F32)
        if c == 0:
            acc_ref[...] = part
        else:
            acc_ref[...] += part
    o_ref[...] = x + 0.5 * _rms(acc_ref[...], gpost_ref[...])


def _ffn_kernel(x_ref, gpre_ref, gpost_ref, win_ref, wout_ref, o_ref, acc_ref):
    _ffn_body(x_ref[...], gpre_ref, gpost_ref, win_ref, wout_ref, o_ref, acc_ref)


def _mix_ffn_kernel(x_ref, ma_ref, mb_ref, mc_ref, md_ref, wo_ref, gmix_ref, gpre_ref, gpost_ref,
                    win_ref, wout_ref, o_ref, acc_ref):
    y = None
    for i, m_ref in enumerate((ma_ref, mb_ref, mc_ref, md_ref)):
        part = jnp.dot(m_ref[...].astype(BF16), wo_ref[i * GROUP_W:(i + 1) * GROUP_W, :],
                       preferred_element_type=F32)
        y = part if y is None else y + part
    x = x_ref[...] + _rms(y, gmix_ref[...])
    _ffn_body(x, gpre_ref, gpost_ref, win_ref, wout_ref, o_ref, acc_ref)


def _ffn(x, g_pre, g_post, w_in, w_out, mix=None, w_o=None, g_mix=None):
    t = x.shape[0]
    tm = min(TOKEN_TILE, t)
    assert t % tm == 0
    tok = pl.BlockSpec((tm, D_MODEL), lambda i: (i, 0))
    vec = _resident((1, D_MODEL))
    w_specs = [_resident((D_MODEL, 2 * FFN_DIM)), _resident((FFN_DIM, D_MODEL))]
    if mix is None:
        kern, args = _ffn_kernel, (x, g_pre, g_post, w_in, w_out)
        in_specs = [tok, vec, vec] + w_specs
    else:
        kern, args = _mix_ffn_kernel, (x, *mix, w_o, g_mix, g_pre, g_post, w_in, w_out)
        part = pl.BlockSpec((tm, GROUP_W), lambda i: (i, 0))
        in_specs = [tok, part, part, part, part, _resident((D_MODEL, D_MODEL)), vec, vec, vec] + w_specs
    return pl.pallas_call(
        kern,
        grid=(t // tm,),
        in_specs=in_specs,
        out_specs=tok,
        out_shape=jax.ShapeDtypeStruct((t, D_MODEL), F32),
        scratch_shapes=[pltpu.VMEM((tm, D_MODEL), F32)],
        compiler_params=_cparams("arbitrary"),
        name="mix_ffn" if mix is not None else "ffn",
    )(*args)


def _proj_kernel(x_ref, g_ref, w_ref, o_ref):
    xn = _rms(x_ref[...], g_ref[...]).astype(BF16)
    o_ref[...] = jnp.dot(xn, w_ref[...], preferred_element_type=F32)


def _proj(x, g, w):
    t = x.shape[0]
    tm = min(TOKEN_TILE, t)
    assert t % tm == 0
    return pl.pallas_call(
        _proj_kernel,
        grid=(t // tm,),
        in_specs=[pl.BlockSpec((tm, D_MODEL), lambda i: (i, 0)),
                  _resident((1, D_MODEL)), _resident((D_MODEL, PROJ_COLS))],
        out_specs=pl.BlockSpec((tm, PROJ_COLS), lambda i: (i, 0)),
        out_shape=jax.ShapeDtypeStruct((t, PROJ_COLS), F32),
        compiler_params=_cparams("arbitrary"),
        name="in_proj",
    )(x, g, w)


def _seq_spec(rows, name, n_steps):
    lo, hi = _DST_COLS[name]
    return pl.BlockSpec((rows, hi - lo), lambda b, n, c=_col_block(name): (b * n_steps + n, c))


def _small_spec(rows, n_steps):
    return pl.BlockSpec((rows, V7X_LANES), lambda b, n: (b * n_steps + n, SMALL_BLOCK))


def _out_spec(rows, width, n_steps):
    return pl.BlockSpec((rows, width), lambda b, n: (b * n_steps + n, 0))


def _causal_conv(x, halo_ref, conv0_ref, w_ref, rows, seq_rows):
    first = V7X_SUBLANES - (CONV_K - 1)

    def taps(i):
        y = w_ref[0:1, :] * halo_ref[i, first:first + seq_rows, :]
        for j in range(1, CONV_K):
            y = y + w_ref[j:j + 1, :] * halo_ref[i, first + j:first + j + seq_rows, :]
        return y

    if seq_rows == rows:
        @pl.when(pl.program_id(1) == 0)
        def _():
            halo_ref[0, first:V7X_SUBLANES, :] = conv0_ref[0]

        halo_ref[0, V7X_SUBLANES:V7X_SUBLANES + rows, :] = x
        y = taps(0)
        halo_ref[0, 0:V7X_SUBLANES, :] = x[rows - V7X_SUBLANES:rows, :]
        return y
    parts = []
    for i in range(rows // seq_rows):
        halo_ref[i, first:V7X_SUBLANES, :] = conv0_ref[i]
        halo_ref[i, V7X_SUBLANES:V7X_SUBLANES + seq_rows, :] = x[i * seq_rows:(i + 1) * seq_rows, :]
        parts.append(taps(i))
    return jnp.concatenate(parts, axis=0)


def _level_masks(rows):
    i = np.arange(rows)[:, None]
    j = np.arange(rows)[None, :]
    out = []
    s = 1
    while s < rows:
        out.append(((i // (2 * s) == j // (2 * s)) & (i % (2 * s) >= s) & (j % (2 * s) < s)).astype(np.float32))
        s *= 2
    return np.stack(out)


def _unit_lower_inverse(a_strict, lv_ref, eye):
    t = eye - a_strict * lv_ref[0]
    for lv in range(1, lv_ref.shape[0]):
        a_s = a_strict * lv_ref[lv]
        t = t - _mm(_mm(t, a_s), t)
    return t


def _gdn_kernel(qkv_ref, z_ref, sm_ref, conv0_ref, s0_ref, cw_ref, lane_ref, g_ref, lv_ref,
                o_ref, sfin_ref, halo_ref, s_ref, *, rows, chunk, n_valid, chained):
    n_chunks = rows // chunk
    shift = chunk.bit_length() - 1
    if chained:
        @pl.when(pl.program_id(1) == 0)
        def _():
            s_ref[...] = s0_ref[0]

    act = _silu(_causal_conv(qkv_ref[...], halo_ref, conv0_ref, cw_ref, rows, rows if chained else chunk))
    sm = sm_ref[...]
    row = _iota2(rows, rows, 0)
    col = _iota2(rows, rows, 1)
    same_chunk = lax.shift_right_logical(row, shift) == lax.shift_right_logical(col, shift)
    tril_blocks = jnp.where(same_chunk, (col <= row).astype(F32), 0.0).astype(BF16)
    valid = (_iota2(rows, V7X_LANES, 0) & (chunk - 1)) < n_valid
    log_alpha = jnp.where(valid, -jnp.exp(lane_ref[0:1, :]) * _softplus(sm + lane_ref[1:2, :]), 0.0)
    beta = jnp.where(valid, jax.nn.sigmoid(sm), 0.0)
    gcum = _mm_exact_lhs(tril_blocks, log_alpha)
    gcum_t = gcum.T
    rc = _iota2(chunk, chunk, 0)
    cc = _iota2(chunk, chunk, 1)
    tril = cc <= rc
    strict = cc < rc
    eye = (cc == rc).astype(F32)
    z = z_ref[...]
    heads = range(GDN_HEADS)
    chunks = range(n_chunks)
    chains = [(h, c) for h in heads for c in chunks]
    rsl = [slice(c * chunk, (c + 1) * chunk) for c in chunks]
    q, k, kb, rhs, qd, gcol = {}, {}, {}, {}, {}, {}
    for h in heads:
        lane = LANE_A + h
        qh = act[:, h * GDN_DK:(h + 1) * GDN_DK]
        kh = act[:, GROUP_W + h * GDN_DK:GROUP_W + (h + 1) * GDN_DK]
        vh = act[:, 2 * GROUP_W + h * GDN_DV:2 * GROUP_W + (h + 1) * GDN_DV]
        qh = qh * lax.rsqrt(jnp.sum(qh * qh, axis=-1, keepdims=True) + EPS) * GDN_DK ** -0.5
        kh = kh * lax.rsqrt(jnp.sum(kh * kh, axis=-1, keepdims=True) + EPS)
        g_h = gcum[:, lane:lane + 1]
        b_h = beta[:, LANE_B + h:LANE_B + h + 1]
        e_h = jnp.exp(g_h)
        kb_h = kh * b_h
        rhs_h = jnp.concatenate([kb_h * e_h, vh * b_h], axis=1)
        qd_h = qh * e_h
        for c in chunks:
            q[h, c], k[h, c], kb[h, c] = qh[rsl[c]], kh[rsl[c]], kb_h[rsl[c]]
            rhs[h, c], qd[h, c], gcol[h, c] = rhs_h[rsl[c]], qd_h[rsl[c]], g_h[rsl[c]]
    decay = {(h, c): jnp.exp(jnp.where(tril, gcol[h, c] - gcum_t[LANE_A + h:LANE_A + h + 1, rsl[c]], NEG_BIG))
             for h, c in chains}
    kk = {i: _mm(kb[i], k[i], _NT) for i in chains}
    qk = {i: _mm(q[i], k[i], _NT) * decay[i] for i in chains}
    a_strict = {i: jnp.where(strict, kk[i] * decay[i], 0.0) for i in chains}
    t_inv = {i: eye - a_strict[i] * lv_ref[0] for i in chains}
    for lv in range(1, lv_ref.shape[0]):
        half = {i: _mm(t_inv[i], a_strict[i] * lv_ref[lv]) for i in chains}
        t_inv = {i: t_inv[i] - _mm(half[i], t_inv[i]) for i in chains}
    sol = {i: _mm(t_inv[i], rhs[i]) for i in chains}
    glast = {i: gcol[i][chunk - 1:chunk] for i in chains}
    k_dec = {i: k[i] * jnp.exp(glast[i] - gcol[i]) for i in chains}
    outs = {}
    finals = [[None] * GDN_HEADS for _ in range(1 if chained else n_chunks)]
    state = {h: s_ref[h] for h in heads} if chained else None
    for c in chunks:
        cur = state if chained else {h: s0_ref[c, h] for h in heads}
        w_s = {h: _mm(sol[h, c][:, :GDN_DK], cur[h]) for h in heads}
        o_s = {h: _mm(qd[h, c], cur[h]) for h in heads}
        u_new = {h: sol[h, c][:, GDN_DK:] - w_s[h] for h in heads}
        for h in heads:
            outs[h, c] = o_s[h] + _mm(qk[h, c], u_new[h])
        nxt = {h: cur[h] * jnp.exp(glast[h, c]) + _mm(k_dec[h, c], u_new[h], _TN) for h in heads}
        if chained:
            state = nxt
        else:
            finals[c] = [nxt[h] for h in heads]
    if chained:
        finals[0] = [state[h] for h in heads]
    head_outs = []
    for h in heads:
        o = outs[h, 0] if n_chunks == 1 else jnp.concatenate([outs[h, c] for c in chunks], axis=0)
        head_outs.append(_rms(o, g_ref[...]) * _silu(z[:, h * GDN_DV:(h + 1) * GDN_DV]))
    o_ref[...] = jnp.concatenate(head_outs, axis=1)
    new_states = jnp.stack([jnp.stack(per_seq) for per_seq in finals])
    sfin_ref[...] = new_states
    if chained:
        s_ref[...] = new_states[0]


def _gdn(p, conv0, s0, conv_w, a_log, dt_bias, norm_g, n_seq, seq_len, rows, chunk, n_valid):
    chained = seq_len > chunk
    seqs_per_step = 1 if chained else rows // chunk
    n_steps = seq_len // rows if chained else 1
    n_blocks = n_seq // seqs_per_step
    assert n_blocks * seqs_per_step == n_seq and chunk & (chunk - 1) == 0
    lane = jnp.zeros((2, V7X_LANES), F32)
    lane = lane.at[0, LANE_A:LANE_A + GDN_HEADS].set(a_log).at[1, LANE_A:LANE_A + GDN_HEADS].set(dt_bias)
    levels = jnp.asarray(_level_masks(chunk))
    halo_rows = V7X_SUBLANES + (rows if chained else chunk)
    return pl.pallas_call(
        functools.partial(_gdn_kernel, rows=rows, chunk=chunk, n_valid=n_valid, chained=chained),
        grid=(n_blocks, n_steps),
        in_specs=[_seq_spec(rows, 'qkv', n_steps), _seq_spec(rows, 'z_a', n_steps), _small_spec(rows, n_steps),
                  pl.BlockSpec((seqs_per_step, CONV_K - 1, 3 * GROUP_W), lambda b, n: (b, 0, 0)),
                  pl.BlockSpec((seqs_per_step, GDN_HEADS, GDN_DK, GDN_DV), lambda b, n: (b, 0, 0, 0)),
                  _resident((CONV_K, 3 * GROUP_W)), _resident((2, V7X_LANES)), _resident((1, GDN_DV)),
                  _resident(levels.shape)],
        out_specs=[_out_spec(rows, GROUP_W, n_steps),
                   pl.BlockSpec((seqs_per_step, GDN_HEADS, GDN_DK, GDN_DV), lambda b, n: (b, 0, 0, 0))],
        out_shape=[jax.ShapeDtypeStruct((n_seq * seq_len, GROUP_W), F32),
                   jax.ShapeDtypeStruct((n_seq, GDN_HEADS, GDN_DK, GDN_DV), F32)],
        scratch_shapes=[pltpu.VMEM((seqs_per_step, halo_rows, 3 * GROUP_W), F32),
                        pltpu.VMEM((GDN_HEADS, GDN_DK, GDN_DV), F32)],
        compiler_params=_cparams("arbitrary", "arbitrary"),
        name="gdn",
    )(p, p, p, conv0, s0, conv_w, lane, norm_g.reshape(1, GDN_DV), levels)


def _gmlp_kernel(u_ref, v_ref, lng_ref, lnb_ref, ws_ref, bias_ref, o_ref, vb_ref, *, rows, chunk):
    u = jax.nn.gelu(u_ref[...], approximate=True)
    v = jax.nn.gelu(v_ref[...], approximate=True)
    mu = jnp.mean(v, axis=-1, keepdims=True)
    var = jnp.mean(jnp.square(v - mu), axis=-1, keepdims=True)
    v = (v - mu) * lax.rsqrt(var + EPS) * lng_ref[...] + lnb_ref[...]
    vb_ref[...] = v
    tril = _iota2(chunk, chunk, 1) <= _iota2(chunk, chunk, 0)
    ws = [jnp.where(tril, ws_ref[g], 0.0).astype(BF16) for g in range(MLP_GROUPS)]
    for c in range(rows // chunk):
        rs = slice(c * chunk, (c + 1) * chunk)
        for g in range(MLP_GROUPS):
            ls = slice(g * MLP_GW, (g + 1) * MLP_GW)
            mixed = _mm(ws[g], v[rs, ls]) + bias_ref[:, ls]
            o_ref[rs, ls] = u[rs, ls] * mixed


def _gmlp(p, ln_g, ln_b, ws, bs, n_seq, seq_len, rows, chunk):
    n_steps = seq_len // rows
    bias = jnp.repeat(bs[:, :chunk].T, MLP_GW, axis=1)
    return pl.pallas_call(
        functools.partial(_gmlp_kernel, rows=rows, chunk=chunk),
        grid=(n_seq, n_steps),
        in_specs=[_seq_spec(rows, 'u', n_steps), _seq_spec(rows, 'v', n_steps),
                  _resident((1, GROUP_W)), _resident((1, GROUP_W)),
                  _resident((MLP_GROUPS, chunk, chunk)), _resident((chunk, GROUP_W))],
        out_specs=[_out_spec(rows, GROUP_W, n_steps), _out_spec(rows, GROUP_W, n_steps)],
        out_shape=[jax.ShapeDtypeStruct((n_seq * seq_len, GROUP_W), F32)] * 2,
        compiler_params=_cparams("arbitrary", "arbitrary"),
        name="gmlp",
    )(p, p, ln_g.reshape(1, GROUP_W), ln_b.reshape(1, GROUP_W), ws[:, :chunk, :chunk], bias)


def _ssd_kernel(xbc_ref, z_ref, sm_ref, conv0_ref, h0_ref, cw_ref, cb_ref, lane_ref, g_ref,
                o_ref, hfin_ref, halo_ref, h_ref, *, rows, n_valid):
    @pl.when(pl.program_id(1) == 0)
    def _():
        h_ref[...] = h0_ref[0]

    act = _silu(_causal_conv(xbc_ref[...], halo_ref, conv0_ref, cw_ref, rows, rows) + cb_ref[...])
    sm = sm_ref[...]
    tril = _iota2(rows, rows, 1) <= _iota2(rows, rows, 0)
    valid = _iota2(rows, V7X_LANES, 0) < n_valid
    dt = jnp.where(valid, _softplus(sm + lane_ref[1:2, :]), 0.0)
    acum = _mm_exact_lhs(tril.astype(BF16), dt * -jnp.exp(lane_ref[0:1, :]))
    acum_t = acum.T
    heads = range(SSM_HEADS)
    group_of = [h // (SSM_HEADS // SSM_GROUPS) for h in heads]
    b_g = [act[:, GROUP_W + g * SSM_STATE:GROUP_W + (g + 1) * SSM_STATE] for g in range(SSM_GROUPS)]
    c_g = [act[:, GROUP_W + (SSM_GROUPS + g) * SSM_STATE:GROUP_W + (SSM_GROUPS + g + 1) * SSM_STATE]
           for g in range(SSM_GROUPS)]
    cb = [_mm(c_g[g], b_g[g], _NT) for g in range(SSM_GROUPS)]
    acol = [acum[:, LANE_DT + h:LANE_DT + h + 1] for h in heads]
    alast = [acum[rows - 1:rows, LANE_DT + h:LANE_DT + h + 1] for h in heads]
    decay = [jnp.exp(jnp.where(tril, acol[h] - acum_t[LANE_DT + h:LANE_DT + h + 1, :], NEG_BIG)) for h in heads]
    x = [act[:, h * SSM_HEAD_DIM:(h + 1) * SSM_HEAD_DIM] for h in heads]
    xdt = [x[h] * dt[:, LANE_DT + h:LANE_DT + h + 1] for h in heads]
    state = [h_ref[h] for h in heads]
    y_intra = [_mm(cb[group_of[h]] * decay[h], xdt[h]) for h in heads]
    y_inter = [_mm(c_g[group_of[h]] * jnp.exp(acol[h]), state[h], _NT) for h in heads]
    h_new = [state[h] * jnp.exp(alast[h]) + _mm(xdt[h], b_g[group_of[h]] * jnp.exp(alast[h] - acol[h]), _TN)
             for h in heads]
    y = jnp.concatenate([y_intra[h] + y_inter[h] + lane_ref[2:3, LANE_DT + h:LANE_DT + h + 1] * x[h]
                         for h in heads], axis=1)
    new_states = jnp.stack(h_new)
    h_ref[...] = new_states
    hfin_ref[0] = new_states
    o_ref[...] = _rms(y * _silu(z_ref[...]), g_ref[...])


def _ssd(p, conv0, h0, conv_w, conv_b, a_log, dt_bias, d_skip, norm_g, n_seq, seq_len, rows, n_valid):
    n_steps = seq_len // rows
    lane = jnp.zeros((3, V7X_LANES), F32)
    lane = (lane.at[0, LANE_DT:LANE_DT + SSM_HEADS].set(a_log)
            .at[1, LANE_DT:LANE_DT + SSM_HEADS].set(dt_bias)
            .at[2, LANE_DT:LANE_DT + SSM_HEADS].set(d_skip))
    return pl.pallas_call(
        functools.partial(_ssd_kernel, rows=rows, n_valid=n_valid),
        grid=(n_seq, n_steps),
        in_specs=[_seq_spec(rows, 'xbc', n_steps), _seq_spec(rows, 'z_c', n_steps), _small_spec(rows, n_steps),
                  pl.BlockSpec((1, CONV_K - 1, SSM_CONV_W), lambda b, n: (b, 0, 0)),
                  pl.BlockSpec((1, SSM_HEADS, SSM_HEAD_DIM, SSM_STATE), lambda b, n: (b, 0, 0, 0)),
                  _resident((CONV_K, SSM_CONV_W)), _resident((1, SSM_CONV_W)), _resident((3, V7X_LANES)),
                  _resident((1, GROUP_W))],
        out_specs=[_out_spec(rows, GROUP_W, n_steps),
                   pl.BlockSpec((1, SSM_HEADS, SSM_HEAD_DIM, SSM_STATE), lambda b, n: (b, 0, 0, 0))],
        out_shape=[jax.ShapeDtypeStruct((n_seq * seq_len, GROUP_W), F32),
                   jax.ShapeDtypeStruct((n_seq, SSM_HEADS, SSM_HEAD_DIM, SSM_STATE), F32)],
        scratch_shapes=[pltpu.VMEM((1, V7X_SUBLANES + rows, SSM_CONV_W), F32),
                        pltpu.VMEM((SSM_HEADS, SSM_HEAD_DIM, SSM_STATE), F32)],
        compiler_params=_cparams("arbitrary", "arbitrary"),
        name="ssd",
    )(p, p, p, conv0, h0, conv_w, conv_b.reshape(1, SSM_CONV_W), lane, norm_g.reshape(1, GROUP_W))


def _rope_lanes(x, cos, sin_signed):
    lane = _iota2(x.shape[0], V7X_LANES, 1)
    half = MLA_ROPE // 2
    swapped = jnp.where(lane % MLA_ROPE < half,
                        pltpu.roll(x, V7X_LANES - half, axis=1), pltpu.roll(x, half, axis=1))
    return x * cos + swapped * sin_signed


def _rope_tables(pos):
    half = MLA_ROPE // 2
    inv_freq = ROPE_THETA ** (-jnp.arange(half, dtype=F32) / half)
    ang = pos.astype(F32)[:, None] * inv_freq[None, :]
    cos, sin = jnp.cos(ang), jnp.sin(ang)
    reps = V7X_LANES // MLA_ROPE
    return jnp.tile(jnp.concatenate([cos, cos], axis=1), (1, reps)), jnp.tile(jnp.concatenate([-sin, sin], axis=1), (1, reps))


def _mla_project(cq_ref, ckv_ref, sm_ref, cos_ref, sin_ref, gq_ref, gkv_ref, wn_ref, wpe_ref, wuk_ref):
    c_q = _rms(cq_ref[...], gq_ref[...]).astype(BF16)
    c_kv = _rms(ckv_ref[...], gkv_ref[...])
    cos, sin = cos_ref[...], sin_ref[...]
    lane = _iota2(c_kv.shape[0], V7X_LANES, 1)
    k_pe = _rope_lanes(jnp.where(lane < MLA_ROPE, sm_ref[...], 0.0), cos, sin)
    q_nope = jnp.dot(c_q, wn_ref[...], preferred_element_type=F32)
    q_heads = []
    for h in range(MLA_HEADS):
        q_lat = _mm(q_nope[:, h * MLA_NOPE:(h + 1) * MLA_NOPE], wuk_ref[h], _NT)
        q_pe = _rope_lanes(jnp.dot(c_q, wpe_ref[h], preferred_element_type=F32), cos, sin)
        q_heads.append((jnp.concatenate([q_lat, q_pe], axis=1) * MLA_SCALE).astype(BF16))
    return c_kv, k_pe, q_heads


def _softmax_step(q, keys, m, l, acc, mask=None):
    s = lax.dot_general(q, keys, _NT, preferred_element_type=F32)
    if mask is not None:
        s = jnp.where(mask, s, NEG_BIG)
    m_new = jnp.maximum(m, jnp.max(s, axis=-1, keepdims=True))
    p = jnp.exp(s - m_new)
    alpha = jnp.exp(m - m_new)
    l = alpha * l + jnp.sum(p, axis=-1, keepdims=True)
    acc = alpha * acc + jnp.dot(p.astype(BF16), keys[:, :MLA_KV_RANK], preferred_element_type=F32)
    return m_new, l, acc


def _mla_prompt_kernel(cq_ref, ckv_ref, sm_ref, cos_ref, sin_ref, gq_ref, gkv_ref, wn_ref, wpe_ref, wuk_ref,
                       wuv_ref, o_ref, ckv_out_ref, kpe_out_ref, keys_ref, vt_ref, q_ref, m_ref, l_ref, acc_ref,
                       *, rows):
    n = pl.program_id(1)
    qrows = MLA_HEADS * rows
    c_kv, k_pe, q_heads = _mla_project(cq_ref, ckv_ref, sm_ref, cos_ref, sin_ref, gq_ref, gkv_ref,
                                       wn_ref, wpe_ref, wuk_ref)
    ckv_out_ref[0] = c_kv
    kpe_out_ref[0] = k_pe[:, :MLA_ROPE]
    base = pl.multiple_of(n * rows, rows)
    keys_ref[pl.ds(base, rows), :] = jnp.concatenate([c_kv, k_pe], axis=1).astype(BF16)
    vt_ref[:, pl.ds(base, rows)] = c_kv.T.astype(BF16)
    q_ref[...] = jnp.concatenate(q_heads, axis=0)
    m_ref[...] = jnp.full((1, qrows), NEG_BIG, F32)
    l_ref[...] = jnp.zeros((1, qrows), F32)
    acc_ref[...] = jnp.zeros((MLA_KV_RANK, qrows), F32)

    def attend(off, mask):
        s = lax.dot_general(keys_ref[pl.ds(off, rows), :], q_ref[...], _NT, preferred_element_type=F32)
        if mask is not None:
            s = jnp.where(mask, s, NEG_BIG)
        m_old = m_ref[...]
        m_new = jnp.maximum(m_old, jnp.max(s, axis=0, keepdims=True))
        p = jnp.exp(s - m_new)
        alpha = jnp.exp(m_old - m_new)
        l_ref[...] = alpha * l_ref[...] + jnp.sum(p, axis=0, keepdims=True)
        acc_ref[...] = alpha * acc_ref[...] + jnp.dot(vt_ref[:, pl.ds(off, rows)], p.astype(BF16),
                                                      preferred_element_type=F32)
        m_ref[...] = m_new

    def body(j, carry):
        attend(pl.multiple_of(j * rows, rows), None)
        return carry

    lax.fori_loop(0, n, body, 0)
    causal = _iota2(rows, qrows, 0) <= (_iota2(rows, qrows, 1) & (rows - 1))
    attend(base, causal)
    o_lat_t = acc_ref[...] / l_ref[...]
    outs = [_mm(o_lat_t[:, h * rows:(h + 1) * rows], wuv_ref[h], _TN) for h in range(MLA_HEADS)]
    o_ref[...] = jnp.concatenate(outs, axis=1)


def _mla_weights(w_uq, w_uk, w_uv):
    w4 = w_uq.reshape(MLA_Q_RANK, MLA_HEADS, MLA_NOPE + MLA_ROPE)
    w_nope = w4[:, :, :MLA_NOPE].reshape(MLA_Q_RANK, MLA_HEADS * MLA_NOPE).astype(BF16)
    w_pe = jnp.pad(w4[:, :, MLA_NOPE:].transpose(1, 0, 2),
                   ((0, 0), (0, 0), (0, V7X_LANES - MLA_ROPE))).astype(BF16)
    return w_nope, w_pe, w_uk.astype(BF16), w_uv.astype(BF16)


def _mla_prompt(p, q_norm_g, kv_norm_g, w_nope, w_pe, w_uk, w_uv, n_seq, seq_len, rows):
    n_steps = seq_len // rows
    cos, sin = _rope_tables(jnp.arange(seq_len))
    tab = pl.BlockSpec((rows, V7X_LANES), lambda b, n: (n, 0))
    return pl.pallas_call(
        functools.partial(_mla_prompt_kernel, rows=rows),
        grid=(n_seq, n_steps),
        in_specs=[_seq_spec(rows, 'c_q', n_steps), _seq_spec(rows, 'c_kv', n_steps), _small_spec(rows, n_steps),
                  tab, tab, _resident((1, MLA_Q_RANK)), _resident((1, MLA_KV_RANK)),
                  _resident(w_nope.shape), _resident(w_pe.shape), _resident(w_uk.shape), _resident(w_uv.shape)],
        out_specs=[_out_spec(rows, GROUP_W, n_steps),
                   pl.BlockSpec((1, rows, MLA_KV_RANK), lambda b, n: (b, n, 0)),
                   pl.BlockSpec((1, rows, MLA_ROPE), lambda b, n: (b, n, 0))],
        out_shape=[jax.ShapeDtypeStruct((n_seq * seq_len, GROUP_W), F32),
                   jax.ShapeDtypeStruct((n_seq, seq_len, MLA_KV_RANK), F32),
                   jax.ShapeDtypeStruct((n_seq, seq_len, MLA_ROPE), F32)],
        scratch_shapes=[pltpu.VMEM((seq_len, 2 * V7X_LANES), BF16),
                        pltpu.VMEM((MLA_KV_RANK, seq_len), BF16),
                        pltpu.VMEM((MLA_HEADS * rows, 2 * V7X_LANES), BF16),
                        pltpu.VMEM((1, MLA_HEADS * rows), F32), pltpu.VMEM((1, MLA_HEADS * rows), F32),
                        pltpu.VMEM((MLA_KV_RANK, MLA_HEADS * rows), F32)],
        compiler_params=_cparams("arbitrary", "arbitrary"),
        name="mla_prompt",
    )(p, p, p, cos, sin, q_norm_g.reshape(1, MLA_Q_RANK), kv_norm_g.reshape(1, MLA_KV_RANK),
      w_nope, w_pe, w_uk, w_uv)


def _mla_decode_kernel(pt_ref, cq_ref, ckv_ref, sm_ref, cos_ref, sin_ref, gq_ref, gkv_ref, wn_ref, wpe_ref,
                       wuk_ref, wuv_ref, *rest, rows, pages_per_step):
    page_refs = rest[:2 * pages_per_step]
    o_ref, ckv_out_ref, kpe_out_ref, q_ref, knew_ref, s_ref, lat_ref = rest[2 * pages_per_step:]
    j = pl.program_id(1)
    qrows = MLA_HEADS * rows
    past_len = s_ref.shape[1]

    @pl.when(j == 0)
    def _():
        c_kv, k_pe, q_heads = _mla_project(cq_ref, ckv_ref, sm_ref, cos_ref, sin_ref, gq_ref, gkv_ref,
                                           wn_ref, wpe_ref, wuk_ref)
        ckv_out_ref[0] = c_kv
        kpe_out_ref[0] = k_pe[:, :MLA_ROPE]
        knew_ref[...] = jnp.concatenate([c_kv, k_pe], axis=1).astype(BF16)
        q_ref[...] = jnp.concatenate(q_heads, axis=0)

    q = q_ref[...]
    q_lat, q_pe = q[:, :MLA_KV_RANK], q[:, MLA_KV_RANK:MLA_KV_RANK + MLA_ROPE]
    for g in range(pages_per_step):
        lat = page_refs[g][...].astype(BF16)
        s = (lax.dot_general(q_lat, lat, _NT, preferred_element_type=F32)
             + lax.dot_general(q_pe, page_refs[pages_per_step + g][...].astype(BF16), _NN,
                               preferred_element_type=F32))
        base = pl.multiple_of((j * pages_per_step + g) * PAGE_SIZE, PAGE_SIZE)
        s_ref[:, pl.ds(base, PAGE_SIZE)] = s
        lat_ref[pl.ds(base, PAGE_SIZE), :] = lat

    @pl.when(j == pl.num_programs(1) - 1)
    def _():
        knew = knew_ref[...]
        causal = _iota2(qrows, rows, 1) <= _iota2(qrows, rows, 0) % rows
        s_new = jnp.where(causal, lax.dot_general(q, knew, _NT, preferred_element_type=F32), NEG_BIG)
        chunk = min(past_len, 16 * PAGE_SIZE)
        m = jnp.max(s_new, axis=-1, keepdims=True)
        for c in range(past_len // chunk):
            m = jnp.maximum(m, jnp.max(s_ref[:, c * chunk:(c + 1) * chunk], axis=-1, keepdims=True))
        p_new = jnp.exp(s_new - m)
        l = jnp.sum(p_new, axis=-1, keepdims=True)
        acc = jnp.dot(p_new.astype(BF16), knew[:, :MLA_KV_RANK], preferred_element_type=F32)
        for c in range(past_len // chunk):
            pr = jnp.exp(s_ref[:, c * chunk:(c + 1) * chunk] - m)
            l = l + jnp.sum(pr, axis=-1, keepdims=True)
            acc = acc + jnp.dot(pr.astype(BF16), lat_ref[c * chunk:(c + 1) * chunk, :],
                                preferred_element_type=F32)
        o_lat = acc / l
        for h in range(MLA_HEADS):
            o_ref[:, h * MLA_V_DIM:(h + 1) * MLA_V_DIM] = _mm(o_lat[h * rows:(h + 1) * rows], wuv_ref[h])


def _mla_decode(p, cache_ckv, cache_kpe, layer, page_table, q_norm_g, kv_norm_g, w_nope, w_pe, w_uk, w_uv,
                n_seq, rows, pages_per_step=32):
    n_pages = page_table.shape[1]
    assert n_pages % pages_per_step == 0
    n_steps = n_pages // pages_per_step
    past_len = n_pages * PAGE_SIZE
    cos, sin = _rope_tables(past_len + jnp.arange(rows))

    def const(shape):
        return pl.BlockSpec(shape, lambda s, j, pt: (0,) * len(shape))

    def seq(name):
        lo, hi = _DST_COLS[name]
        return pl.BlockSpec((rows, hi - lo), lambda s, j, pt, c=_col_block(name): (s, c))

    def page(shape, g):
        return pl.BlockSpec((None, None) + shape,
                            lambda s, j, pt, g=g: (layer, pt[s, j * pages_per_step + g], 0, 0))

    in_specs = ([seq('c_q'), seq('c_kv'), pl.BlockSpec((rows, V7X_LANES), lambda s, j, pt: (s, SMALL_BLOCK)),
                 const((rows, V7X_LANES)), const((rows, V7X_LANES)),
                 const((1, MLA_Q_RANK)), const((1, MLA_KV_RANK)),
                 const(w_nope.shape), const(w_pe.shape), const(w_uk.shape), const(w_uv.shape)]
                + [page((PAGE_SIZE, MLA_KV_RANK), g) for g in range(pages_per_step)]
                + [page((MLA_ROPE, PAGE_SIZE), g) for g in range(pages_per_step)])
    qrows = MLA_HEADS * rows
    return pl.pallas_call(
        functools.partial(_mla_decode_kernel, rows=rows, pages_per_step=pages_per_step),
        grid_spec=pltpu.PrefetchScalarGridSpec(
            num_scalar_prefetch=1,
            grid=(n_seq, n_steps),
            in_specs=in_specs,
            out_specs=[pl.BlockSpec((rows, GROUP_W), lambda s, j, pt: (s, 0)),
                       pl.BlockSpec((1, rows, MLA_KV_RANK), lambda s, j, pt: (s, 0, 0)),
                       pl.BlockSpec((1, rows, MLA_ROPE), lambda s, j, pt: (s, 0, 0))],
            scratch_shapes=[pltpu.VMEM((qrows, 2 * V7X_LANES), BF16),
                            pltpu.VMEM((rows, 2 * V7X_LANES), BF16),
                            pltpu.VMEM((qrows, past_len), F32),
                            pltpu.VMEM((past_len, MLA_KV_RANK), BF16)]),
        out_shape=[jax.ShapeDtypeStruct((n_seq * rows, GROUP_W), F32),
                   jax.ShapeDtypeStruct((n_seq, rows, MLA_KV_RANK), F32),
                   jax.ShapeDtypeStruct((n_seq, rows, MLA_ROPE), F32)],
        compiler_params=_cparams("arbitrary", "arbitrary"),
        name="mla_decode",
    )(page_table, p, p, p, cos, sin, q_norm_g.reshape(1, MLA_Q_RANK), kv_norm_g.reshape(1, MLA_KV_RANK),
      w_nope, w_pe, w_uk, w_uv, *([cache_ckv] * pages_per_step), *([cache_kpe] * pages_per_step))


def _last_rows(p, n_seq, seq_len, n_valid, name):
    lo, hi = _DST_COLS[name]
    return p.reshape(n_seq, seq_len, PROJ_COLS)[:, n_valid - (CONV_K - 1):n_valid, lo:hi]


def _trunk_layer(x, w, layer, n_seq, seq_len, rows, n_valid, gdn_conv, gdn_s, ssm_conv, ssm_h, paged):
    g = w['norm_g']
    assert n_valid == seq_len or seq_len == rows
    step_valid = min(n_valid, rows)
    x = _ffn(x, g[0:1], g[1:2], w['ffn_w_in'][0], w['ffn_w_out'][0])
    p = _proj(x, g[2:3], w['w_in'])
    if seq_len == rows:
        gdn_shape = (rows * SAMPLE_SEQS_PER_STEP, rows, n_valid)
        mlp_shape = (1, n_seq * seq_len, min(SEQ_TILE, n_seq * seq_len), rows)
    else:
        gdn_shape = (rows, GDN_CHUNK, GDN_CHUNK)
        mlp_shape = (n_seq, seq_len, rows, MLP_CHUNK)
    out_a, gdn_s_new = _gdn(p, gdn_conv, gdn_s, w['gdn_conv_w'], w['gdn_a_log'], w['gdn_dt_bias'],
                            w['gdn_norm_g'], n_seq, seq_len, *gdn_shape)
    out_b, v_b = _gmlp(p, w['mlp_ln_g'], w['mlp_ln_b'], w['mlp_ws'], w['mlp_bs'], *mlp_shape)
    out_c, ssm_h_new = _ssd(p, ssm_conv, ssm_h, w['ssm_conv_w'], w['ssm_conv_b'], w['ssm_a_log'],
                            w['ssm_dt_bias'], w['ssm_d'], w['ssm_norm_g'], n_seq, seq_len, rows, step_valid)
    mla_w = (w['mla_q_norm_g'], w['mla_kv_norm_g'], w['mla_w_nope'], w['mla_w_pe'], w['mla_w_uk'], w['mla_w_uv'])
    if paged is None:
        out_d, c_kv, k_pe = _mla_prompt(p, *mla_w, n_seq, seq_len, rows)
    else:
        cache_ckv, cache_kpe, page_table = paged
        out_d, c_kv, k_pe = _mla_decode(p, cache_ckv, cache_kpe, layer, page_table, *mla_w, n_seq, rows)
    x = _ffn(x, g[4:5], g[5:6], w['ffn_w_in'][1], w['ffn_w_out'][1],
             mix=(out_a, out_b, out_c, out_d), w_o=w['w_out'], g_mix=g[3:4])
    gdn_conv_new = _last_rows(p, n_seq, seq_len, n_valid, 'qkv')
    ssm_conv_new = _last_rows(p, n_seq, seq_len, n_valid, 'xbc')
    v_b = v_b.reshape(n_seq, seq_len, GROUP_W)[:, :n_valid]
    return x, (c_kv[:, :n_valid], k_pe[:, :n_valid], gdn_s_new, gdn_conv_new, ssm_h_new, ssm_conv_new, v_b)


def kernel(x_prompt, x_sample, cache_ckv, cache_kpe, page_table, state_gdn_s, state_gdn_conv,
           state_ssm_h, state_ssm_conv, norm_g, ffn_w_in, ffn_w_out, w_in, w_out,
           gdn_conv_w, gdn_a_log, gdn_dt_bias, gdn_norm_g, mlp_ln_g, mlp_ln_b, mlp_ws, mlp_bs,
           ssm_conv_w, ssm_conv_b, ssm_a_log, ssm_dt_bias, ssm_d, ssm_norm_g,
           mla_q_norm_g, mla_w_uq, mla_kv_norm_g, mla_w_uk, mla_w_uv):
    bp, lp, _ = x_prompt.shape
    bs, ls, _ = x_sample.shape
    assert lp % SEQ_TILE == 0 and CONV_K - 1 <= ls <= SAMPLE_ROWS
    w_in_p = jnp.pad(w_in[:, :, _PERM], ((0, 0), (0, 0), (0, PROJ_COLS - PROJ_USED)))
    cache_kpe_t = jnp.swapaxes(cache_kpe, 2, 3)
    weights = dict(norm_g=norm_g, ffn_w_in=ffn_w_in.astype(BF16), ffn_w_out=ffn_w_out.astype(BF16),
                   w_in=w_in_p.astype(BF16), w_out=w_out.astype(BF16),
                   gdn_conv_w=gdn_conv_w, gdn_a_log=gdn_a_log, gdn_dt_bias=gdn_dt_bias, gdn_norm_g=gdn_norm_g,
                   mlp_ln_g=mlp_ln_g, mlp_ln_b=mlp_ln_b, mlp_ws=mlp_ws, mlp_bs=mlp_bs,
                   ssm_conv_w=ssm_conv_w, ssm_conv_b=ssm_conv_b, ssm_a_log=ssm_a_log,
                   ssm_dt_bias=ssm_dt_bias, ssm_d=ssm_d, ssm_norm_g=ssm_norm_g,
                   mla_q_norm_g=mla_q_norm_g, mla_kv_norm_g=mla_kv_norm_g)
    xp = x_prompt.reshape(bp * lp, D_MODEL)
    xs = jnp.pad(x_sample, ((0, 0), (0, SAMPLE_ROWS - ls), (0, 0))).reshape(bs * SAMPLE_ROWS, D_MODEL)
    zeros_p = (jnp.zeros((bp, CONV_K - 1, 3 * GROUP_W), F32), jnp.zeros((bp, GDN_HEADS, GDN_DK, GDN_DV), F32),
               jnp.zeros((bp, CONV_K - 1, SSM_CONV_W), F32), jnp.zeros((bp, SSM_HEADS, SSM_HEAD_DIM, SSM_STATE), F32))
    st_p, st_s = [], []
    for l in range(DEPTH):
        wl = {name: arr[l] for name, arr in weights.items()}
        wl['mla_w_nope'], wl['mla_w_pe'], wl['mla_w_uk'], wl['mla_w_uv'] = _mla_weights(
            mla_w_uq[l], mla_w_uk[l], mla_w_uv[l])
        xp, st = _trunk_layer(xp, wl, l, bp, lp, SEQ_TILE, lp, *zeros_p, None)
        st_p.append(st)
        xs, st = _trunk_layer(xs, wl, l, bs, SAMPLE_ROWS, SAMPLE_ROWS, ls,
                              state_gdn_conv[l], state_gdn_s[l], state_ssm_conv[l], state_ssm_h[l],
                              (cache_ckv, cache_kpe_t, page_table))
        st_s.append(st)

    def stack(states, i):
        return jnp.stack([s[i] for s in states])

    y_prompt = xp.reshape(bp, lp, D_MODEL)
    y_sample = xs.reshape(bs, SAMPLE_ROWS, D_MODEL)[:, :ls]
    return (y_prompt, y_sample,
            stack(st_p, 0), stack(st_p, 1), stack(st_p, 2), stack(st_p, 3), stack(st_p, 4), stack(st_p, 5),
            stack(st_s, 0), stack(st_s, 1), stack(st_s, 2), stack(st_s, 3), stack(st_s, 4), stack(st_s, 5),
            stack(st_s, 6))
```

```python
import functools

import numpy as np
import jax
import jax.numpy as jnp
from jax import lax
from jax.experimental import pallas as pl
from jax.experimental.pallas import tpu as pltpu

F32 = jnp.float32
BF16 = jnp.bfloat16

D_MODEL = 1024
DEPTH = 2
PAGE_SIZE = 128
GROUP_W = 256
CONV_K = 4
FFN_DIM = 2816
EPS = 1e-6

GDN_HEADS = 4
GDN_DK = 64
GDN_DV = 64
MLP_CHUNK = 128
MLP_GROUPS = 4
MLP_GW = 64
SSM_HEADS = 4
SSM_HEAD_DIM = 64
SSM_GROUPS = 2
SSM_STATE = 128
SSM_CONV_W = 768
MLA_HEADS = 4
MLA_NOPE = 64
MLA_ROPE = 32
MLA_V_DIM = 64
MLA_Q_RANK = 256
MLA_KV_RANK = 128
MLA_SCALE = (MLA_NOPE + MLA_ROPE) ** -0.5
ROPE_THETA = 10000.0

V7X_LANES = 128
V7X_SUBLANES = 8
V7X_MXU_DIM = 256
V7X_VMEM_LIMIT_BYTES = 56 * 1024 * 1024

_SRC_COLS = dict(
    qkv=(0, 768), z_a=(768, 1024), a_a=(1024, 1028), b_a=(1028, 1032),
    u=(1032, 1288), v=(1288, 1544),
    z_c=(1544, 1800), xbc=(1800, 2568), dt=(2568, 2572),
    c_q=(2572, 2828), c_kv=(2828, 2956), k_pe=(2956, 2988))
_DST_ORDER = ('qkv', 'xbc', 'z_a', 'u', 'v', 'z_c', 'c_q', 'c_kv', 'k_pe', 'a_a', 'b_a', 'dt')
_DST_COLS = {}
_off = 0
for _name in _DST_ORDER:
    _w = _SRC_COLS[_name][1] - _SRC_COLS[_name][0]
    _DST_COLS[_name] = (_off, _off + _w)
    _off += _w
PROJ_USED = _off
PROJ_COLS = -(-PROJ_USED // V7X_LANES) * V7X_LANES
_PERM = np.concatenate([np.arange(*_SRC_COLS[n]) for n in _DST_ORDER])
SMALL_BLOCK = _DST_COLS['k_pe'][0] // V7X_LANES
LANE_A = _DST_COLS['a_a'][0] % V7X_LANES
LANE_B = _DST_COLS['b_a'][0] % V7X_LANES
LANE_DT = _DST_COLS['dt'][0] % V7X_LANES


def _col_block(name):
    lo, hi = _DST_COLS[name]
    assert lo % (hi - lo) == 0
    return lo // (hi - lo)


FFN_CHUNK = V7X_MXU_DIM
TOKEN_TILE = 512
SEQ_TILE = 256
SAMPLE_ROWS = V7X_SUBLANES
SAMPLE_SEQS_PER_STEP = 8
GDN_CHUNK = 64
NEG_BIG = -1e30


def _cparams(*sem):
    return pltpu.CompilerParams(dimension_semantics=sem, vmem_limit_bytes=V7X_VMEM_LIMIT_BYTES)


def _rms(x, g):
    return x * lax.rsqrt(jnp.mean(x * x, axis=-1, keepdims=True) + EPS) * g


def _silu(x):
    return x * jax.nn.sigmoid(x)


def _softplus(x):
    return jnp.maximum(x, 0.0) + jnp.log(1.0 + jnp.exp(-jnp.abs(x)))


_NN = (((1,), (0,)), ((), ()))
_NT = (((1,), (1,)), ((), ()))
_TN = (((0,), (0,)), ((), ()))


def _mm(a, b, dims=_NN):
    return lax.dot_general(a.astype(BF16), b.astype(BF16), dims, preferred_element_type=F32)


def _split3(x):
    x0 = x.astype(BF16)
    r1 = x - x0.astype(F32)
    x1 = r1.astype(BF16)
    x2 = (r1 - x1.astype(F32)).astype(BF16)
    return x0, x1, x2


def _mm_exact_lhs(a_bf16, b):
    b0, b1, b2 = _split3(b)
    dot = lambda y: lax.dot_general(a_bf16, y, _NN, preferred_element_type=F32)
    return dot(b0) + (dot(b1) + dot(b2))


def _iota2(n, m, axis):
    return lax.broadcasted_iota(jnp.int32, (n, m), axis)


def _resident(shape):
    return pl.BlockSpec(shape, lambda *_: (0,) * len(shape), pipeline_mode=pl.Buffered(1))


def _ffn_body(x, gpre_ref, gpost_ref, win_ref, wout_ref, o_ref, acc_ref):
    xn = _rms(x, gpre_ref[...]).astype(BF16)
    n_chunks = FFN_DIM // FFN_CHUNK
    for c in range(n_chunks):
        lo = c * FFN_CHUNK
        gate = jnp.dot(xn, win_ref[:, lo:lo + FFN_CHUNK], preferred_element_type=F32)
        up = jnp.dot(xn, win_ref[:, FFN_DIM + lo:FFN_DIM + lo + FFN_CHUNK], preferred_element_type=F32)
        h = (_silu(gate) * up).astype(BF16)
        part = jnp.dot(h, wout_ref[lo:lo + FFN_CHUNK, :], preferred_element_type=F32)
        if c == 0:
            acc_ref[...] = part
        else:
            acc_ref[...] += part
    o_ref[...] = x + 0.5 * _rms(acc_ref[...], gpost_ref[...])


def _ffn_kernel(x_ref, gpre_ref, gpost_ref, win_ref, wout_ref, o_ref, acc_ref):
    _ffn_body(x_ref[...], gpre_ref, gpost_ref, win_ref, wout_ref, o_ref, acc_ref)


def _mix_ffn_kernel(x_ref, ma_ref, mb_ref, mc_ref, md_ref, wo_ref, gmix_ref, gpre_ref, gpost_ref,
                    win_ref, wout_ref, o_ref, acc_ref):
    y = None
    for i, m_ref in enumerate((ma_ref, mb_ref, mc_ref, md_ref)):
        part = jnp.dot(m_ref[...].astype(BF16), wo_ref[i * GROUP_W:(i + 1) * GROUP_W, :],
                       preferred_element_type=F32)
        y = part if y is None else y + part
    x = x_ref[...] + _rms(y, gmix_ref[...])
    _ffn_body(x, gpre_ref, gpost_ref, win_ref, wout_ref, o_ref, acc_ref)


def _ffn(x, g_pre, g_post, w_in, w_out, mix=None, w_o=None, g_mix=None):
    t = x.shape[0]
    tm = min(TOKEN_TILE, t)
    assert t % tm == 0
    tok = pl.BlockSpec((tm, D_MODEL), lambda i: (i, 0))
    vec = _resident((1, D_MODEL))
    w_specs = [_resident((D_MODEL, 2 * FFN_DIM)), _resident((FFN_DIM, D_MODEL))]
    if mix is None:
        kern, args = _ffn_kernel, (x, g_pre, g_post, w_in, w_out)
        in_specs = [tok, vec, vec] + w_specs
    else:
        kern, args = _mix_ffn_kernel, (x, *mix, w_o, g_mix, g_pre, g_post, w_in, w_out)
        part = pl.BlockSpec((tm, GROUP_W), lambda i: (i, 0))
        in_specs = [tok, part, part, part, part, _resident((D_MODEL, D_MODEL)), vec, vec, vec] + w_specs
    return pl.pallas_call(
        kern,
        grid=(t // tm,),
        in_specs=in_specs,
        out_specs=tok,
        out_shape=jax.ShapeDtypeStruct((t, D_MODEL), F32),
        scratch_shapes=[pltpu.VMEM((tm, D_MODEL), F32)],
        compiler_params=_cparams("arbitrary"),
        name="mix_ffn" if mix is not None else "ffn",
    )(*args)


def _proj_kernel(x_ref, g_ref, w_ref, o_ref):
    xn = _rms(x_ref[...], g_ref[...]).astype(BF16)
    o_ref[...] = jnp.dot(xn, w_ref[...], preferred_element_type=F32)


def _proj(x, g, w):
    t = x.shape[0]
    tm = min(TOKEN_TILE, t)
    assert t % tm == 0
    return pl.pallas_call(
        _proj_kernel,
        grid=(t // tm,),
        in_specs=[pl.BlockSpec((tm, D_MODEL), lambda i: (i, 0)),
                  _resident((1, D_MODEL)), _resident((D_MODEL, PROJ_COLS))],
        out_specs=pl.BlockSpec((tm, PROJ_COLS), lambda i: (i, 0)),
        out_shape=jax.ShapeDtypeStruct((t, PROJ_COLS), F32),
        compiler_params=_cparams("arbitrary"),
        name="in_proj",
    )(x, g, w)


def _seq_spec(rows, name, n_steps):
    lo, hi = _DST_COLS[name]
    return pl.BlockSpec((rows, hi - lo), lambda b, n, c=_col_block(name): (b * n_steps + n, c))


def _small_spec(rows, n_steps):
    return pl.BlockSpec((rows, V7X_LANES), lambda b, n: (b * n_steps + n, SMALL_BLOCK))


def _out_spec(rows, width, n_steps):
    return pl.BlockSpec((rows, width), lambda b, n: (b * n_steps + n, 0))


def _causal_conv(x, halo_ref, conv0_ref, w_ref, rows, seq_rows):
    first = V7X_SUBLANES - (CONV_K - 1)

    def taps(i):
        y = w_ref[0:1, :] * halo_ref[i, first:first + seq_rows, :]
        for j in range(1, CONV_K):
            y = y + w_ref[j:j + 1, :] * halo_ref[i, first + j:first + j + seq_rows, :]
        return y

    if seq_rows == rows:
        @pl.when(pl.program_id(1) == 0)
        def _():
            halo_ref[0, first:V7X_SUBLANES, :] = conv0_ref[0]

        halo_ref[0, V7X_SUBLANES:V7X_SUBLANES + rows, :] = x
        y = taps(0)
        halo_ref[0, 0:V7X_SUBLANES, :] = x[rows - V7X_SUBLANES:rows, :]
        return y
    parts = []
    for i in range(rows // seq_rows):
        halo_ref[i, first:V7X_SUBLANES, :] = conv0_ref[i]
        halo_ref[i, V7X_SUBLANES:V7X_SUBLANES + seq_rows, :] = x[i * seq_rows:(i + 1) * seq_rows, :]
        parts.append(taps(i))
    return jnp.concatenate(parts, axis=0)


def _level_masks(rows):
    i = np.arange(rows)[:, None]
    j = np.arange(rows)[None, :]
    out = []
    s = 1
    while s < rows:
        out.append(((i // (2 * s) == j // (2 * s)) & (i % (2 * s) >= s) & (j % (2 * s) < s)).astype(np.float32))
        s *= 2
    return np.stack(out)


def _unit_lower_inverse(a_strict, lv_ref, eye):
    t = eye - a_strict * lv_ref[0]
    for lv in range(1, lv_ref.shape[0]):
        a_s = a_strict * lv_ref[lv]
        t = t - _mm(_mm(t, a_s), t)
    return t


def _gdn_kernel(qkv_ref, z_ref, sm_ref, conv0_ref, s0_ref, cw_ref, lane_ref, g_ref, lv_ref,
                o_ref, sfin_ref, halo_ref, s_ref, *, rows, chunk, n_valid, chained):
    n_chunks = rows // chunk
    shift = chunk.bit_length() - 1
    if chained:
        @pl.when(pl.program_id(1) == 0)
        def _():
            s_ref[...] = s0_ref[0]

    act = _silu(_causal_conv(qkv_ref[...], halo_ref, conv0_ref, cw_ref, rows, rows if chained else chunk))
    sm = sm_ref[...]
    row = _iota2(rows, rows, 0)
    col = _iota2(rows, rows, 1)
    same_chunk = lax.shift_right_logical(row, shift) == lax.shift_right_logical(col, shift)
    tril_blocks = jnp.where(same_chunk, (col <= row).astype(F32), 0.0).astype(BF16)
    valid = (_iota2(rows, V7X_LANES, 0) & (chunk - 1)) < n_valid
    log_alpha = jnp.where(valid, -jnp.exp(lane_ref[0:1, :]) * _softplus(sm + lane_ref[1:2, :]), 0.0)
    beta = jnp.where(valid, jax.nn.sigmoid(sm), 0.0)
    gcum = _mm_exact_lhs(tril_blocks, log_alpha)
    gcum_t = gcum.T
    rc = _iota2(chunk, chunk, 0)
    cc = _iota2(chunk, chunk, 1)
    tril = cc <= rc
    strict = cc < rc
    eye = (cc == rc).astype(F32)
    z = z_ref[...]
    heads = range(GDN_HEADS)
    chunks = range(n_chunks)
    chains = [(h, c) for h in heads for c in chunks]
    rsl = [slice(c * chunk, (c + 1) * chunk) for c in chunks]
    q, k, kb, rhs, qd, gcol = {}, {}, {}, {}, {}, {}
    for h in heads:
        lane = LANE_A + h
        qh = act[:, h * GDN_DK:(h + 1) * GDN_DK]
        kh = act[:, GROUP_W + h * GDN_DK:GROUP_W + (h + 1) * GDN_DK]
        vh = act[:, 2 * GROUP_W + h * GDN_DV:2 * GROUP_W + (h + 1) * GDN_DV]
        qh = qh * lax.rsqrt(jnp.sum(qh * qh, axis=-1, keepdims=True) + EPS) * GDN_DK ** -0.5
        kh = kh * lax.rsqrt(jnp.sum(kh * kh, axis=-1, keepdims=True) + EPS)
        g_h = gcum[:, lane:lane + 1]
        b_h = beta[:, LANE_B + h:LANE_B + h + 1]
        e_h = jnp.exp(g_h)
        kb_h = kh * b_h
        rhs_h = jnp.concatenate([kb_h * e_h, vh * b_h], axis=1)
        qd_h = qh * e_h
        for c in chunks:
            q[h, c], k[h, c], kb[h, c] = qh[rsl[c]], kh[rsl[c]], kb_h[rsl[c]]
            rhs[h, c], qd[h, c], gcol[h, c] = rhs_h[rsl[c]], qd_h[rsl[c]], g_h[rsl[c]]
    decay = {(h, c): jnp.exp(jnp.where(tril, gcol[h, c] - gcum_t[LANE_A + h:LANE_A + h + 1, rsl[c]], NEG_BIG))
             for h, c in chains}
    kk = {i: _mm(kb[i], k[i], _NT) for i in chains}
    qk = {i: _mm(q[i], k[i], _NT) * decay[i] for i in chains}
    a_strict = {i: jnp.where(strict, kk[i] * decay[i], 0.0) for i in chains}
    t_inv = {i: eye - a_strict[i] * lv_ref[0] for i in chains}
    for lv in range(1, lv_ref.shape[0]):
        half = {i: _mm(t_inv[i], a_strict[i] * lv_ref[lv]) for i in chains}
        t_inv = {i: t_inv[i] - _mm(half[i], t_inv[i]) for i in chains}
    sol = {i: _mm(t_inv[i], rhs[i]) for i in chains}
    glast = {i: gcol[i][chunk - 1:chunk] for i in chains}
    k_dec = {i: k[i] * jnp.exp(glast[i] - gcol[i]) for i in chains}
    outs = {}
    finals = [[None] * GDN_HEADS for _ in range(1 if chained else n_chunks)]
    state = {h: s_ref[h] for h in heads} if chained else None
    for c in chunks:
        cur = state if chained else {h: s0_ref[c, h] for h in heads}
        w_s = {h: _mm(sol[h, c][:, :GDN_DK], cur[h]) for h in heads}
        o_s = {h: _mm(qd[h, c], cur[h]) for h in heads}
        u_new = {h: sol[h, c][:, GDN_DK:] - w_s[h] for h in heads}
        for h in heads:
            outs[h, c] = o_s[h] + _mm(qk[h, c], u_new[h])
        nxt = {h: cur[h] * jnp.exp(glast[h, c]) + _mm(k_dec[h, c], u_new[h], _TN) for h in heads}
        if chained:
            state = nxt
        else:
            finals[c] = [nxt[h] for h in heads]
    if chained:
        finals[0] = [state[h] for h in heads]
    head_outs = []
    for h in heads:
        o = outs[h, 0] if n_chunks == 1 else jnp.concatenate([outs[h, c] for c in chunks], axis=0)
        head_outs.append(_rms(o, g_ref[...]) * _silu(z[:, h * GDN_DV:(h + 1) * GDN_DV]))
    o_ref[...] = jnp.concatenate(head_outs, axis=1)
    new_states = jnp.stack([jnp.stack(per_seq) for per_seq in finals])
    sfin_ref[...] = new_states
    if chained:
        s_ref[...] = new_states[0]


def _gdn(p, conv0, s0, conv_w, a_log, dt_bias, norm_g, n_seq, seq_len, rows, chunk, n_valid):
    chained = seq_len > chunk
    seqs_per_step = 1 if chained else rows // chunk
    n_steps = seq_len // rows if chained else 1
    n_blocks = n_seq // seqs_per_step
    assert n_blocks * seqs_per_step == n_seq and chunk & (chunk - 1) == 0
    lane = jnp.zeros((2, V7X_LANES), F32)
    lane = lane.at[0, LANE_A:LANE_A + GDN_HEADS].set(a_log).at[1, LANE_A:LANE_A + GDN_HEADS].set(dt_bias)
    levels = jnp.asarray(_level_masks(chunk))
    halo_rows = V7X_SUBLANES + (rows if chained else chunk)
    return pl.pallas_call(
        functools.partial(_gdn_kernel, rows=rows, chunk=chunk, n_valid=n_valid, chained=chained),
        grid=(n_blocks, n_steps),
        in_specs=[_seq_spec(rows, 'qkv', n_steps), _seq_spec(rows, 'z_a', n_steps), _small_spec(rows, n_steps),
                  pl.BlockSpec((seqs_per_step, CONV_K - 1, 3 * GROUP_W), lambda b, n: (b, 0, 0)),
                  pl.BlockSpec((seqs_per_step, GDN_HEADS, GDN_DK, GDN_DV), lambda b, n: (b, 0, 0, 0)),
                  _resident((CONV_K, 3 * GROUP_W)), _resident((2, V7X_LANES)), _resident((1, GDN_DV)),
                  _resident(levels.shape)],
        out_specs=[_out_spec(rows, GROUP_W, n_steps),
                   pl.BlockSpec((seqs_per_step, GDN_HEADS, GDN_DK, GDN_DV), lambda b, n: (b, 0, 0, 0))],
        out_shape=[jax.ShapeDtypeStruct((n_seq * seq_len, GROUP_W), F32),
                   jax.ShapeDtypeStruct((n_seq, GDN_HEADS, GDN_DK, GDN_DV), F32)],
        scratch_shapes=[pltpu.VMEM((seqs_per_step, halo_rows, 3 * GROUP_W), F32),
                        pltpu.VMEM((GDN_HEADS, GDN_DK, GDN_DV), F32)],
        compiler_params=_cparams("arbitrary", "arbitrary"),
        name="gdn",
    )(p, p, p, conv0, s0, conv_w, lane, norm_g.reshape(1, GDN_DV), levels)


def _gmlp_kernel(u_ref, v_ref, lng_ref, lnb_ref, ws_ref, bias_ref, o_ref, vb_ref, *, rows, chunk):
    u = jax.nn.gelu(u_ref[...], approximate=True)
    v = jax.nn.gelu(v_ref[...], approximate=True)
    mu = jnp.mean(v, axis=-1, keepdims=True)
    var = jnp.mean(jnp.square(v - mu), axis=-1, keepdims=True)
    v = (v - mu) * lax.rsqrt(var + EPS) * lng_ref[...] + lnb_ref[...]
    vb_ref[...] = v
    tril = _iota2(chunk, chunk, 1) <= _iota2(chunk, chunk, 0)
    ws = [jnp.where(tril, ws_ref[g], 0.0).astype(BF16) for g in range(MLP_GROUPS)]
    for c in range(rows // chunk):
        rs = slice(c * chunk, (c + 1) * chunk)
        for g in range(MLP_GROUPS):
            ls = slice(g * MLP_GW, (g + 1) * MLP_GW)
            mixed = _mm(ws[g], v[rs, ls]) + bias_ref[:, ls]
            o_ref[rs, ls] = u[rs, ls] * mixed


def _gmlp(p, ln_g, ln_b, ws, bs, n_seq, seq_len, rows, chunk):
    n_steps = seq_len // rows
    bias = jnp.repeat(bs[:, :chunk].T, MLP_GW, axis=1)
    return pl.pallas_call(
        functools.partial(_gmlp_kernel, rows=rows, chunk=chunk),
        grid=(n_seq, n_steps),
        in_specs=[_seq_spec(rows, 'u', n_steps), _seq_spec(rows, 'v', n_steps),
                  _resident((1, GROUP_W)), _resident((1, GROUP_W)),
                  _resident((MLP_GROUPS, chunk, chunk)), _resident((chunk, GROUP_W))],
        out_specs=[_out_spec(rows, GROUP_W, n_steps), _out_spec(rows, GROUP_W, n_steps)],
        out_shape=[jax.ShapeDtypeStruct((n_seq * seq_len, GROUP_W), F32)] * 2,
        compiler_params=_cparams("arbitrary", "arbitrary"),
        name="gmlp",
    )(p, p, ln_g.reshape(1, GROUP_W), ln_b.reshape(1, GROUP_W), ws[:, :chunk, :chunk], bias)


def _ssd_kernel(xbc_ref, z_ref, sm_ref, conv0_ref, h0_ref, cw_ref, cb_ref, lane_ref, g_ref,
                o_ref, hfin_ref, halo_ref, h_ref, *, rows, n_valid):
    @pl.when(pl.program_id(1) == 0)
    def _():
        h_ref[...] = h0_ref[0]

    act = _silu(_causal_conv(xbc_ref[...], halo_ref, conv0_ref, cw_ref, rows, rows) + cb_ref[...])
    sm = sm_ref[...]
    tril = _iota2(rows, rows, 1) <= _iota2(rows, rows, 0)
    valid = _iota2(rows, V7X_LANES, 0) < n_valid
    dt = jnp.where(valid, _softplus(sm + lane_ref[1:2, :]), 0.0)
    acum = _mm_exact_lhs(tril.astype(BF16), dt * -jnp.exp(lane_ref[0:1, :]))
    acum_t = acum.T
    heads = range(SSM_HEADS)
    group_of = [h // (SSM_HEADS // SSM_GROUPS) for h in heads]
    b_g = [act[:, GROUP_W + g * SSM_STATE:GROUP_W + (g + 1) * SSM_STATE] for g in range(SSM_GROUPS)]
    c_g = [act[:, GROUP_W + (SSM_GROUPS + g) * SSM_STATE:GROUP_W + (SSM_GROUPS + g + 1) * SSM_STATE]
           for g in range(SSM_GROUPS)]
    cb = [_mm(c_g[g], b_g[g], _NT) for g in range(SSM_GROUPS)]
    acol = [acum[:, LANE_DT + h:LANE_DT + h + 1] for h in heads]
    alast = [acum[rows - 1:rows, LANE_DT + h:LANE_DT + h + 1] for h in heads]
    decay = [jnp.exp(jnp.where(tril, acol[h] - acum_t[LANE_DT + h:LANE_DT + h + 1, :], NEG_BIG)) for h in heads]
    x = [act[:, h * SSM_HEAD_DIM:(h + 1) * SSM_HEAD_DIM] for h in heads]
    xdt = [x[h] * dt[:, LANE_DT + h:LANE_DT + h + 1] for h in heads]
    state = [h_ref[h] for h in heads]
    y_intra = [_mm(cb[group_of[h]] * decay[h], xdt[h]) for h in heads]
    y_inter = [_mm(c_g[group_of[h]] * jnp.exp(acol[h]), state[h], _NT) for h in heads]
    h_new = [state[h] * jnp.exp(alast[h]) + _mm(xdt[h], b_g[group_of[h]] * jnp.exp(alast[h] - acol[h]), _TN)
             for h in heads]
    y = jnp.concatenate([y_intra[h] + y_inter[h] + lane_ref[2:3, LANE_DT + h:LANE_DT + h + 1] * x[h]
                         for h in heads], axis=1)
    new_states = jnp.stack(h_new)
    h_ref[...] = new_states
    hfin_ref[0] = new_states
    o_ref[...] = _rms(y * _silu(z_ref[...]), g_ref[...])


def _ssd(p, conv0, h0, conv_w, conv_b, a_log, dt_bias, d_skip, norm_g, n_seq, seq_len, rows, n_valid):
    n_steps = seq_len // rows
    lane = jnp.zeros((3, V7X_LANES), F32)
    lane = (lane.at[0, LANE_DT:LANE_DT + SSM_HEADS].set(a_log)
            .at[1, LANE_DT:LANE_DT + SSM_HEADS].set(dt_bias)
            .at[2, LANE_DT:LANE_DT + SSM_HEADS].set(d_skip))
    return pl.pallas_call(
        functools.partial(_ssd_kernel, rows=rows, n_valid=n_valid),
        grid=(n_seq, n_steps),
        in_specs=[_seq_spec(rows, 'xbc', n_steps), _seq_spec(rows, 'z_c', n_steps), _small_spec(rows, n_steps),
                  pl.BlockSpec((1, CONV_K - 1, SSM_CONV_W), lambda b, n: (b, 0, 0)),
                  pl.BlockSpec((1, SSM_HEADS, SSM_HEAD_DIM, SSM_STATE), lambda b, n: (b, 0, 0, 0)),
                  _resident((CONV_K, SSM_CONV_W)), _resident((1, SSM_CONV_W)), _resident((3, V7X_LANES)),
                  _resident((1, GROUP_W))],
        out_specs=[_out_spec(rows, GROUP_W, n_steps),
                   pl.BlockSpec((1, SSM_HEADS, SSM_HEAD_DIM, SSM_STATE), lambda b, n: (b, 0, 0, 0))],
        out_shape=[jax.ShapeDtypeStruct((n_seq * seq_len, GROUP_W), F32),
                   jax.ShapeDtypeStruct((n_seq, SSM_HEADS, SSM_HEAD_DIM, SSM_STATE), F32)],
        scratch_shapes=[pltpu.VMEM((1, V7X_SUBLANES + rows, SSM_CONV_W), F32),
                        pltpu.VMEM((SSM_HEADS, SSM_HEAD_DIM, SSM_STATE), F32)],
        compiler_params=_cparams("arbitrary", "arbitrary"),
        name="ssd",
    )(p, p, p, conv0, h0, conv_w, conv_b.reshape(1, SSM_CONV_W), lane, norm_g.reshape(1, GROUP_W))


def _rope_lanes(x, cos, sin_signed):
    lane = _iota2(x.shape[0], V7X_LANES, 1)
    half = MLA_ROPE // 2
    swapped = jnp.where(lane % MLA_ROPE < half,
                        pltpu.roll(x, V7X_LANES - half, axis=1), pltpu.roll(x, half, axis=1))
    return x * cos + swapped * sin_signed


def _rope_tables(pos):
    half = MLA_ROPE // 2
    inv_freq = ROPE_THETA ** (-jnp.arange(half, dtype=F32) / half)
    ang = pos.astype(F32)[:, None] * inv_freq[None, :]
    cos, sin = jnp.cos(ang), jnp.sin(ang)
    reps = V7X_LANES // MLA_ROPE
    return jnp.tile(jnp.concatenate([cos, cos], axis=1), (1, reps)), jnp.tile(jnp.concatenate([-sin, sin], axis=1), (1, reps))


def _mla_project(cq_ref, ckv_ref, sm_ref, cos_ref, sin_ref, gq_ref, gkv_ref, wn_ref, wpe_ref, wuk_ref):
    c_q = _rms(cq_ref[...], gq_ref[...]).astype(BF16)
    c_kv = _rms(ckv_ref[...], gkv_ref[...])
    cos, sin = cos_ref[...], sin_ref[...]
    lane = _iota2(c_kv.shape[0], V7X_LANES, 1)
    k_pe = _rope_lanes(jnp.where(lane < MLA_ROPE, sm_ref[...], 0.0), cos, sin)
    q_nope = jnp.dot(c_q, wn_ref[...], preferred_element_type=F32)
    q_heads = []
    for h in range(MLA_HEADS):
        q_lat = _mm(q_nope[:, h * MLA_NOPE:(h + 1) * MLA_NOPE], wuk_ref[h], _NT)
        q_pe = _rope_lanes(jnp.dot(c_q, wpe_ref[h], preferred_element_type=F32), cos, sin)
        q_heads.append((jnp.concatenate([q_lat, q_pe], axis=1) * MLA_SCALE).astype(BF16))
    return c_kv, k_pe, q_heads


def _softmax_step(q, keys, m, l, acc, mask=None):
    s = lax.dot_general(q, keys, _NT, preferred_element_type=F32)
    if mask is not None:
        s = jnp.where(mask, s, NEG_BIG)
    m_new = jnp.maximum(m, jnp.max(s, axis=-1, keepdims=True))
    p = jnp.exp(s - m_new)
    alpha = jnp.exp(m - m_new)
    l = alpha * l + jnp.sum(p, axis=-1, keepdims=True)
    acc = alpha * acc + jnp.dot(p.astype(BF16), keys[:, :MLA_KV_RANK], preferred_element_type=F32)
    return m_new, l, acc


def _mla_prompt_kernel(cq_ref, ckv_ref, sm_ref, cos_ref, sin_ref, gq_ref, gkv_ref, wn_ref, wpe_ref, wuk_ref,
                       wuv_ref, o_ref, ckv_out_ref, kpe_out_ref, keys_ref, vt_ref, q_ref, m_ref, l_ref, acc_ref,
                       *, rows):
    n = pl.program_id(1)
    qrows = MLA_HEADS * rows
    c_kv, k_pe, q_heads = _mla_project(cq_ref, ckv_ref, sm_ref, cos_ref, sin_ref, gq_ref, gkv_ref,
                                       wn_ref, wpe_ref, wuk_ref)
    ckv_out_ref[0] = c_kv
    kpe_out_ref[0] = k_pe[:, :MLA_ROPE]
    base = pl.multiple_of(n * rows, rows)
    keys_ref[pl.ds(base, rows), :] = jnp.concatenate([c_kv, k_pe], axis=1).astype(BF16)
    vt_ref[:, pl.ds(base, rows)] = c_kv.T.astype(BF16)
    q_ref[...] = jnp.concatenate(q_heads, axis=0)
    m_ref[...] = jnp.full((1, qrows), NEG_BIG, F32)
    l_ref[...] = jnp.zeros((1, qrows), F32)
    acc_ref[...] = jnp.zeros((MLA_KV_RANK, qrows), F32)

    def attend(off, mask):
        s = lax.dot_general(keys_ref[pl.ds(off, rows), :], q_ref[...], _NT, preferred_element_type=F32)
        if mask is not None:
            s = jnp.where(mask, s, NEG_BIG)
        m_old = m_ref[...]
        m_new = jnp.maximum(m_old, jnp.max(s, axis=0, keepdims=True))
        p = jnp.exp(s - m_new)
        alpha = jnp.exp(m_old - m_new)
        l_ref[...] = alpha * l_ref[...] + jnp.sum(p, axis=0, keepdims=True)
        acc_ref[...] = alpha * acc_ref[...] + jnp.dot(vt_ref[:, pl.ds(off, rows)], p.astype(BF16),
                                                      preferred_element_type=F32)
        m_ref[...] = m_new

    def body(j, carry):
        attend(pl.multiple_of(j * rows, rows), None)
        return carry

    lax.fori_loop(0, n, body, 0)
    causal = _iota2(rows, qrows, 0) <= (_iota2(rows, qrows, 1) & (rows - 1))
    attend(base, causal)
    o_lat_t = acc_ref[...] / l_ref[...]
    outs = [_mm(o_lat_t[:, h * rows:(h + 1) * rows], wuv_ref[h], _TN) for h in range(MLA_HEADS)]
    o_ref[...] = jnp.concatenate(outs, axis=1)


def _mla_weights(w_uq, w_uk, w_uv):
    w4 = w_uq.reshape(MLA_Q_RANK, MLA_HEADS, MLA_NOPE + MLA_ROPE)
    w_nope = w4[:, :, :MLA_NOPE].reshape(MLA_Q_RANK, MLA_HEADS * MLA_NOPE).astype(BF16)
    w_pe = jnp.pad(w4[:, :, MLA_NOPE:].transpose(1, 0, 2),
                   ((0, 0), (0, 0), (0, V7X_LANES - MLA_ROPE))).astype(BF16)
    return w_nope, w_pe, w_uk.astype(BF16), w_uv.astype(BF16)


def _mla_prompt(p, q_norm_g, kv_norm_g, w_nope, w_pe, w_uk, w_uv, n_seq, seq_len, rows):
    n_steps = seq_len // rows
    cos, sin = _rope_tables(jnp.arange(seq_len))
    tab = pl.BlockSpec((rows, V7X_LANES), lambda b, n: (n, 0))
    return pl.pallas_call(
        functools.partial(_mla_prompt_kernel, rows=rows),
        grid=(n_seq, n_steps),
        in_specs=[_seq_spec(rows, 'c_q', n_steps), _seq_spec(rows, 'c_kv', n_steps), _small_spec(rows, n_steps),
                  tab, tab, _resident((1, MLA_Q_RANK)), _resident((1, MLA_KV_RANK)),
                  _resident(w_nope.shape), _resident(w_pe.shape), _resident(w_uk.shape), _resident(w_uv.shape)],
        out_specs=[_out_spec(rows, GROUP_W, n_steps),
                   pl.BlockSpec((1, rows, MLA_KV_RANK), lambda b, n: (b, n, 0)),
                   pl.BlockSpec((1, rows, MLA_ROPE), lambda b, n: (b, n, 0))],
        out_shape=[jax.ShapeDtypeStruct((n_seq * seq_len, GROUP_W), F32),
                   jax.ShapeDtypeStruct((n_seq, seq_len, MLA_KV_RANK), F32),
                   jax.ShapeDtypeStruct((n_seq, seq_len, MLA_ROPE), F32)],
        scratch_shapes=[pltpu.VMEM((seq_len, 2 * V7X_LANES), BF16),
                        pltpu.VMEM((MLA_KV_RANK, seq_len), BF16),
                        pltpu.VMEM((MLA_HEADS * rows, 2 * V7X_LANES), BF16),
                        pltpu.VMEM((1, MLA_HEADS * rows), F32), pltpu.VMEM((1, MLA_HEADS * rows), F32),
                        pltpu.VMEM((MLA_KV_RANK, MLA_HEADS * rows), F32)],
        compiler_params=_cparams("arbitrary", "arbitrary"),
        name="mla_prompt",
    )(p, p, p, cos, sin, q_norm_g.reshape(1, MLA_Q_RANK), kv_norm_g.reshape(1, MLA_KV_RANK),
      w_nope, w_pe, w_uk, w_uv)


def _mla_decode_kernel(pt_ref, cq_ref, ckv_ref, sm_ref, cos_ref, sin_ref, gq_ref, gkv_ref, wn_ref, wpe_ref,
                       wuk_ref, wuv_ref, ckv_hbm, kpe_hbm, o_ref, ckv_out_ref, kpe_out_ref,
                       lat_buf, pe_buf, sems, s_ref, latb_ref, *, rows, layer, n_pages, pages_per_chunk):
    seq = pl.program_id(0)
    slot = lax.rem(seq, 2)
    qrows = MLA_HEADS * rows
    chunk = pages_per_chunk * PAGE_SIZE
    n_chunks = n_pages // pages_per_chunk

    def page_copies(src_seq, dst_slot, g):
        page = pt_ref[src_seq, g]
        return (pltpu.make_async_copy(ckv_hbm.at[layer, page], lat_buf.at[dst_slot, g], sems.at[0, dst_slot]),
                pltpu.make_async_copy(kpe_hbm.at[layer, page], pe_buf.at[dst_slot, g], sems.at[1, dst_slot]))

    def fetch(src_seq, dst_slot):
        def body(g, carry):
            for cp in page_copies(src_seq, dst_slot, g):
                cp.start()
            return carry
        lax.fori_loop(0, n_pages, body, 0)

    @pl.when(seq == 0)
    def _():
        fetch(0, 0)

    @pl.when(seq + 1 < pl.num_programs(0))
    def _():
        fetch(seq + 1, 1 - slot)

    c_kv, k_pe, q_heads = _mla_project(cq_ref, ckv_ref, sm_ref, cos_ref, sin_ref, gq_ref, gkv_ref,
                                       wn_ref, wpe_ref, wuk_ref)
    ckv_out_ref[0] = c_kv
    kpe_out_ref[0] = k_pe[:, :MLA_ROPE]
    knew = jnp.concatenate([c_kv, k_pe], axis=1).astype(BF16)
    q = jnp.concatenate(q_heads, axis=0)
    q_lat, q_pe = q[:, :MLA_KV_RANK], q[:, MLA_KV_RANK:MLA_KV_RANK + MLA_ROPE]
    causal = _iota2(qrows, rows, 1) <= (_iota2(qrows, rows, 0) & (rows - 1))
    s_new = jnp.where(causal, lax.dot_general(q, knew, _NT, preferred_element_type=F32), NEG_BIG)

    def wait_body(g, carry):
        for cp in page_copies(seq, slot, g):
            cp.wait()
        return carry
    lax.fori_loop(0, n_pages, wait_body, 0)

    def to_bf16(g, carry):
        latb_ref[pl.ds(pl.multiple_of(g * PAGE_SIZE, PAGE_SIZE), PAGE_SIZE), :] = lat_buf[slot, g].astype(BF16)
        return carry
    lax.fori_loop(0, n_pages, to_bf16, 0, unroll=8)

    m = jnp.max(s_new, axis=-1, keepdims=True)
    for c in range(n_chunks):
        s_lat = lax.dot_general(q_lat, latb_ref[c * chunk:(c + 1) * chunk, :], _NT, preferred_element_type=F32)
        s_pe = jnp.concatenate(
            [jnp.dot(q_pe, pe_buf[slot, c * pages_per_chunk + u].astype(BF16), preferred_element_type=F32)
             for u in range(pages_per_chunk)], axis=1)
        s = s_lat + s_pe
        s_ref[:, c * chunk:(c + 1) * chunk] = s
        m = jnp.maximum(m, jnp.max(s, axis=-1, keepdims=True))

    p_new = jnp.exp(s_new - m)
    l = jnp.sum(p_new, axis=-1, keepdims=True)
    acc = jnp.dot(p_new.astype(BF16), knew[:, :MLA_KV_RANK], preferred_element_type=F32)
    for c in range(n_chunks):
        pr = jnp.exp(s_ref[:, c * chunk:(c + 1) * chunk] - m)
        l = l + jnp.sum(pr, axis=-1, keepdims=True)
        acc = acc + jnp.dot(pr.astype(BF16), latb_ref[c * chunk:(c + 1) * chunk, :], preferred_element_type=F32)
    o_lat = acc / l
    outs = [_mm(o_lat[h * rows:(h + 1) * rows], wuv_ref[h]) for h in range(MLA_HEADS)]
    o_ref[...] = jnp.concatenate(outs, axis=1)


def _mla_decode(p, cache_ckv, cache_kpe, layer, page_table, q_norm_g, kv_norm_g, w_nope, w_pe, w_uk, w_uv,
                n_seq, rows, pages_per_chunk=16):
    n_pages = page_table.shape[1]
    pages_per_chunk = min(pages_per_chunk, n_pages)
    assert n_pages % pages_per_chunk == 0
    past_len = n_pages * PAGE_SIZE
    cos, sin = _rope_tables(past_len + jnp.arange(rows))

    def const(shape):
        return pl.BlockSpec(shape, lambda s, pt: (0,) * len(shape))

    def seq(name):
        lo, hi = _DST_COLS[name]
        return pl.BlockSpec((rows, hi - lo), lambda s, pt, c=_col_block(name): (s, c))

    in_specs = [seq('c_q'), seq('c_kv'), pl.BlockSpec((rows, V7X_LANES), lambda s, pt: (s, SMALL_BLOCK)),
                const((rows, V7X_LANES)), const((rows, V7X_LANES)),
                const((1, MLA_Q_RANK)), const((1, MLA_KV_RANK)),
                const(w_nope.shape), const(w_pe.shape), const(w_uk.shape), const(w_uv.shape),
                pl.BlockSpec(memory_space=pl.ANY), pl.BlockSpec(memory_space=pl.ANY)]
    qrows = MLA_HEADS * rows
    return pl.pallas_call(
        functools.partial(_mla_decode_kernel, rows=rows, layer=layer, n_pages=n_pages,
                          pages_per_chunk=pages_per_chunk),
        grid_spec=pltpu.PrefetchScalarGridSpec(
            num_scalar_prefetch=1,
            grid=(n_seq,),
            in_specs=in_specs,
            out_specs=[pl.BlockSpec((rows, GROUP_W), lambda s, pt: (s, 0)),
                       pl.BlockSpec((1, rows, MLA_KV_RANK), lambda s, pt: (s, 0, 0)),
                       pl.BlockSpec((1, rows, MLA_ROPE), lambda s, pt: (s, 0, 0))],
            scratch_shapes=[pltpu.VMEM((2, n_pages, PAGE_SIZE, MLA_KV_RANK), F32),
                            pltpu.VMEM((2, n_pages, MLA_ROPE, PAGE_SIZE), F32),
                            pltpu.SemaphoreType.DMA((2, 2)),
                            pltpu.VMEM((qrows, past_len), F32),
                            pltpu.VMEM((past_len, MLA_KV_RANK), BF16)]),
        out_shape=[jax.ShapeDtypeStruct((n_seq * rows, GROUP_W), F32),
                   jax.ShapeDtypeStruct((n_seq, rows, MLA_KV_RANK), F32),
                   jax.ShapeDtypeStruct((n_seq, rows, MLA_ROPE), F32)],
        compiler_params=_cparams("arbitrary"),
        name="mla_decode",
    )(page_table, p, p, p, cos, sin, q_norm_g.reshape(1, MLA_Q_RANK), kv_norm_g.reshape(1, MLA_KV_RANK),
      w_nope, w_pe, w_uk, w_uv, cache_ckv, cache_kpe)


def _last_rows(p, n_seq, seq_len, n_valid, name):
    lo, hi = _DST_COLS[name]
    return p.reshape(n_seq, seq_len, PROJ_COLS)[:, n_valid - (CONV_K - 1):n_valid, lo:hi]


def _trunk_layer(x, w, layer, n_seq, seq_len, rows, n_valid, gdn_conv, gdn_s, ssm_conv, ssm_h, paged):
    g = w['norm_g']
    assert n_valid == seq_len or seq_len == rows
    step_valid = min(n_valid, rows)
    x = _ffn(x, g[0:1], g[1:2], w['ffn_w_in'][0], w['ffn_w_out'][0])
    p = _proj(x, g[2:3], w['w_in'])
    if seq_len == rows:
        gdn_shape = (rows * SAMPLE_SEQS_PER_STEP, rows, n_valid)
        mlp_shape = (1, n_seq * seq_len, min(SEQ_TILE, n_seq * seq_len), rows)
    else:
        gdn_shape = (rows, GDN_CHUNK, GDN_CHUNK)
        mlp_shape = (n_seq, seq_len, rows, MLP_CHUNK)
    out_a, gdn_s_new = _gdn(p, gdn_conv, gdn_s, w['gdn_conv_w'], w['gdn_a_log'], w['gdn_dt_bias'],
                            w['gdn_norm_g'], n_seq, seq_len, *gdn_shape)
    out_b, v_b = _gmlp(p, w['mlp_ln_g'], w['mlp_ln_b'], w['mlp_ws'], w['mlp_bs'], *mlp_shape)
    out_c, ssm_h_new = _ssd(p, ssm_conv, ssm_h, w['ssm_conv_w'], w['ssm_conv_b'], w['ssm_a_log'],
                            w['ssm_dt_bias'], w['ssm_d'], w['ssm_norm_g'], n_seq, seq_len, rows, step_valid)
    mla_w = (w['mla_q_norm_g'], w['mla_kv_norm_g'], w['mla_w_nope'], w['mla_w_pe'], w['mla_w_uk'], w['mla_w_uv'])
    if paged is None:
        out_d, c_kv, k_pe = _mla_prompt(p, *mla_w, n_seq, seq_len, rows)
    else:
        cache_ckv, cache_kpe, page_table = paged
        out_d, c_kv, k_pe = _mla_decode(p, cache_ckv, cache_kpe, layer, page_table, *mla_w, n_seq, rows)
    x = _ffn(x, g[4:5], g[5:6], w['ffn_w_in'][1], w['ffn_w_out'][1],
             mix=(out_a, out_b, out_c, out_d), w_o=w['w_out'], g_mix=g[3:4])
    gdn_conv_new = _last_rows(p, n_seq, seq_len, n_valid, 'qkv')
    ssm_conv_new = _last_rows(p, n_seq, seq_len, n_valid, 'xbc')
    v_b = v_b.reshape(n_seq, seq_len, GROUP_W)[:, :n_valid]
    return x, (c_kv[:, :n_valid], k_pe[:, :n_valid], gdn_s_new, gdn_conv_new, ssm_h_new, ssm_conv_new, v_b)


def kernel(x_prompt, x_sample, cache_ckv, cache_kpe, page_table, state_gdn_s, state_gdn_conv,
           state_ssm_h, state_ssm_conv, norm_g, ffn_w_in, ffn_w_out, w_in, w_out,
           gdn_conv_w, gdn_a_log, gdn_dt_bias, gdn_norm_g, mlp_ln_g, mlp_ln_b, mlp_ws, mlp_bs,
           ssm_conv_w, ssm_conv_b, ssm_a_log, ssm_dt_bias, ssm_d, ssm_norm_g,
           mla_q_norm_g, mla_w_uq, mla_kv_norm_g, mla_w_uk, mla_w_uv):
    bp, lp, _ = x_prompt.shape
    bs, ls, _ = x_sample.shape
    assert lp % SEQ_TILE == 0 and CONV_K - 1 <= ls <= SAMPLE_ROWS
    w_in_p = jnp.pad(w_in[:, :, _PERM], ((0, 0), (0, 0), (0, PROJ_COLS - PROJ_USED)))
    cache_kpe_t = jnp.swapaxes(cache_kpe, 2, 3)
    weights = dict(norm_g=norm_g, ffn_w_in=ffn_w_in.astype(BF16), ffn_w_out=ffn_w_out.astype(BF16),
                   w_in=w_in_p.astype(BF16), w_out=w_out.astype(BF16),
                   gdn_conv_w=gdn_conv_w, gdn_a_log=gdn_a_log, gdn_dt_bias=gdn_dt_bias, gdn_norm_g=gdn_norm_g,
                   mlp_ln_g=mlp_ln_g, mlp_ln_b=mlp_ln_b, mlp_ws=mlp_ws, mlp_bs=mlp_bs,
                   ssm_conv_w=ssm_conv_w, ssm_conv_b=ssm_conv_b, ssm_a_log=ssm_a_log,
                   ssm_dt_bias=ssm_dt_bias, ssm_d=ssm_d, ssm_norm_g=ssm_norm_g,
                   mla_q_norm_g=mla_q_norm_g, mla_kv_norm_g=mla_kv_norm_g)
    xp = x_prompt.reshape(bp * lp, D_MODEL)
    xs = jnp.pad(x_sample, ((0, 0), (0, SAMPLE_ROWS - ls), (0, 0))).reshape(bs * SAMPLE_ROWS, D_MODEL)
    zeros_p = (jnp.zeros((bp, CONV_K - 1, 3 * GROUP_W), F32), jnp.zeros((bp, GDN_HEADS, GDN_DK, GDN_DV), F32),
               jnp.zeros((bp, CONV_K - 1, SSM_CONV_W), F32), jnp.zeros((bp, SSM_HEADS, SSM_HEAD_DIM, SSM_STATE), F32))
    st_p, st_s = [], []
    for l in range(DEPTH):
        wl = {name: arr[l] for name, arr in weights.items()}
        wl['mla_w_nope'], wl['mla_w_pe'], wl['mla_w_uk'], wl['mla_w_uv'] = _mla_weights(
            mla_w_uq[l], mla_w_uk[l], mla_w_uv[l])
        xp, st = _trunk_layer(xp, wl, l, bp, lp, SEQ_TILE, lp, *zeros_p, None)
        st_p.append(st)
        xs, st = _trunk_layer(xs, wl, l, bs, SAMPLE_ROWS, SAMPLE_ROWS, ls,
                              state_gdn_conv[l], state_gdn_s[l], state_ssm_conv[l], state_ssm_h[l],
                              (cache_ckv, cache_kpe_t, page_table))
        st_s.append(st)

    def stack(states, i):
        return jnp.stack([s[i] for s in states])

    y_prompt = xp.reshape(bp, lp, D_MODEL)
    y_sample = xs.reshape(bs, SAMPLE_ROWS, D_MODEL)[:, :ls]
    return (y_prompt, y_sample,
            stack(st_p, 0), stack(st_p, 1), stack(st_p, 2), stack(st_p, 3), stack(st_p, 4), stack(st_p, 5),
            stack(st_s, 0), stack(st_s, 1), stack(st_s, 2), stack(st_s, 3), stack(st_s, 4), stack(st_s, 5),
            stack(st_s, 6))
```

```python
import functools

import numpy as np
import jax
import jax.numpy as jnp
from jax import lax
from jax.experimental import pallas as pl
from jax.experimental.pallas import tpu as pltpu

F32 = jnp.float32
BF16 = jnp.bfloat16

D_MODEL = 1024
DEPTH = 2
PAGE_SIZE = 128
GROUP_W = 256
CONV_K = 4
FFN_DIM = 2816
EPS = 1e-6

GDN_HEADS = 4
GDN_DK = 64
GDN_DV = 64
MLP_CHUNK = 128
MLP_GROUPS = 4
MLP_GW = 64
SSM_HEADS = 4
SSM_HEAD_DIM = 64
SSM_GROUPS = 2
SSM_STATE = 128
SSM_CONV_W = 768
MLA_HEADS = 4
MLA_NOPE = 64
MLA_ROPE = 32
MLA_V_DIM = 64
MLA_Q_RANK = 256
MLA_KV_RANK = 128
MLA_SCALE = (MLA_NOPE + MLA_ROPE) ** -0.5
ROPE_THETA = 10000.0

V7X_LANES = 128
V7X_SUBLANES = 8
V7X_MXU_DIM = 256
V7X_VMEM_LIMIT_BYTES = 56 * 1024 * 1024

_SRC_COLS = dict(
    qkv=(0, 768), z_a=(768, 1024), a_a=(1024, 1028), b_a=(1028, 1032),
    u=(1032, 1288), v=(1288, 1544),
    z_c=(1544, 1800), xbc=(1800, 2568), dt=(2568, 2572),
    c_q=(2572, 2828), c_kv=(2828, 2956), k_pe=(2956, 2988))
_DST_ORDER = ('qkv', 'xbc', 'z_a', 'u', 'v', 'z_c', 'c_q', 'c_kv', 'k_pe', 'a_a', 'b_a', 'dt')
_DST_COLS = {}
_off = 0
for _name in _DST_ORDER:
    _w = _SRC_COLS[_name][1] - _SRC_COLS[_name][0]
    _DST_COLS[_name] = (_off, _off + _w)
    _off += _w
PROJ_USED = _off
PROJ_COLS = -(-PROJ_USED // V7X_LANES) * V7X_LANES
_PERM = np.concatenate([np.arange(*_SRC_COLS[n]) for n in _DST_ORDER])
SMALL_BLOCK = _DST_COLS['k_pe'][0] // V7X_LANES
LANE_A = _DST_COLS['a_a'][0] % V7X_LANES
LANE_B = _DST_COLS['b_a'][0] % V7X_LANES
LANE_DT = _DST_COLS['dt'][0] % V7X_LANES


def _col_block(name):
    lo, hi = _DST_COLS[name]
    assert lo % (hi - lo) == 0
    return lo // (hi - lo)


FFN_CHUNK = V7X_MXU_DIM
TOKEN_TILE = 512
SEQ_TILE = 256
MLA_Q_TILE = 512
MLA_KV_BLOCK = 256
SAMPLE_ROWS = V7X_SUBLANES
SAMPLE_SEQS_PER_STEP = 8
GDN_CHUNK = 64
GDN_HEAD_GROUP = 4
MLA_VT_ROWS = MLA_KV_RANK + 16
NEG_BIG = -1e30


def _cparams(*sem):
    return pltpu.CompilerParams(dimension_semantics=sem, vmem_limit_bytes=V7X_VMEM_LIMIT_BYTES)


def _rms(x, g):
    return x * lax.rsqrt(jnp.mean(x * x, axis=-1, keepdims=True) + EPS) * g


def _silu(x):
    return x * jax.nn.sigmoid(x)


def _softplus(x):
    return jnp.maximum(x, 0.0) + jnp.log(1.0 + jnp.exp(-jnp.abs(x)))


_NN = (((1,), (0,)), ((), ()))
_NT = (((1,), (1,)), ((), ()))
_TN = (((0,), (0,)), ((), ()))


def _mm(a, b, dims=_NN):
    return lax.dot_general(a.astype(BF16), b.astype(BF16), dims, preferred_element_type=F32)


def _split3(x):
    x0 = x.astype(BF16)
    r1 = x - x0.astype(F32)
    x1 = r1.astype(BF16)
    x2 = (r1 - x1.astype(F32)).astype(BF16)
    return x0, x1, x2


def _mm_exact_lhs(a_bf16, b):
    b0, b1, b2 = _split3(b)
    dot = lambda y: lax.dot_general(a_bf16, y, _NN, preferred_element_type=F32)
    return dot(b0) + (dot(b1) + dot(b2))


def _iota2(n, m, axis):
    return lax.broadcasted_iota(jnp.int32, (n, m), axis)


def _resident(shape):
    return pl.BlockSpec(shape, lambda *_: (0,) * len(shape), pipeline_mode=pl.Buffered(1))


def _ffn_body(x, gpre_ref, gpost_ref, win_ref, wout_ref, o_ref, acc_ref):
    xn = _rms(x, gpre_ref[...]).astype(BF16)
    n_chunks = FFN_DIM // FFN_CHUNK
    for c in range(n_chunks):
        lo = c * FFN_CHUNK
        gate = jnp.dot(xn, win_ref[:, lo:lo + FFN_CHUNK], preferred_element_type=F32)
        up = jnp.dot(xn, win_ref[:, FFN_DIM + lo:FFN_DIM + lo + FFN_CHUNK], preferred_element_type=F32)
        h = (_silu(gate) * up).astype(BF16)
        part = jnp.dot(h, wout_ref[lo:lo + FFN_CHUNK, :], preferred_element_type=F32)
        if c == 0:
            acc_ref[...] = part
        else:
            acc_ref[...] += part
    o_ref[...] = x + 0.5 * _rms(acc_ref[...], gpost_ref[...])


def _ffn_kernel(x_ref, gpre_ref, gpost_ref, win_ref, wout_ref, o_ref, acc_ref):
    _ffn_body(x_ref[...], gpre_ref, gpost_ref, win_ref, wout_ref, o_ref, acc_ref)


def _mix_ffn_kernel(x_ref, ma_ref, mb_ref, mc_ref, md_ref, wo_ref, gmix_ref, gpre_ref, gpost_ref,
                    win_ref, wout_ref, o_ref, acc_ref):
    y = None
    for i, m_ref in enumerate((ma_ref, mb_ref, mc_ref, md_ref)):
        part = jnp.dot(m_ref[...].astype(BF16), wo_ref[i * GROUP_W:(i + 1) * GROUP_W, :],
                       preferred_element_type=F32)
        y = part if y is None else y + part
    x = x_ref[...] + _rms(y, gmix_ref[...])
    _ffn_body(x, gpre_ref, gpost_ref, win_ref, wout_ref, o_ref, acc_ref)


def _ffn(x, g_pre, g_post, w_in, w_out, mix=None, w_o=None, g_mix=None):
    t = x.shape[0]
    tm = min(TOKEN_TILE, t)
    assert t % tm == 0
    tok = pl.BlockSpec((tm, D_MODEL), lambda i: (i, 0))
    vec = _resident((1, D_MODEL))
    w_specs = [_resident((D_MODEL, 2 * FFN_DIM)), _resident((FFN_DIM, D_MODEL))]
    if mix is None:
        kern, args = _ffn_kernel, (x, g_pre, g_post, w_in, w_out)
        in_specs = [tok, vec, vec] + w_specs
    else:
        kern, args = _mix_ffn_kernel, (x, *mix, w_o, g_mix, g_pre, g_post, w_in, w_out)
        part = pl.BlockSpec((tm, GROUP_W), lambda i: (i, 0))
        in_specs = [tok, part, part, part, part, _resident((D_MODEL, D_MODEL)), vec, vec, vec] + w_specs
    return pl.pallas_call(
        kern,
        grid=(t // tm,),
        in_specs=in_specs,
        out_specs=tok,
        out_shape=jax.ShapeDtypeStruct((t, D_MODEL), F32),
        scratch_shapes=[pltpu.VMEM((tm, D_MODEL), F32)],
        compiler_params=_cparams("arbitrary"),
        name="mix_ffn" if mix is not None else "ffn",
    )(*args)


def _proj_kernel(x_ref, g_ref, w_ref, o_ref):
    xn = _rms(x_ref[...], g_ref[...]).astype(BF16)
    o_ref[...] = jnp.dot(xn, w_ref[...], preferred_element_type=F32)


def _proj(x, g, w):
    t = x.shape[0]
    tm = min(TOKEN_TILE, t)
    assert t % tm == 0
    return pl.pallas_call(
        _proj_kernel,
        grid=(t // tm,),
        in_specs=[pl.BlockSpec((tm, D_MODEL), lambda i: (i, 0)),
                  _resident((1, D_MODEL)), _resident((D_MODEL, PROJ_COLS))],
        out_specs=pl.BlockSpec((tm, PROJ_COLS), lambda i: (i, 0)),
        out_shape=jax.ShapeDtypeStruct((t, PROJ_COLS), F32),
        compiler_params=_cparams("arbitrary"),
        name="in_proj",
    )(x, g, w)


def _seq_spec(rows, name, n_steps):
    lo, hi = _DST_COLS[name]
    return pl.BlockSpec((rows, hi - lo), lambda b, n, c=_col_block(name): (b * n_steps + n, c))


def _small_spec(rows, n_steps):
    return pl.BlockSpec((rows, V7X_LANES), lambda b, n: (b * n_steps + n, SMALL_BLOCK))


def _out_spec(rows, width, n_steps):
    return pl.BlockSpec((rows, width), lambda b, n: (b * n_steps + n, 0))


def _causal_conv(x, halo_ref, conv0_ref, w_ref, rows, seq_rows):
    first = V7X_SUBLANES - (CONV_K - 1)

    def taps(i):
        y = w_ref[0:1, :] * halo_ref[i, first:first + seq_rows, :]
        for j in range(1, CONV_K):
            y = y + w_ref[j:j + 1, :] * halo_ref[i, first + j:first + j + seq_rows, :]
        return y

    if seq_rows == rows:
        @pl.when(pl.program_id(1) == 0)
        def _():
            halo_ref[0, first:V7X_SUBLANES, :] = conv0_ref[0]

        halo_ref[0, V7X_SUBLANES:V7X_SUBLANES + rows, :] = x
        y = taps(0)
        halo_ref[0, 0:V7X_SUBLANES, :] = x[rows - V7X_SUBLANES:rows, :]
        return y
    parts = []
    for i in range(rows // seq_rows):
        halo_ref[i, first:V7X_SUBLANES, :] = conv0_ref[i]
        halo_ref[i, V7X_SUBLANES:V7X_SUBLANES + seq_rows, :] = x[i * seq_rows:(i + 1) * seq_rows, :]
        parts.append(taps(i))
    return jnp.concatenate(parts, axis=0)


def _level_masks(rows):
    i = np.arange(rows)[:, None]
    j = np.arange(rows)[None, :]
    out = []
    s = 1
    while s < rows:
        out.append(((i // (2 * s) == j // (2 * s)) & (i % (2 * s) >= s) & (j % (2 * s) < s)).astype(np.float32))
        s *= 2
    return np.stack(out)


def _unit_lower_inverse(a_strict, lv_ref, eye):
    t = eye - a_strict * lv_ref[0]
    for lv in range(1, lv_ref.shape[0]):
        a_s = a_strict * lv_ref[lv]
        t = t - _mm(_mm(t, a_s), t)
    return t


def _gdn_kernel(qkv_ref, z_ref, sm_ref, conv0_ref, s0_ref, cw_ref, lane_ref, g_ref, lv_ref,
                o_ref, sfin_ref, halo_ref, s_ref, *, rows, chunk, n_valid, chained):
    n_chunks = rows // chunk
    shift = chunk.bit_length() - 1
    if chained:
        @pl.when(pl.program_id(1) == 0)
        def _():
            s_ref[...] = s0_ref[0]

    act = _silu(_causal_conv(qkv_ref[...], halo_ref, conv0_ref, cw_ref, rows, rows if chained else chunk))
    sm = sm_ref[...]
    row = _iota2(rows, rows, 0)
    col = _iota2(rows, rows, 1)
    same_chunk = lax.shift_right_logical(row, shift) == lax.shift_right_logical(col, shift)
    tril_blocks = jnp.where(same_chunk, (col <= row).astype(F32), 0.0).astype(BF16)
    valid = (_iota2(rows, V7X_LANES, 0) & (chunk - 1)) < n_valid
    log_alpha = jnp.where(valid, -jnp.exp(lane_ref[0:1, :]) * _softplus(sm + lane_ref[1:2, :]), 0.0)
    beta = jnp.where(valid, jax.nn.sigmoid(sm), 0.0)
    gcum = _mm_exact_lhs(tril_blocks, log_alpha)
    gcum_t = gcum.T
    rc = _iota2(chunk, chunk, 0)
    cc = _iota2(chunk, chunk, 1)
    tril = cc <= rc
    strict = cc < rc
    eye = (cc == rc).astype(F32)
    z = z_ref[...]
    chunks = range(n_chunks)
    rsl = [slice(c * chunk, (c + 1) * chunk) for c in chunks]
    outs = {}
    finals = [[None] * GDN_HEADS for _ in range(1 if chained else n_chunks)]
    for first in range(0, GDN_HEADS, GDN_HEAD_GROUP):
        heads = range(first, first + GDN_HEAD_GROUP)
        chains = [(h, c) for h in heads for c in chunks]
        q, k, kb, rhs, qd, gcol = {}, {}, {}, {}, {}, {}
        for h in heads:
            lane = LANE_A + h
            qh = act[:, h * GDN_DK:(h + 1) * GDN_DK]
            kh = act[:, GROUP_W + h * GDN_DK:GROUP_W + (h + 1) * GDN_DK]
            vh = act[:, 2 * GROUP_W + h * GDN_DV:2 * GROUP_W + (h + 1) * GDN_DV]
            qh = qh * lax.rsqrt(jnp.sum(qh * qh, axis=-1, keepdims=True) + EPS) * GDN_DK ** -0.5
            kh = kh * lax.rsqrt(jnp.sum(kh * kh, axis=-1, keepdims=True) + EPS)
            g_h = gcum[:, lane:lane + 1]
            b_h = beta[:, LANE_B + h:LANE_B + h + 1]
            e_h = jnp.exp(g_h)
            kb_h = kh * b_h
            rhs_h = jnp.concatenate([kb_h * e_h, vh * b_h], axis=1)
            qd_h = qh * e_h
            for c in chunks:
                q[h, c], k[h, c], kb[h, c] = qh[rsl[c]], kh[rsl[c]], kb_h[rsl[c]]
                rhs[h, c], qd[h, c], gcol[h, c] = rhs_h[rsl[c]], qd_h[rsl[c]], g_h[rsl[c]]
        decay = {(h, c): jnp.exp(jnp.where(tril, gcol[h, c] - gcum_t[LANE_A + h:LANE_A + h + 1, rsl[c]], NEG_BIG))
                 for h, c in chains}
        kk = {i: _mm(kb[i], k[i], _NT) for i in chains}
        qk = {i: _mm(q[i], k[i], _NT) * decay[i] for i in chains}
        a_strict = {i: jnp.where(strict, kk[i] * decay[i], 0.0) for i in chains}
        t_inv = {i: eye - a_strict[i] * lv_ref[0] for i in chains}
        for lv in range(1, lv_ref.shape[0]):
            half = {i: _mm(t_inv[i], a_strict[i] * lv_ref[lv]) for i in chains}
            t_inv = {i: t_inv[i] - _mm(half[i], t_inv[i]) for i in chains}
        sol = {i: _mm(t_inv[i], rhs[i]) for i in chains}
        glast = {i: gcol[i][chunk - 1:chunk] for i in chains}
        k_dec = {i: k[i] * jnp.exp(glast[i] - gcol[i]) for i in chains}
        state = {h: s_ref[h] for h in heads} if chained else None
        for c in chunks:
            cur = state if chained else {h: s0_ref[c, h] for h in heads}
            w_s = {h: _mm(sol[h, c][:, :GDN_DK], cur[h]) for h in heads}
            o_s = {h: _mm(qd[h, c], cur[h]) for h in heads}
            u_new = {h: sol[h, c][:, GDN_DK:] - w_s[h] for h in heads}
            for h in heads:
                outs[h, c] = o_s[h] + _mm(qk[h, c], u_new[h])
            nxt = {h: cur[h] * jnp.exp(glast[h, c]) + _mm(k_dec[h, c], u_new[h], _TN) for h in heads}
            if chained:
                state = nxt
            else:
                for h in heads:
                    finals[c][h] = nxt[h]
        if chained:
            for h in heads:
                finals[0][h] = state[h]
    head_outs = []
    for h in range(GDN_HEADS):
        o = outs[h, 0] if n_chunks == 1 else jnp.concatenate([outs[h, c] for c in chunks], axis=0)
        head_outs.append(_rms(o, g_ref[...]) * _silu(z[:, h * GDN_DV:(h + 1) * GDN_DV]))
    o_ref[...] = jnp.concatenate(head_outs, axis=1)
    new_states = jnp.stack([jnp.stack(per_seq) for per_seq in finals])
    sfin_ref[...] = new_states
    if chained:
        s_ref[...] = new_states[0]


def _gdn(p, conv0, s0, conv_w, a_log, dt_bias, norm_g, n_seq, seq_len, rows, chunk, n_valid):
    chained = seq_len > chunk
    seqs_per_step = 1 if chained else rows // chunk
    n_steps = seq_len // rows if chained else 1
    n_blocks = n_seq // seqs_per_step
    assert n_blocks * seqs_per_step == n_seq and chunk & (chunk - 1) == 0
    lane = jnp.zeros((2, V7X_LANES), F32)
    lane = lane.at[0, LANE_A:LANE_A + GDN_HEADS].set(a_log).at[1, LANE_A:LANE_A + GDN_HEADS].set(dt_bias)
    levels = jnp.asarray(_level_masks(chunk))
    halo_rows = V7X_SUBLANES + (rows if chained else chunk)
    return pl.pallas_call(
        functools.partial(_gdn_kernel, rows=rows, chunk=chunk, n_valid=n_valid, chained=chained),
        grid=(n_blocks, n_steps),
        in_specs=[_seq_spec(rows, 'qkv', n_steps), _seq_spec(rows, 'z_a', n_steps), _small_spec(rows, n_steps),
                  pl.BlockSpec((seqs_per_step, CONV_K - 1, 3 * GROUP_W), lambda b, n: (b, 0, 0)),
                  pl.BlockSpec((seqs_per_step, GDN_HEADS, GDN_DK, GDN_DV), lambda b, n: (b, 0, 0, 0)),
                  _resident((CONV_K, 3 * GROUP_W)), _resident((2, V7X_LANES)), _resident((1, GDN_DV)),
                  _resident(levels.shape)],
        out_specs=[_out_spec(rows, GROUP_W, n_steps),
                   pl.BlockSpec((seqs_per_step, GDN_HEADS, GDN_DK, GDN_DV), lambda b, n: (b, 0, 0, 0))],
        out_shape=[jax.ShapeDtypeStruct((n_seq * seq_len, GROUP_W), F32),
                   jax.ShapeDtypeStruct((n_seq, GDN_HEADS, GDN_DK, GDN_DV), F32)],
        scratch_shapes=[pltpu.VMEM((seqs_per_step, halo_rows, 3 * GROUP_W), F32),
                        pltpu.VMEM((GDN_HEADS, GDN_DK, GDN_DV), F32)],
        compiler_params=_cparams("arbitrary", "arbitrary"),
        name="gdn",
    )(p, p, p, conv0, s0, conv_w, lane, norm_g.reshape(1, GDN_DV), levels)


def _gmlp_kernel(u_ref, v_ref, lng_ref, lnb_ref, ws_ref, bias_ref, o_ref, vb_ref, *, rows, chunk):
    u = jax.nn.gelu(u_ref[...], approximate=True)
    v = jax.nn.gelu(v_ref[...], approximate=True)
    mu = jnp.mean(v, axis=-1, keepdims=True)
    var = jnp.mean(jnp.square(v - mu), axis=-1, keepdims=True)
    v = (v - mu) * lax.rsqrt(var + EPS) * lng_ref[...] + lnb_ref[...]
    vb_ref[...] = v
    tril = _iota2(chunk, chunk, 1) <= _iota2(chunk, chunk, 0)
    ws = [jnp.where(tril, ws_ref[g], 0.0).astype(BF16) for g in range(MLP_GROUPS)]
    for c in range(rows // chunk):
        rs = slice(c * chunk, (c + 1) * chunk)
        for g in range(MLP_GROUPS):
            ls = slice(g * MLP_GW, (g + 1) * MLP_GW)
            mixed = _mm(ws[g], v[rs, ls]) + bias_ref[:, ls]
            o_ref[rs, ls] = u[rs, ls] * mixed


def _gmlp(p, ln_g, ln_b, ws, bs, n_seq, seq_len, rows, chunk):
    n_steps = seq_len // rows
    bias = jnp.repeat(bs[:, :chunk].T, MLP_GW, axis=1)
    return pl.pallas_call(
        functools.partial(_gmlp_kernel, rows=rows, chunk=chunk),
        grid=(n_seq, n_steps),
        in_specs=[_seq_spec(rows, 'u', n_steps), _seq_spec(rows, 'v', n_steps),
                  _resident((1, GROUP_W)), _resident((1, GROUP_W)),
                  _resident((MLP_GROUPS, chunk, chunk)), _resident((chunk, GROUP_W))],
        out_specs=[_out_spec(rows, GROUP_W, n_steps), _out_spec(rows, GROUP_W, n_steps)],
        out_shape=[jax.ShapeDtypeStruct((n_seq * seq_len, GROUP_W), F32)] * 2,
        compiler_params=_cparams("arbitrary", "arbitrary"),
        name="gmlp",
    )(p, p, ln_g.reshape(1, GROUP_W), ln_b.reshape(1, GROUP_W), ws[:, :chunk, :chunk], bias)


def _ssd_kernel(xbc_ref, z_ref, sm_ref, conv0_ref, h0_ref, cw_ref, cb_ref, lane_ref, g_ref,
                o_ref, hfin_ref, halo_ref, h_ref, *, rows, n_valid):
    @pl.when(pl.program_id(1) == 0)
    def _():
        h_ref[...] = h0_ref[0]

    act = _silu(_causal_conv(xbc_ref[...], halo_ref, conv0_ref, cw_ref, rows, rows) + cb_ref[...])
    sm = sm_ref[...]
    tril = _iota2(rows, rows, 1) <= _iota2(rows, rows, 0)
    valid = _iota2(rows, V7X_LANES, 0) < n_valid
    dt = jnp.where(valid, _softplus(sm + lane_ref[1:2, :]), 0.0)
    acum = _mm_exact_lhs(tril.astype(BF16), dt * -jnp.exp(lane_ref[0:1, :]))
    acum_t = acum.T
    heads = range(SSM_HEADS)
    group_of = [h // (SSM_HEADS // SSM_GROUPS) for h in heads]
    b_g = [act[:, GROUP_W + g * SSM_STATE:GROUP_W + (g + 1) * SSM_STATE] for g in range(SSM_GROUPS)]
    c_g = [act[:, GROUP_W + (SSM_GROUPS + g) * SSM_STATE:GROUP_W + (SSM_GROUPS + g + 1) * SSM_STATE]
           for g in range(SSM_GROUPS)]
    cb = [_mm(c_g[g], b_g[g], _NT) for g in range(SSM_GROUPS)]
    acol = [acum[:, LANE_DT + h:LANE_DT + h + 1] for h in heads]
    alast = [acum[rows - 1:rows, LANE_DT + h:LANE_DT + h + 1] for h in heads]
    decay = [jnp.exp(jnp.where(tril, acol[h] - acum_t[LANE_DT + h:LANE_DT + h + 1, :], NEG_BIG)) for h in heads]
    x = [act[:, h * SSM_HEAD_DIM:(h + 1) * SSM_HEAD_DIM] for h in heads]
    xdt = [x[h] * dt[:, LANE_DT + h:LANE_DT + h + 1] for h in heads]
    state = [h_ref[h] for h in heads]
    y_intra = [_mm(cb[group_of[h]] * decay[h], xdt[h]) for h in heads]
    y_inter = [_mm(c_g[group_of[h]] * jnp.exp(acol[h]), state[h], _NT) for h in heads]
    h_new = [state[h] * jnp.exp(alast[h]) + _mm(xdt[h], b_g[group_of[h]] * jnp.exp(alast[h] - acol[h]), _TN)
             for h in heads]
    y = jnp.concatenate([y_intra[h] + y_inter[h] + lane_ref[2:3, LANE_DT + h:LANE_DT + h + 1] * x[h]
                         for h in heads], axis=1)
    new_states = jnp.stack(h_new)
    h_ref[...] = new_states
    hfin_ref[0] = new_states
    o_ref[...] = _rms(y * _silu(z_ref[...]), g_ref[...])


def _ssd(p, conv0, h0, conv_w, conv_b, a_log, dt_bias, d_skip, norm_g, n_seq, seq_len, rows, n_valid):
    n_steps = seq_len // rows
    lane = jnp.zeros((3, V7X_LANES), F32)
    lane = (lane.at[0, LANE_DT:LANE_DT + SSM_HEADS].set(a_log)
            .at[1, LANE_DT:LANE_DT + SSM_HEADS].set(dt_bias)
            .at[2, LANE_DT:LANE_DT + SSM_HEADS].set(d_skip))
    return pl.pallas_call(
        functools.partial(_ssd_kernel, rows=rows, n_valid=n_valid),
        grid=(n_seq, n_steps),
        in_specs=[_seq_spec(rows, 'xbc', n_steps), _seq_spec(rows, 'z_c', n_steps), _small_spec(rows, n_steps),
                  pl.BlockSpec((1, CONV_K - 1, SSM_CONV_W), lambda b, n: (b, 0, 0)),
                  pl.BlockSpec((1, SSM_HEADS, SSM_HEAD_DIM, SSM_STATE), lambda b, n: (b, 0, 0, 0)),
                  _resident((CONV_K, SSM_CONV_W)), _resident((1, SSM_CONV_W)), _resident((3, V7X_LANES)),
                  _resident((1, GROUP_W))],
        out_specs=[_out_spec(rows, GROUP_W, n_steps),
                   pl.BlockSpec((1, SSM_HEADS, SSM_HEAD_DIM, SSM_STATE), lambda b, n: (b, 0, 0, 0))],
        out_shape=[jax.ShapeDtypeStruct((n_seq * seq_len, GROUP_W), F32),
                   jax.ShapeDtypeStruct((n_seq, SSM_HEADS, SSM_HEAD_DIM, SSM_STATE), F32)],
        scratch_shapes=[pltpu.VMEM((1, V7X_SUBLANES + rows, SSM_CONV_W), F32),
                        pltpu.VMEM((SSM_HEADS, SSM_HEAD_DIM, SSM_STATE), F32)],
        compiler_params=_cparams("arbitrary", "arbitrary"),
        name="ssd",
    )(p, p, p, conv0, h0, conv_w, conv_b.reshape(1, SSM_CONV_W), lane, norm_g.reshape(1, GROUP_W))


def _rope_lanes(x, cos, sin_signed):
    lane = _iota2(x.shape[0], V7X_LANES, 1)
    half = MLA_ROPE // 2
    swapped = jnp.where(lane % MLA_ROPE < half,
                        pltpu.roll(x, V7X_LANES - half, axis=1), pltpu.roll(x, half, axis=1))
    return x * cos + swapped * sin_signed


def _rope_tables(pos):
    half = MLA_ROPE // 2
    inv_freq = ROPE_THETA ** (-jnp.arange(half, dtype=F32) / half)
    ang = pos.astype(F32)[:, None] * inv_freq[None, :]
    cos, sin = jnp.cos(ang), jnp.sin(ang)
    reps = V7X_LANES // MLA_ROPE
    return jnp.tile(jnp.concatenate([cos, cos], axis=1), (1, reps)), jnp.tile(jnp.concatenate([-sin, sin], axis=1), (1, reps))


def _mla_project(cq_ref, ckv_ref, sm_ref, cos_ref, sin_ref, gq_ref, gkv_ref, wn_ref, wpe_ref, wuk_ref):
    c_q = _rms(cq_ref[...], gq_ref[...]).astype(BF16)
    c_kv = _rms(ckv_ref[...], gkv_ref[...])
    cos, sin = cos_ref[...], sin_ref[...]
    lane = _iota2(c_kv.shape[0], V7X_LANES, 1)
    k_pe = _rope_lanes(jnp.where(lane < MLA_ROPE, sm_ref[...], 0.0), cos, sin)
    q_nope = jnp.dot(c_q, wn_ref[...], preferred_element_type=F32)
    q_heads = []
    for h in range(MLA_HEADS):
        q_lat = _mm(q_nope[:, h * MLA_NOPE:(h + 1) * MLA_NOPE], wuk_ref[h], _NT)
        q_pe = _rope_lanes(jnp.dot(c_q, wpe_ref[h], preferred_element_type=F32), cos, sin)
        q_heads.append((jnp.concatenate([q_lat, q_pe], axis=1) * MLA_SCALE).astype(BF16))
    return c_kv, k_pe, q_heads


def _softmax_step(q, keys, m, l, acc, mask=None):
    s = lax.dot_general(q, keys, _NT, preferred_element_type=F32)
    if mask is not None:
        s = jnp.where(mask, s, NEG_BIG)
    m_new = jnp.maximum(m, jnp.max(s, axis=-1, keepdims=True))
    p = jnp.exp(s - m_new)
    alpha = jnp.exp(m - m_new)
    l = alpha * l + jnp.sum(p, axis=-1, keepdims=True)
    acc = alpha * acc + jnp.dot(p.astype(BF16), keys[:, :MLA_KV_RANK], preferred_element_type=F32)
    return m_new, l, acc


def _mla_prompt_kernel(cq_ref, ckv_ref, sm_ref, cos_ref, sin_ref, gq_ref, gkv_ref, wn_ref, wpe_ref, wuk_ref,
                       wuv_ref, o_ref, ckv_out_ref, kpe_out_ref, keys_ref, vt_ref, q_ref, m_ref, acc_ref,
                       *, rows, kv_block):
    n = pl.program_id(1)
    qrows = MLA_HEADS * rows
    c_kv, k_pe, q_heads = _mla_project(cq_ref, ckv_ref, sm_ref, cos_ref, sin_ref, gq_ref, gkv_ref,
                                       wn_ref, wpe_ref, wuk_ref)
    ckv_out_ref[0] = c_kv
    kpe_out_ref[0] = k_pe[:, :MLA_ROPE]
    base = pl.multiple_of(n * rows, rows)
    keys_ref[pl.ds(base, rows), :] = jnp.concatenate([c_kv, k_pe], axis=1).astype(BF16)
    vt_ref[0:MLA_KV_RANK, pl.ds(base, rows)] = c_kv.T.astype(BF16)
    vt_ref[MLA_KV_RANK:MLA_VT_ROWS, pl.ds(base, rows)] = jnp.ones((MLA_VT_ROWS - MLA_KV_RANK, rows), BF16)
    q_ref[...] = jnp.concatenate(q_heads, axis=0)
    m_ref[...] = jnp.full((1, qrows), NEG_BIG, F32)
    acc_ref[...] = jnp.zeros((MLA_VT_ROWS, qrows), F32)

    def attend(off, mask):
        s = lax.dot_general(keys_ref[pl.ds(off, kv_block), :], q_ref[...], _NT, preferred_element_type=F32)
        if mask is not None:
            s = jnp.where(mask, s, NEG_BIG)
        m_old = m_ref[...]
        m_new = jnp.maximum(m_old, jnp.max(s, axis=0, keepdims=True))
        p = jnp.exp((s - m_new).astype(BF16))
        alpha = jnp.exp(m_old - m_new)
        acc_ref[...] = alpha * acc_ref[...] + jnp.dot(vt_ref[:, pl.ds(off, kv_block)], p,
                                                      preferred_element_type=F32)
        m_ref[...] = m_new

    def body(j, carry):
        attend(pl.multiple_of(j * kv_block, kv_block), None)
        return carry

    blocks_per_step = rows // kv_block
    lax.fori_loop(0, n * blocks_per_step, body, 0)
    q_pos = _iota2(kv_block, qrows, 1) & (rows - 1)
    for d in range(blocks_per_step):
        attend(base + d * kv_block, d * kv_block + _iota2(kv_block, qrows, 0) <= q_pos)
    o_lat_t = acc_ref[0:MLA_KV_RANK, :] / acc_ref[MLA_KV_RANK:MLA_KV_RANK + 1, :]
    outs = [_mm(o_lat_t[:, h * rows:(h + 1) * rows], wuv_ref[h], _TN) for h in range(MLA_HEADS)]
    o_ref[...] = jnp.concatenate(outs, axis=1)


def _mla_weights(w_uq, w_uk, w_uv):
    w4 = w_uq.reshape(MLA_Q_RANK, MLA_HEADS, MLA_NOPE + MLA_ROPE)
    w_nope = w4[:, :, :MLA_NOPE].reshape(MLA_Q_RANK, MLA_HEADS * MLA_NOPE).astype(BF16)
    w_pe = jnp.pad(w4[:, :, MLA_NOPE:].transpose(1, 0, 2),
                   ((0, 0), (0, 0), (0, V7X_LANES - MLA_ROPE))).astype(BF16)
    return w_nope, w_pe, w_uk.astype(BF16), w_uv.astype(BF16)


def _mla_prompt(p, q_norm_g, kv_norm_g, w_nope, w_pe, w_uk, w_uv, n_seq, seq_len, rows, kv_block):
    n_steps = seq_len // rows
    assert seq_len % rows == 0 and rows % kv_block == 0 and rows & (rows - 1) == 0
    cos, sin = _rope_tables(jnp.arange(seq_len))
    tab = pl.BlockSpec((rows, V7X_LANES), lambda b, n: (n, 0))
    return pl.pallas_call(
        functools.partial(_mla_prompt_kernel, rows=rows, kv_block=kv_block),
        grid=(n_seq, n_steps),
        in_specs=[_seq_spec(rows, 'c_q', n_steps), _seq_spec(rows, 'c_kv', n_steps), _small_spec(rows, n_steps),
                  tab, tab, _resident((1, MLA_Q_RANK)), _resident((1, MLA_KV_RANK)),
                  _resident(w_nope.shape), _resident(w_pe.shape), _resident(w_uk.shape), _resident(w_uv.shape)],
        out_specs=[_out_spec(rows, GROUP_W, n_steps),
                   pl.BlockSpec((1, rows, MLA_KV_RANK), lambda b, n: (b, n, 0)),
                   pl.BlockSpec((1, rows, MLA_ROPE), lambda b, n: (b, n, 0))],
        out_shape=[jax.ShapeDtypeStruct((n_seq * seq_len, GROUP_W), F32),
                   jax.ShapeDtypeStruct((n_seq, seq_len, MLA_KV_RANK), F32),
                   jax.ShapeDtypeStruct((n_seq, seq_len, MLA_ROPE), F32)],
        scratch_shapes=[pltpu.VMEM((seq_len, 2 * V7X_LANES), BF16),
                        pltpu.VMEM((MLA_VT_ROWS, seq_len), BF16),
                        pltpu.VMEM((MLA_HEADS * rows, 2 * V7X_LANES), BF16),
                        pltpu.VMEM((1, MLA_HEADS * rows), F32),
                        pltpu.VMEM((MLA_VT_ROWS, MLA_HEADS * rows), F32)],
        compiler_params=_cparams("arbitrary", "arbitrary"),
        name="mla_prompt",
    )(p, p, p, cos, sin, q_norm_g.reshape(1, MLA_Q_RANK), kv_norm_g.reshape(1, MLA_KV_RANK),
      w_nope, w_pe, w_uk, w_uv)


def _mla_decode_kernel(pt_ref, cq_ref, ckv_ref, sm_ref, cos_ref, sin_ref, gq_ref, gkv_ref, wn_ref, wpe_ref,
                       wuk_ref, wuv_ref, ckv_hbm, kpe_hbm, o_ref, ckv_out_ref, kpe_out_ref,
                       lat_buf, pe_buf, sems, s_ref, latb_ref, *, rows, layer, n_pages, pages_per_chunk):
    seq = pl.program_id(0)
    slot = lax.rem(seq, 2)
    qrows = MLA_HEADS * rows
    chunk = pages_per_chunk * PAGE_SIZE
    n_chunks = n_pages // pages_per_chunk

    def page_copies(src_seq, dst_slot, g):
        page = pt_ref[src_seq, g]
        return (pltpu.make_async_copy(ckv_hbm.at[layer, page], lat_buf.at[dst_slot, g], sems.at[0, dst_slot]),
                pltpu.make_async_copy(kpe_hbm.at[layer, page],
                                      pe_buf.at[dst_slot, :, pl.ds(pl.multiple_of(g * PAGE_SIZE, PAGE_SIZE), PAGE_SIZE)],
                                      sems.at[1, dst_slot]))

    def fetch(src_seq, dst_slot):
        def body(g, carry):
            for cp in page_copies(src_seq, dst_slot, g):
                cp.start()
            return carry
        lax.fori_loop(0, n_pages, body, 0, unroll=4)

    @pl.when(seq == 0)
    def _():
        fetch(0, 0)

    @pl.when(seq + 1 < pl.num_programs(0))
    def _():
        fetch(seq + 1, 1 - slot)

    c_kv, k_pe, q_heads = _mla_project(cq_ref, ckv_ref, sm_ref, cos_ref, sin_ref, gq_ref, gkv_ref,
                                       wn_ref, wpe_ref, wuk_ref)
    ckv_out_ref[0] = c_kv
    kpe_out_ref[0] = k_pe[:, :MLA_ROPE]
    knew = jnp.concatenate([c_kv, k_pe], axis=1).astype(BF16)
    q = jnp.concatenate(q_heads, axis=0)
    q_lat, q_pe = q[:, :MLA_KV_RANK], q[:, MLA_KV_RANK:MLA_KV_RANK + MLA_ROPE]
    causal = _iota2(qrows, rows, 1) <= (_iota2(qrows, rows, 0) & (rows - 1))
    s_new = jnp.where(causal, lax.dot_general(q, knew, _NT, preferred_element_type=F32), NEG_BIG)

    def wait_body(g, carry):
        for cp in page_copies(seq, slot, g):
            cp.wait()
        return carry
    lax.fori_loop(0, n_pages, wait_body, 0, unroll=4)

    def to_bf16(g, carry):
        latb_ref[pl.ds(pl.multiple_of(g * PAGE_SIZE, PAGE_SIZE), PAGE_SIZE), :] = lat_buf[slot, g].astype(BF16)
        return carry
    lax.fori_loop(0, n_pages, to_bf16, 0, unroll=8)

    m = jnp.max(s_new, axis=-1, keepdims=True)
    for c in range(n_chunks):
        s_lat = lax.dot_general(q_lat, latb_ref[c * chunk:(c + 1) * chunk, :], _NT, preferred_element_type=F32)
        s_pe = jnp.dot(q_pe, pe_buf[slot, :, c * chunk:(c + 1) * chunk].astype(BF16), preferred_element_type=F32)
        s = s_lat + s_pe
        s_ref[:, c * chunk:(c + 1) * chunk] = s
        m = jnp.maximum(m, jnp.max(s, axis=-1, keepdims=True))

    p_new = jnp.exp(s_new - m)
    l = jnp.sum(p_new, axis=-1, keepdims=True)
    acc = jnp.dot(p_new.astype(BF16), knew[:, :MLA_KV_RANK], preferred_element_type=F32)
    for c in range(n_chunks):
        pr = jnp.exp(s_ref[:, c * chunk:(c + 1) * chunk] - m)
        l = l + jnp.sum(pr, axis=-1, keepdims=True)
        acc = acc + jnp.dot(pr.astype(BF16), latb_ref[c * chunk:(c + 1) * chunk, :], preferred_element_type=F32)
    o_lat = acc / l
    outs = [_mm(o_lat[h * rows:(h + 1) * rows], wuv_ref[h]) for h in range(MLA_HEADS)]
    o_ref[...] = jnp.concatenate(outs, axis=1)


def _mla_decode(p, cache_ckv, cache_kpe, layer, page_table, q_norm_g, kv_norm_g, w_nope, w_pe, w_uk, w_uv,
                n_seq, rows, pages_per_chunk=16):
    n_pages = page_table.shape[1]
    pages_per_chunk = min(pages_per_chunk, n_pages)
    assert n_pages % pages_per_chunk == 0
    past_len = n_pages * PAGE_SIZE
    cos, sin = _rope_tables(past_len + jnp.arange(rows))

    def const(shape):
        return pl.BlockSpec(shape, lambda s, pt: (0,) * len(shape))

    def seq(name):
        lo, hi = _DST_COLS[name]
        return pl.BlockSpec((rows, hi - lo), lambda s, pt, c=_col_block(name): (s, c))

    in_specs = [seq('c_q'), seq('c_kv'), pl.BlockSpec((rows, V7X_LANES), lambda s, pt: (s, SMALL_BLOCK)),
                const((rows, V7X_LANES)), const((rows, V7X_LANES)),
                const((1, MLA_Q_RANK)), const((1, MLA_KV_RANK)),
                const(w_nope.shape), const(w_pe.shape), const(w_uk.shape), const(w_uv.shape),
                pl.BlockSpec(memory_space=pl.ANY), pl.BlockSpec(memory_space=pl.ANY)]
    qrows = MLA_HEADS * rows
    return pl.pallas_call(
        functools.partial(_mla_decode_kernel, rows=rows, layer=layer, n_pages=n_pages,
                          pages_per_chunk=pages_per_chunk),
        grid_spec=pltpu.PrefetchScalarGridSpec(
            num_scalar_prefetch=1,
            grid=(n_seq,),
            in_specs=in_specs,
            out_specs=[pl.BlockSpec((rows, GROUP_W), lambda s, pt: (s, 0)),
                       pl.BlockSpec((1, rows, MLA_KV_RANK), lambda s, pt: (s, 0, 0)),
                       pl.BlockSpec((1, rows, MLA_ROPE), lambda s, pt: (s, 0, 0))],
            scratch_shapes=[pltpu.VMEM((2, n_pages, PAGE_SIZE, MLA_KV_RANK), F32),
                            pltpu.VMEM((2, MLA_ROPE, past_len), F32),
                            pltpu.SemaphoreType.DMA((2, 2)),
                            pltpu.VMEM((qrows, past_len), F32),
                            pltpu.VMEM((past_len, MLA_KV_RANK), BF16)]),
        out_shape=[jax.ShapeDtypeStruct((n_seq * rows, GROUP_W), F32),
                   jax.ShapeDtypeStruct((n_seq, rows, MLA_KV_RANK), F32),
                   jax.ShapeDtypeStruct((n_seq, rows, MLA_ROPE), F32)],
        compiler_params=_cparams("arbitrary"),
        name="mla_decode",
    )(page_table, p, p, p, cos, sin, q_norm_g.reshape(1, MLA_Q_RANK), kv_norm_g.reshape(1, MLA_KV_RANK),
      w_nope, w_pe, w_uk, w_uv, cache_ckv, cache_kpe)


def _last_rows(p, n_seq, seq_len, n_valid, name):
    lo, hi = _DST_COLS[name]
    return p.reshape(n_seq, seq_len, PROJ_COLS)[:, n_valid - (CONV_K - 1):n_valid, lo:hi]


def _trunk_layer(x, w, layer, n_seq, seq_len, rows, n_valid, gdn_conv, gdn_s, ssm_conv, ssm_h, paged):
    g = w['norm_g']
    assert n_valid == seq_len or seq_len == rows
    step_valid = min(n_valid, rows)
    x = _ffn(x, g[0:1], g[1:2], w['ffn_w_in'][0], w['ffn_w_out'][0])
    p = _proj(x, g[2:3], w['w_in'])
    if seq_len == rows:
        gdn_shape = (rows * SAMPLE_SEQS_PER_STEP, rows, n_valid)
        mlp_shape = (1, n_seq * seq_len, min(SEQ_TILE, n_seq * seq_len), rows)
    else:
        gdn_shape = (rows, GDN_CHUNK, GDN_CHUNK)
        mlp_shape = (n_seq, seq_len, rows, MLP_CHUNK)
    out_a, gdn_s_new = _gdn(p, gdn_conv, gdn_s, w['gdn_conv_w'], w['gdn_a_log'], w['gdn_dt_bias'],
                            w['gdn_norm_g'], n_seq, seq_len, *gdn_shape)
    out_b, v_b = _gmlp(p, w['mlp_ln_g'], w['mlp_ln_b'], w['mlp_ws'], w['mlp_bs'], *mlp_shape)
    out_c, ssm_h_new = _ssd(p, ssm_conv, ssm_h, w['ssm_conv_w'], w['ssm_conv_b'], w['ssm_a_log'],
                            w['ssm_dt_bias'], w['ssm_d'], w['ssm_norm_g'], n_seq, seq_len, rows, step_valid)
    mla_w = (w['mla_q_norm_g'], w['mla_kv_norm_g'], w['mla_w_nope'], w['mla_w_pe'], w['mla_w_uk'], w['mla_w_uv'])
    if paged is None:
        out_d, c_kv, k_pe = _mla_prompt(p, *mla_w, n_seq, seq_len, MLA_Q_TILE, MLA_KV_BLOCK)
    else:
        cache_ckv, cache_kpe, page_table = paged
        out_d, c_kv, k_pe = _mla_decode(p, cache_ckv, cache_kpe, layer, page_table, *mla_w, n_seq, rows)
    x = _ffn(x, g[4:5], g[5:6], w['ffn_w_in'][1], w['ffn_w_out'][1],
             mix=(out_a, out_b, out_c, out_d), w_o=w['w_out'], g_mix=g[3:4])
    gdn_conv_new = _last_rows(p, n_seq, seq_len, n_valid, 'qkv')
    ssm_conv_new = _last_rows(p, n_seq, seq_len, n_valid, 'xbc')
    v_b = v_b.reshape(n_seq, seq_len, GROUP_W)[:, :n_valid]
    return x, (c_kv[:, :n_valid], k_pe[:, :n_valid], gdn_s_new, gdn_conv_new, ssm_h_new, ssm_conv_new, v_b)


def kernel(x_prompt, x_sample, cache_ckv, cache_kpe, page_table, state_gdn_s, state_gdn_conv,
           state_ssm_h, state_ssm_conv, norm_g, ffn_w_in, ffn_w_out, w_in, w_out,
           gdn_conv_w, gdn_a_log, gdn_dt_bias, gdn_norm_g, mlp_ln_g, mlp_ln_b, mlp_ws, mlp_bs,
           ssm_conv_w, ssm_conv_b, ssm_a_log, ssm_dt_bias, ssm_d, ssm_norm_g,
           mla_q_norm_g, mla_w_uq, mla_kv_norm_g, mla_w_uk, mla_w_uv):
    bp, lp, _ = x_prompt.shape
    bs, ls, _ = x_sample.shape
    assert lp % SEQ_TILE == 0 and CONV_K - 1 <= ls <= SAMPLE_ROWS
    w_in_p = jnp.pad(w_in[:, :, _PERM], ((0, 0), (0, 0), (0, PROJ_COLS - PROJ_USED)))
    cache_kpe_t = jnp.swapaxes(cache_kpe, 2, 3)
    weights = dict(norm_g=norm_g, ffn_w_in=ffn_w_in.astype(BF16), ffn_w_out=ffn_w_out.astype(BF16),
                   w_in=w_in_p.astype(BF16), w_out=w_out.astype(BF16),
                   gdn_conv_w=gdn_conv_w, gdn_a_log=gdn_a_log, gdn_dt_bias=gdn_dt_bias, gdn_norm_g=gdn_norm_g,
                   mlp_ln_g=mlp_ln_g, mlp_ln_b=mlp_ln_b, mlp_ws=mlp_ws, mlp_bs=mlp_bs,
                   ssm_conv_w=ssm_conv_w, ssm_conv_b=ssm_conv_b, ssm_a_log=ssm_a_log,
                   ssm_dt_bias=ssm_dt_bias, ssm_d=ssm_d, ssm_norm_g=ssm_norm_g,
                   mla_q_norm_g=mla_q_norm_g, mla_kv_norm_g=mla_kv_norm_g)
    xp = x_prompt.reshape(bp * lp, D_MODEL)
    xs = jnp.pad(x_sample, ((0, 0), (0, SAMPLE_ROWS - ls), (0, 0))).reshape(bs * SAMPLE_ROWS, D_MODEL)
    zeros_p = (jnp.zeros((bp, CONV_K - 1, 3 * GROUP_W), F32), jnp.zeros((bp, GDN_HEADS, GDN_DK, GDN_DV), F32),
               jnp.zeros((bp, CONV_K - 1, SSM_CONV_W), F32), jnp.zeros((bp, SSM_HEADS, SSM_HEAD_DIM, SSM_STATE), F32))
    st_p, st_s = [], []
    for l in range(DEPTH):
        wl = {name: arr[l] for name, arr in weights.items()}
        wl['mla_w_nope'], wl['mla_w_pe'], wl['mla_w_uk'], wl['mla_w_uv'] = _mla_weights(
            mla_w_uq[l], mla_w_uk[l], mla_w_uv[l])
        xp, st = _trunk_layer(xp, wl, l, bp, lp, SEQ_TILE, lp, *zeros_p, None)
        st_p.append(st)
        xs, st = _trunk_layer(xs, wl, l, bs, SAMPLE_ROWS, SAMPLE_ROWS, ls,
                              state_gdn_conv[l], state_gdn_s[l], state_ssm_conv[l], state_ssm_h[l],
                              (cache_ckv, cache_kpe_t, page_table))
        st_s.append(st)

    def stack(states, i):
        return jnp.stack([s[i] for s in states])

    y_prompt = xp.reshape(bp, lp, D_MODEL)
    y_sample = xs.reshape(bs, SAMPLE_ROWS, D_MODEL)[:, :ls]
    return (y_prompt, y_sample,
            stack(st_p, 0), stack(st_p, 1), stack(st_p, 2), stack(st_p, 3), stack(st_p, 4), stack(st_p, 5),
            stack(st_s, 0), stack(st_s, 1), stack(st_s, 2), stack(st_s, 3), stack(st_s, 4), stack(st_s, 5),
            stack(st_s, 6))
```

```python
import functools

import numpy as np
import jax
import jax.numpy as jnp
from jax import lax
from jax.experimental import pallas as pl
from jax.experimental.pallas import tpu as pltpu

F32 = jnp.float32
BF16 = jnp.bfloat16

D_MODEL = 1024
DEPTH = 2
PAGE_SIZE = 128
GROUP_W = 256
CONV_K = 4
FFN_DIM = 2816
EPS = 1e-6

GDN_HEADS = 4
GDN_DK = 64
GDN_DV = 64
MLP_CHUNK = 128
MLP_GROUPS = 4
MLP_GW = 64
SSM_HEADS = 4
SSM_HEAD_DIM = 64
SSM_GROUPS = 2
SSM_STATE = 128
SSM_CONV_W = 768
MLA_HEADS = 4
MLA_NOPE = 64
MLA_ROPE = 32
MLA_V_DIM = 64
MLA_Q_RANK = 256
MLA_KV_RANK = 128
MLA_SCALE = (MLA_NOPE + MLA_ROPE) ** -0.5
ROPE_THETA = 10000.0

V7X_LANES = 128
V7X_SUBLANES = 8
V7X_MXU_DIM = 256
V7X_VMEM_LIMIT_BYTES = 56 * 1024 * 1024

_SRC_COLS = dict(
    qkv=(0, 768), z_a=(768, 1024), a_a=(1024, 1028), b_a=(1028, 1032),
    u=(1032, 1288), v=(1288, 1544),
    z_c=(1544, 1800), xbc=(1800, 2568), dt=(2568, 2572),
    c_q=(2572, 2828), c_kv=(2828, 2956), k_pe=(2956, 2988))
_DST_ORDER = ('qkv', 'xbc', 'z_a', 'u', 'v', 'z_c', 'c_q', 'c_kv', 'k_pe', 'a_a', 'b_a', 'dt')
_DST_COLS = {}
_off = 0
for _name in _DST_ORDER:
    _w = _SRC_COLS[_name][1] - _SRC_COLS[_name][0]
    _DST_COLS[_name] = (_off, _off + _w)
    _off += _w
PROJ_USED = _off
PROJ_COLS = -(-PROJ_USED // V7X_LANES) * V7X_LANES
_PERM = np.concatenate([np.arange(*_SRC_COLS[n]) for n in _DST_ORDER])
SMALL_BLOCK = _DST_COLS['k_pe'][0] // V7X_LANES
LANE_A = _DST_COLS['a_a'][0] % V7X_LANES
LANE_B = _DST_COLS['b_a'][0] % V7X_LANES
LANE_DT = _DST_COLS['dt'][0] % V7X_LANES


def _col_block(name):
    lo, hi = _DST_COLS[name]
    assert lo % (hi - lo) == 0
    return lo // (hi - lo)


FFN_CHUNK = V7X_MXU_DIM
TOKEN_TILE = 512
SEQ_TILE = 256
MLA_Q_TILE = 512
MLA_KV_BLOCK = 512
SAMPLE_ROWS = V7X_SUBLANES
SAMPLE_SEQS_PER_STEP = 8
GDN_CHUNK = 64
GDN_HEAD_GROUP = 4
MLA_VT_ROWS = MLA_KV_RANK + 16
NEG_BIG = -1e30


def _cparams(*sem):
    return pltpu.CompilerParams(dimension_semantics=sem, vmem_limit_bytes=V7X_VMEM_LIMIT_BYTES)


def _rms(x, g):
    return x * lax.rsqrt(jnp.mean(x * x, axis=-1, keepdims=True) + EPS) * g


def _silu(x):
    return x * jax.nn.sigmoid(x)


def _softplus(x):
    return jnp.maximum(x, 0.0) + jnp.log(1.0 + jnp.exp(-jnp.abs(x)))


_NN = (((1,), (0,)), ((), ()))
_NT = (((1,), (1,)), ((), ()))
_TN = (((0,), (0,)), ((), ()))


def _mm(a, b, dims=_NN):
    return lax.dot_general(a.astype(BF16), b.astype(BF16), dims, preferred_element_type=F32)


def _split3(x):
    x0 = x.astype(BF16)
    r1 = x - x0.astype(F32)
    x1 = r1.astype(BF16)
    x2 = (r1 - x1.astype(F32)).astype(BF16)
    return x0, x1, x2


def _mm_exact_lhs(a_bf16, b):
    b0, b1, b2 = _split3(b)
    dot = lambda y: lax.dot_general(a_bf16, y, _NN, preferred_element_type=F32)
    return dot(b0) + (dot(b1) + dot(b2))


def _iota2(n, m, axis):
    return lax.broadcasted_iota(jnp.int32, (n, m), axis)


def _resident(shape):
    return pl.BlockSpec(shape, lambda *_: (0,) * len(shape), pipeline_mode=pl.Buffered(1))


def _ffn_body(x, gpre_ref, gpost_ref, win_ref, wout_ref, o_ref, acc_ref):
    xn = _rms(x, gpre_ref[...]).astype(BF16)
    n_chunks = FFN_DIM // FFN_CHUNK
    for c in range(n_chunks):
        lo = c * FFN_CHUNK
        gate = jnp.dot(xn, win_ref[:, lo:lo + FFN_CHUNK], preferred_element_type=F32)
        up = jnp.dot(xn, win_ref[:, FFN_DIM + lo:FFN_DIM + lo + FFN_CHUNK], preferred_element_type=F32)
        h = (_silu(gate) * up).astype(BF16)
        part = jnp.dot(h, wout_ref[lo:lo + FFN_CHUNK, :], preferred_element_type=F32)
        if c == 0:
            acc_ref[...] = part
        else:
            acc_ref[...] += part
    o_ref[...] = x + 0.5 * _rms(acc_ref[...], gpost_ref[...])


def _ffn_kernel(x_ref, gpre_ref, gpost_ref, win_ref, wout_ref, o_ref, acc_ref):
    _ffn_body(x_ref[...], gpre_ref, gpost_ref, win_ref, wout_ref, o_ref, acc_ref)


def _mix_ffn_kernel(x_ref, ma_ref, mb_ref, mc_ref, md_ref, wo_ref, gmix_ref, gpre_ref, gpost_ref,
                    win_ref, wout_ref, o_ref, acc_ref):
    y = None
    for i, m_ref in enumerate((ma_ref, mb_ref, mc_ref, md_ref)):
        part = jnp.dot(m_ref[...].astype(BF16), wo_ref[i * GROUP_W:(i + 1) * GROUP_W, :],
                       preferred_element_type=F32)
        y = part if y is None else y + part
    x = x_ref[...] + _rms(y, gmix_ref[...])
    _ffn_body(x, gpre_ref, gpost_ref, win_ref, wout_ref, o_ref, acc_ref)


def _ffn(x, g_pre, g_post, w_in, w_out, mix=None, w_o=None, g_mix=None):
    t = x.shape[0]
    tm = min(TOKEN_TILE, t)
    assert t % tm == 0
    tok = pl.BlockSpec((tm, D_MODEL), lambda i: (i, 0))
    vec = _resident((1, D_MODEL))
    w_specs = [_resident((D_MODEL, 2 * FFN_DIM)), _resident((FFN_DIM, D_MODEL))]
    if mix is None:
        kern, args = _ffn_kernel, (x, g_pre, g_post, w_in, w_out)
        in_specs = [tok, vec, vec] + w_specs
    else:
        kern, args = _mix_ffn_kernel, (x, *mix, w_o, g_mix, g_pre, g_post, w_in, w_out)
        part = pl.BlockSpec((tm, GROUP_W), lambda i: (i, 0))
        in_specs = [tok, part, part, part, part, _resident((D_MODEL, D_MODEL)), vec, vec, vec] + w_specs
    return pl.pallas_call(
        kern,
        grid=(t // tm,),
        in_specs=in_specs,
        out_specs=tok,
        out_shape=jax.ShapeDtypeStruct((t, D_MODEL), F32),
        scratch_shapes=[pltpu.VMEM((tm, D_MODEL), F32)],
        compiler_params=_cparams("arbitrary"),
        name="mix_ffn" if mix is not None else "ffn",
    )(*args)


def _proj_kernel(x_ref, g_ref, w_ref, o_ref):
    xn = _rms(x_ref[...], g_ref[...]).astype(BF16)
    o_ref[...] = jnp.dot(xn, w_ref[...], preferred_element_type=F32)


def _proj(x, g, w):
    t = x.shape[0]
    tm = min(TOKEN_TILE, t)
    assert t % tm == 0
    return pl.pallas_call(
        _proj_kernel,
        grid=(t // tm,),
        in_specs=[pl.BlockSpec((tm, D_MODEL), lambda i: (i, 0)),
                  _resident((1, D_MODEL)), _resident((D_MODEL, PROJ_COLS))],
        out_specs=pl.BlockSpec((tm, PROJ_COLS), lambda i: (i, 0)),
        out_shape=jax.ShapeDtypeStruct((t, PROJ_COLS), F32),
        compiler_params=_cparams("arbitrary"),
        name="in_proj",
    )(x, g, w)


def _seq_spec(rows, name, n_steps):
    lo, hi = _DST_COLS[name]
    return pl.BlockSpec((rows, hi - lo), lambda b, n, c=_col_block(name): (b * n_steps + n, c))


def _small_spec(rows, n_steps):
    return pl.BlockSpec((rows, V7X_LANES), lambda b, n: (b * n_steps + n, SMALL_BLOCK))


def _out_spec(rows, width, n_steps):
    return pl.BlockSpec((rows, width), lambda b, n: (b * n_steps + n, 0))


def _causal_conv(x, halo_ref, conv0_ref, w_ref, rows, seq_rows):
    first = V7X_SUBLANES - (CONV_K - 1)

    def taps(i):
        y = w_ref[0:1, :] * halo_ref[i, first:first + seq_rows, :]
        for j in range(1, CONV_K):
            y = y + w_ref[j:j + 1, :] * halo_ref[i, first + j:first + j + seq_rows, :]
        return y

    if seq_rows == rows:
        @pl.when(pl.program_id(1) == 0)
        def _():
            halo_ref[0, first:V7X_SUBLANES, :] = conv0_ref[0]

        halo_ref[0, V7X_SUBLANES:V7X_SUBLANES + rows, :] = x
        y = taps(0)
        halo_ref[0, 0:V7X_SUBLANES, :] = x[rows - V7X_SUBLANES:rows, :]
        return y
    parts = []
    for i in range(rows // seq_rows):
        halo_ref[i, first:V7X_SUBLANES, :] = conv0_ref[i]
        halo_ref[i, V7X_SUBLANES:V7X_SUBLANES + seq_rows, :] = x[i * seq_rows:(i + 1) * seq_rows, :]
        parts.append(taps(i))
    return jnp.concatenate(parts, axis=0)


def _level_masks(rows):
    i = np.arange(rows)[:, None]
    j = np.arange(rows)[None, :]
    out = []
    s = 1
    while s < rows:
        out.append(((i // (2 * s) == j // (2 * s)) & (i % (2 * s) >= s) & (j % (2 * s) < s)).astype(np.float32))
        s *= 2
    return np.stack(out)


def _unit_lower_inverse(a_strict, lv_ref, eye):
    t = eye - a_strict * lv_ref[0]
    for lv in range(1, lv_ref.shape[0]):
        a_s = a_strict * lv_ref[lv]
        t = t - _mm(_mm(t, a_s), t)
    return t


def _gdn_kernel(qkv_ref, z_ref, sm_ref, conv0_ref, s0_ref, cw_ref, lane_ref, g_ref, lv_ref,
                o_ref, sfin_ref, halo_ref, s_ref, *, rows, chunk, n_valid, chained):
    n_chunks = rows // chunk
    shift = chunk.bit_length() - 1
    if chained:
        @pl.when(pl.program_id(1) == 0)
        def _():
            s_ref[...] = s0_ref[0]

    act = _silu(_causal_conv(qkv_ref[...], halo_ref, conv0_ref, cw_ref, rows, rows if chained else chunk))
    sm = sm_ref[...]
    row = _iota2(rows, rows, 0)
    col = _iota2(rows, rows, 1)
    same_chunk = lax.shift_right_logical(row, shift) == lax.shift_right_logical(col, shift)
    tril_blocks = jnp.where(same_chunk, (col <= row).astype(F32), 0.0).astype(BF16)
    valid = (_iota2(rows, V7X_LANES, 0) & (chunk - 1)) < n_valid
    log_alpha = jnp.where(valid, -jnp.exp(lane_ref[0:1, :]) * _softplus(sm + lane_ref[1:2, :]), 0.0)
    beta = jnp.where(valid, jax.nn.sigmoid(sm), 0.0)
    gcum = _mm_exact_lhs(tril_blocks, log_alpha)
    gcum_t = gcum.T
    rc = _iota2(chunk, chunk, 0)
    cc = _iota2(chunk, chunk, 1)
    tril = cc <= rc
    strict = cc < rc
    eye = (cc == rc).astype(F32)
    z = z_ref[...]
    chunks = range(n_chunks)
    rsl = [slice(c * chunk, (c + 1) * chunk) for c in chunks]
    outs = {}
    finals = [[None] * GDN_HEADS for _ in range(1 if chained else n_chunks)]
    for first in range(0, GDN_HEADS, GDN_HEAD_GROUP):
        heads = range(first, first + GDN_HEAD_GROUP)
        chains = [(h, c) for h in heads for c in chunks]
        q, k, kb, rhs, qd, gcol = {}, {}, {}, {}, {}, {}
        for h in heads:
            lane = LANE_A + h
            qh = act[:, h * GDN_DK:(h + 1) * GDN_DK]
            kh = act[:, GROUP_W + h * GDN_DK:GROUP_W + (h + 1) * GDN_DK]
            vh = act[:, 2 * GROUP_W + h * GDN_DV:2 * GROUP_W + (h + 1) * GDN_DV]
            qh = qh * lax.rsqrt(jnp.sum(qh * qh, axis=-1, keepdims=True) + EPS) * GDN_DK ** -0.5
            kh = kh * lax.rsqrt(jnp.sum(kh * kh, axis=-1, keepdims=True) + EPS)
            g_h = gcum[:, lane:lane + 1]
            b_h = beta[:, LANE_B + h:LANE_B + h + 1]
            e_h = jnp.exp(g_h)
            kb_h = kh * b_h
            rhs_h = jnp.concatenate([kb_h * e_h, vh * b_h], axis=1)
            qd_h = qh * e_h
            for c in chunks:
                q[h, c], k[h, c], kb[h, c] = qh[rsl[c]], kh[rsl[c]], kb_h[rsl[c]]
                rhs[h, c], qd[h, c], gcol[h, c] = rhs_h[rsl[c]], qd_h[rsl[c]], g_h[rsl[c]]
        decay = {(h, c): jnp.exp(jnp.where(tril, gcol[h, c] - gcum_t[LANE_A + h:LANE_A + h + 1, rsl[c]], NEG_BIG))
                 for h, c in chains}
        kk = {i: _mm(kb[i], k[i], _NT) for i in chains}
        qk = {i: _mm(q[i], k[i], _NT) * decay[i] for i in chains}
        a_strict = {i: jnp.where(strict, kk[i] * decay[i], 0.0) for i in chains}
        t_inv = {i: eye - a_strict[i] * lv_ref[0] for i in chains}
        for lv in range(1, lv_ref.shape[0]):
            half = {i: _mm(t_inv[i], a_strict[i] * lv_ref[lv]) for i in chains}
            t_inv = {i: t_inv[i] - _mm(half[i], t_inv[i]) for i in chains}
        sol = {i: _mm(t_inv[i], rhs[i]) for i in chains}
        glast = {i: gcol[i][chunk - 1:chunk] for i in chains}
        k_dec = {i: k[i] * jnp.exp(glast[i] - gcol[i]) for i in chains}
        state = {h: s_ref[h] for h in heads} if chained else None
        for c in chunks:
            cur = state if chained else {h: s0_ref[c, h] for h in heads}
            w_s = {h: _mm(sol[h, c][:, :GDN_DK], cur[h]) for h in heads}
            o_s = {h: _mm(qd[h, c], cur[h]) for h in heads}
            u_new = {h: sol[h, c][:, GDN_DK:] - w_s[h] for h in heads}
            for h in heads:
                outs[h, c] = o_s[h] + _mm(qk[h, c], u_new[h])
            nxt = {h: cur[h] * jnp.exp(glast[h, c]) + _mm(k_dec[h, c], u_new[h], _TN) for h in heads}
            if chained:
                state = nxt
            else:
                for h in heads:
                    finals[c][h] = nxt[h]
        if chained:
            for h in heads:
                finals[0][h] = state[h]
    head_outs = []
    for h in range(GDN_HEADS):
        o = outs[h, 0] if n_chunks == 1 else jnp.concatenate([outs[h, c] for c in chunks], axis=0)
        head_outs.append(_rms(o, g_ref[...]) * _silu(z[:, h * GDN_DV:(h + 1) * GDN_DV]))
    o_ref[...] = jnp.concatenate(head_outs, axis=1)
    new_states = jnp.stack([jnp.stack(per_seq) for per_seq in finals])
    sfin_ref[...] = new_states
    if chained:
        s_ref[...] = new_states[0]


def _gdn(p, conv0, s0, conv_w, a_log, dt_bias, norm_g, n_seq, seq_len, rows, chunk, n_valid):
    chained = seq_len > chunk
    seqs_per_step = 1 if chained else rows // chunk
    n_steps = seq_len // rows if chained else 1
    n_blocks = n_seq // seqs_per_step
    assert n_blocks * seqs_per_step == n_seq and chunk & (chunk - 1) == 0
    lane = jnp.zeros((2, V7X_LANES), F32)
    lane = lane.at[0, LANE_A:LANE_A + GDN_HEADS].set(a_log).at[1, LANE_A:LANE_A + GDN_HEADS].set(dt_bias)
    levels = jnp.asarray(_level_masks(chunk))
    halo_rows = V7X_SUBLANES + (rows if chained else chunk)
    return pl.pallas_call(
        functools.partial(_gdn_kernel, rows=rows, chunk=chunk, n_valid=n_valid, chained=chained),
        grid=(n_blocks, n_steps),
        in_specs=[_seq_spec(rows, 'qkv', n_steps), _seq_spec(rows, 'z_a', n_steps), _small_spec(rows, n_steps),
                  pl.BlockSpec((seqs_per_step, CONV_K - 1, 3 * GROUP_W), lambda b, n: (b, 0, 0)),
                  pl.BlockSpec((seqs_per_step, GDN_HEADS, GDN_DK, GDN_DV), lambda b, n: (b, 0, 0, 0)),
                  _resident((CONV_K, 3 * GROUP_W)), _resident((2, V7X_LANES)), _resident((1, GDN_DV)),
                  _resident(levels.shape)],
        out_specs=[_out_spec(rows, GROUP_W, n_steps),
                   pl.BlockSpec((seqs_per_step, GDN_HEADS, GDN_DK, GDN_DV), lambda b, n: (b, 0, 0, 0))],
        out_shape=[jax.ShapeDtypeStruct((n_seq * seq_len, GROUP_W), F32),
                   jax.ShapeDtypeStruct((n_seq, GDN_HEADS, GDN_DK, GDN_DV), F32)],
        scratch_shapes=[pltpu.VMEM((seqs_per_step, halo_rows, 3 * GROUP_W), F32),
                        pltpu.VMEM((GDN_HEADS, GDN_DK, GDN_DV), F32)],
        compiler_params=_cparams("arbitrary", "arbitrary"),
        name="gdn",
    )(p, p, p, conv0, s0, conv_w, lane, norm_g.reshape(1, GDN_DV), levels)


def _gmlp_kernel(u_ref, v_ref, lng_ref, lnb_ref, ws_ref, bias_ref, o_ref, vb_ref, *, rows, chunk):
    u = jax.nn.gelu(u_ref[...], approximate=True)
    v = jax.nn.gelu(v_ref[...], approximate=True)
    mu = jnp.mean(v, axis=-1, keepdims=True)
    var = jnp.mean(jnp.square(v - mu), axis=-1, keepdims=True)
    v = (v - mu) * lax.rsqrt(var + EPS) * lng_ref[...] + lnb_ref[...]
    vb_ref[...] = v
    tril = _iota2(chunk, chunk, 1) <= _iota2(chunk, chunk, 0)
    ws = [jnp.where(tril, ws_ref[g], 0.0).astype(BF16) for g in range(MLP_GROUPS)]
    for c in range(rows // chunk):
        rs = slice(c * chunk, (c + 1) * chunk)
        for g in range(MLP_GROUPS):
            ls = slice(g * MLP_GW, (g + 1) * MLP_GW)
            mixed = _mm(ws[g], v[rs, ls]) + bias_ref[:, ls]
            o_ref[rs, ls] = u[rs, ls] * mixed


def _gmlp(p, ln_g, ln_b, ws, bs, n_seq, seq_len, rows, chunk):
    n_steps = seq_len // rows
    bias = jnp.repeat(bs[:, :chunk].T, MLP_GW, axis=1)
    return pl.pallas_call(
        functools.partial(_gmlp_kernel, rows=rows, chunk=chunk),
        grid=(n_seq, n_steps),
        in_specs=[_seq_spec(rows, 'u', n_steps), _seq_spec(rows, 'v', n_steps),
                  _resident((1, GROUP_W)), _resident((1, GROUP_W)),
                  _resident((MLP_GROUPS, chunk, chunk)), _resident((chunk, GROUP_W))],
        out_specs=[_out_spec(rows, GROUP_W, n_steps), _out_spec(rows, GROUP_W, n_steps)],
        out_shape=[jax.ShapeDtypeStruct((n_seq * seq_len, GROUP_W), F32)] * 2,
        compiler_params=_cparams("arbitrary", "arbitrary"),
        name="gmlp",
    )(p, p, ln_g.reshape(1, GROUP_W), ln_b.reshape(1, GROUP_W), ws[:, :chunk, :chunk], bias)


def _ssd_kernel(xbc_ref, z_ref, sm_ref, conv0_ref, h0_ref, cw_ref, cb_ref, lane_ref, g_ref,
                o_ref, hfin_ref, halo_ref, h_ref, *, rows, seq_rows, n_valid):
    chained = seq_rows == rows
    n_seqs = rows // seq_rows
    shift = seq_rows.bit_length() - 1
    if chained:
        @pl.when(pl.program_id(1) == 0)
        def _():
            h_ref[...] = h0_ref[0]

    act = _silu(_causal_conv(xbc_ref[...], halo_ref, conv0_ref, cw_ref, rows, seq_rows) + cb_ref[...])
    sm = sm_ref[...]
    row = _iota2(rows, rows, 0)
    col = _iota2(rows, rows, 1)
    tril = (col <= row) & (lax.shift_right_logical(row, shift) == lax.shift_right_logical(col, shift))
    valid = (_iota2(rows, V7X_LANES, 0) & (seq_rows - 1)) < n_valid
    dt = jnp.where(valid, _softplus(sm + lane_ref[1:2, :]), 0.0)
    acum = _mm_exact_lhs(jnp.where(tril, 1.0, 0.0).astype(BF16), dt * -jnp.exp(lane_ref[0:1, :]))
    acum_t = acum.T
    heads = range(SSM_HEADS)
    seqs = range(n_seqs)
    rsl = [slice(i * seq_rows, (i + 1) * seq_rows) for i in seqs]
    group_of = [h // (SSM_HEADS // SSM_GROUPS) for h in heads]
    b_g = [act[:, GROUP_W + g * SSM_STATE:GROUP_W + (g + 1) * SSM_STATE] for g in range(SSM_GROUPS)]
    c_g = [act[:, GROUP_W + (SSM_GROUPS + g) * SSM_STATE:GROUP_W + (SSM_GROUPS + g + 1) * SSM_STATE]
           for g in range(SSM_GROUPS)]
    cb = [_mm(c_g[g], b_g[g], _NT) for g in range(SSM_GROUPS)]
    acol = [acum[:, LANE_DT + h:LANE_DT + h + 1] for h in heads]
    decay = [jnp.exp(jnp.where(tril, acol[h] - acum_t[LANE_DT + h:LANE_DT + h + 1, :], NEG_BIG)) for h in heads]
    x = [act[:, h * SSM_HEAD_DIM:(h + 1) * SSM_HEAD_DIM] for h in heads]
    xdt = [x[h] * dt[:, LANE_DT + h:LANE_DT + h + 1] for h in heads]
    c_dec = [c_g[group_of[h]] * jnp.exp(acol[h]) for h in heads]
    y_intra = [_mm(cb[group_of[h]] * decay[h], xdt[h]) for h in heads]
    state = {(h, i): (h_ref[h] if chained else h0_ref[i, h]) for h in heads for i in seqs}
    alast = {(h, i): acol[h][(i + 1) * seq_rows - 1:(i + 1) * seq_rows] for h in heads for i in seqs}
    y_inter = {(h, i): _mm(c_dec[h][rsl[i]], state[h, i], _NT) for h in heads for i in seqs}
    h_new = {(h, i): state[h, i] * jnp.exp(alast[h, i])
             + _mm(xdt[h][rsl[i]], b_g[group_of[h]][rsl[i]] * jnp.exp(alast[h, i] - acol[h][rsl[i]]), _TN)
             for h in heads for i in seqs}
    y = jnp.concatenate(
        [y_intra[h] + (y_inter[h, 0] if chained else jnp.concatenate([y_inter[h, i] for i in seqs], axis=0))
         + lane_ref[2:3, LANE_DT + h:LANE_DT + h + 1] * x[h] for h in heads], axis=1)
    new_states = jnp.stack([jnp.stack([h_new[h, i] for h in heads]) for i in seqs])
    hfin_ref[...] = new_states
    if chained:
        h_ref[...] = new_states[0]
    o_ref[...] = _rms(y * _silu(z_ref[...]), g_ref[...])


def _ssd(p, conv0, h0, conv_w, conv_b, a_log, dt_bias, d_skip, norm_g, n_seq, seq_len, rows, n_valid):
    chained = seq_len >= rows
    seq_rows = rows if chained else seq_len
    seqs_per_step = rows // seq_rows
    n_steps = seq_len // rows if chained else 1
    n_blocks = n_seq // seqs_per_step
    assert n_blocks * seqs_per_step == n_seq and seq_rows & (seq_rows - 1) == 0
    lane = jnp.zeros((3, V7X_LANES), F32)
    lane = (lane.at[0, LANE_DT:LANE_DT + SSM_HEADS].set(a_log)
            .at[1, LANE_DT:LANE_DT + SSM_HEADS].set(dt_bias)
            .at[2, LANE_DT:LANE_DT + SSM_HEADS].set(d_skip))
    return pl.pallas_call(
        functools.partial(_ssd_kernel, rows=rows, seq_rows=seq_rows, n_valid=n_valid),
        grid=(n_blocks, n_steps),
        in_specs=[_seq_spec(rows, 'xbc', n_steps), _seq_spec(rows, 'z_c', n_steps), _small_spec(rows, n_steps),
                  pl.BlockSpec((seqs_per_step, CONV_K - 1, SSM_CONV_W), lambda b, n: (b, 0, 0)),
                  pl.BlockSpec((seqs_per_step, SSM_HEADS, SSM_HEAD_DIM, SSM_STATE), lambda b, n: (b, 0, 0, 0)),
                  _resident((CONV_K, SSM_CONV_W)), _resident((1, SSM_CONV_W)), _resident((3, V7X_LANES)),
                  _resident((1, GROUP_W))],
        out_specs=[_out_spec(rows, GROUP_W, n_steps),
                   pl.BlockSpec((seqs_per_step, SSM_HEADS, SSM_HEAD_DIM, SSM_STATE), lambda b, n: (b, 0, 0, 0))],
        out_shape=[jax.ShapeDtypeStruct((n_seq * seq_len, GROUP_W), F32),
                   jax.ShapeDtypeStruct((n_seq, SSM_HEADS, SSM_HEAD_DIM, SSM_STATE), F32)],
        scratch_shapes=[pltpu.VMEM((seqs_per_step, V7X_SUBLANES + seq_rows, SSM_CONV_W), F32),
                        pltpu.VMEM((SSM_HEADS, SSM_HEAD_DIM, SSM_STATE), F32)],
        compiler_params=_cparams("arbitrary", "arbitrary"),
        name="ssd",
    )(p, p, p, conv0, h0, conv_w, conv_b.reshape(1, SSM_CONV_W), lane, norm_g.reshape(1, GROUP_W))


def _rope_lanes(x, cos, sin_signed):
    lane = _iota2(x.shape[0], V7X_LANES, 1)
    half = MLA_ROPE // 2
    swapped = jnp.where(lane % MLA_ROPE < half,
                        pltpu.roll(x, V7X_LANES - half, axis=1), pltpu.roll(x, half, axis=1))
    return x * cos + swapped * sin_signed


def _rope_tables(pos):
    half = MLA_ROPE // 2
    inv_freq = ROPE_THETA ** (-jnp.arange(half, dtype=F32) / half)
    ang = pos.astype(F32)[:, None] * inv_freq[None, :]
    cos, sin = jnp.cos(ang), jnp.sin(ang)
    reps = V7X_LANES // MLA_ROPE
    return jnp.tile(jnp.concatenate([cos, cos], axis=1), (1, reps)), jnp.tile(jnp.concatenate([-sin, sin], axis=1), (1, reps))


def _mla_project(cq_ref, ckv_ref, sm_ref, cos_ref, sin_ref, gq_ref, gkv_ref, wn_ref, wpe_ref, wuk_ref):
    c_q = _rms(cq_ref[...], gq_ref[...]).astype(BF16)
    c_kv = _rms(ckv_ref[...], gkv_ref[...])
    cos, sin = cos_ref[...], sin_ref[...]
    lane = _iota2(c_kv.shape[0], V7X_LANES, 1)
    k_pe = _rope_lanes(jnp.where(lane < MLA_ROPE, sm_ref[...], 0.0), cos, sin)
    q_nope = jnp.dot(c_q, wn_ref[...], preferred_element_type=F32)
    q_heads = []
    for h in range(MLA_HEADS):
        q_lat = _mm(q_nope[:, h * MLA_NOPE:(h + 1) * MLA_NOPE], wuk_ref[h], _NT)
        q_pe = _rope_lanes(jnp.dot(c_q, wpe_ref[h], preferred_element_type=F32), cos, sin)
        q_heads.append((jnp.concatenate([q_lat, q_pe], axis=1) * MLA_SCALE).astype(BF16))
    return c_kv, k_pe, q_heads


def _softmax_step(q, keys, m, l, acc, mask=None):
    s = lax.dot_general(q, keys, _NT, preferred_element_type=F32)
    if mask is not None:
        s = jnp.where(mask, s, NEG_BIG)
    m_new = jnp.maximum(m, jnp.max(s, axis=-1, keepdims=True))
    p = jnp.exp(s - m_new)
    alpha = jnp.exp(m - m_new)
    l = alpha * l + jnp.sum(p, axis=-1, keepdims=True)
    acc = alpha * acc + jnp.dot(p.astype(BF16), keys[:, :MLA_KV_RANK], preferred_element_type=F32)
    return m_new, l, acc


def _mla_prompt_kernel(cq_ref, ckv_ref, sm_ref, cos_ref, sin_ref, gq_ref, gkv_ref, wn_ref, wpe_ref, wuk_ref,
                       wuv_ref, o_ref, ckv_out_ref, kpe_out_ref, keys_ref, vt_ref, q_ref, m_ref, acc_ref,
                       *, rows, kv_block):
    n = pl.program_id(1)
    qrows = MLA_HEADS * rows
    c_kv, k_pe, q_heads = _mla_project(cq_ref, ckv_ref, sm_ref, cos_ref, sin_ref, gq_ref, gkv_ref,
                                       wn_ref, wpe_ref, wuk_ref)
    ckv_out_ref[0] = c_kv
    kpe_out_ref[0] = k_pe[:, :MLA_ROPE]
    base = pl.multiple_of(n * rows, rows)
    keys_ref[pl.ds(base, rows), :] = jnp.concatenate([c_kv, k_pe], axis=1).astype(BF16)
    vt_ref[0:MLA_KV_RANK, pl.ds(base, rows)] = c_kv.T.astype(BF16)
    vt_ref[MLA_KV_RANK:MLA_VT_ROWS, pl.ds(base, rows)] = jnp.ones((MLA_VT_ROWS - MLA_KV_RANK, rows), BF16)
    q_ref[...] = jnp.concatenate(q_heads, axis=0)
    m_ref[...] = jnp.full((1, qrows), NEG_BIG, F32)
    acc_ref[...] = jnp.zeros((MLA_VT_ROWS, qrows), F32)

    def attend(off, mask):
        s = lax.dot_general(keys_ref[pl.ds(off, kv_block), :], q_ref[...], _NT, preferred_element_type=F32)
        if mask is not None:
            s = jnp.where(mask, s, NEG_BIG)
        m_old = m_ref[...]
        m_new = jnp.maximum(m_old, jnp.max(s, axis=0, keepdims=True))
        p = jnp.exp((s - m_new).astype(BF16))
        alpha = jnp.exp(m_old - m_new)
        acc_ref[...] = alpha * acc_ref[...] + jnp.dot(vt_ref[:, pl.ds(off, kv_block)], p,
                                                      preferred_element_type=F32)
        m_ref[...] = m_new

    def body(j, carry):
        attend(pl.multiple_of(j * kv_block, kv_block), None)
        return carry

    blocks_per_step = rows // kv_block
    lax.fori_loop(0, n * blocks_per_step, body, 0)
    q_pos = _iota2(kv_block, qrows, 1) & (rows - 1)
    for d in range(blocks_per_step):
        attend(base + d * kv_block, d * kv_block + _iota2(kv_block, qrows, 0) <= q_pos)
    o_lat_t = acc_ref[0:MLA_KV_RANK, :] / acc_ref[MLA_KV_RANK:MLA_KV_RANK + 1, :]
    outs = [_mm(o_lat_t[:, h * rows:(h + 1) * rows], wuv_ref[h], _TN) for h in range(MLA_HEADS)]
    o_ref[...] = jnp.concatenate(outs, axis=1)


def _mla_weights(w_uq, w_uk, w_uv):
    w4 = w_uq.reshape(MLA_Q_RANK, MLA_HEADS, MLA_NOPE + MLA_ROPE)
    w_nope = w4[:, :, :MLA_NOPE].reshape(MLA_Q_RANK, MLA_HEADS * MLA_NOPE).astype(BF16)
    w_pe = jnp.pad(w4[:, :, MLA_NOPE:].transpose(1, 0, 2),
                   ((0, 0), (0, 0), (0, V7X_LANES - MLA_ROPE))).astype(BF16)
    return w_nope, w_pe, w_uk.astype(BF16), w_uv.astype(BF16)


def _mla_prompt(p, q_norm_g, kv_norm_g, w_nope, w_pe, w_uk, w_uv, n_seq, seq_len, rows, kv_block):
    n_steps = seq_len // rows
    assert seq_len % rows == 0 and rows % kv_block == 0 and rows & (rows - 1) == 0
    cos, sin = _rope_tables(jnp.arange(seq_len))
    tab = pl.BlockSpec((rows, V7X_LANES), lambda b, n: (n, 0))
    return pl.pallas_call(
        functools.partial(_mla_prompt_kernel, rows=rows, kv_block=kv_block),
        grid=(n_seq, n_steps),
        in_specs=[_seq_spec(rows, 'c_q', n_steps), _seq_spec(rows, 'c_kv', n_steps), _small_spec(rows, n_steps),
                  tab, tab, _resident((1, MLA_Q_RANK)), _resident((1, MLA_KV_RANK)),
                  _resident(w_nope.shape), _resident(w_pe.shape), _resident(w_uk.shape), _resident(w_uv.shape)],
        out_specs=[_out_spec(rows, GROUP_W, n_steps),
                   pl.BlockSpec((1, rows, MLA_KV_RANK), lambda b, n: (b, n, 0)),
                   pl.BlockSpec((1, rows, MLA_ROPE), lambda b, n: (b, n, 0))],
        out_shape=[jax.ShapeDtypeStruct((n_seq * seq_len, GROUP_W), F32),
                   jax.ShapeDtypeStruct((n_seq, seq_len, MLA_KV_RANK), F32),
                   jax.ShapeDtypeStruct((n_seq, seq_len, MLA_ROPE), F32)],
        scratch_shapes=[pltpu.VMEM((seq_len, 2 * V7X_LANES), BF16),
                        pltpu.VMEM((MLA_VT_ROWS, seq_len), BF16),
                        pltpu.VMEM((MLA_HEADS * rows, 2 * V7X_LANES), BF16),
                        pltpu.VMEM((1, MLA_HEADS * rows), F32),
                        pltpu.VMEM((MLA_VT_ROWS, MLA_HEADS * rows), F32)],
        compiler_params=_cparams("arbitrary", "arbitrary"),
        name="mla_prompt",
    )(p, p, p, cos, sin, q_norm_g.reshape(1, MLA_Q_RANK), kv_norm_g.reshape(1, MLA_KV_RANK),
      w_nope, w_pe, w_uk, w_uv)


def _mla_decode_kernel(pt_ref, cq_ref, ckv_ref, sm_ref, cos_ref, sin_ref, gq_ref, gkv_ref, wn_ref, wpe_ref,
                       wuk_ref, wuv_ref, ckv_hbm, kpe_hbm, o_ref, ckv_out_ref, kpe_out_ref,
                       lat_buf, pe_buf, sems, s_ref, latb_ref, *, rows, layer, n_pages, pages_per_chunk):
    seq = pl.program_id(0)
    slot = lax.rem(seq, 2)
    qrows = MLA_HEADS * rows
    chunk = pages_per_chunk * PAGE_SIZE
    n_chunks = n_pages // pages_per_chunk

    def page_copies(src_seq, dst_slot, g):
        page = pt_ref[src_seq, g]
        return (pltpu.make_async_copy(ckv_hbm.at[layer, page], lat_buf.at[dst_slot, g], sems.at[0, dst_slot]),
                pltpu.make_async_copy(kpe_hbm.at[layer, page],
                                      pe_buf.at[dst_slot, :, pl.ds(pl.multiple_of(g * PAGE_SIZE, PAGE_SIZE), PAGE_SIZE)],
                                      sems.at[1, dst_slot]))

    def fetch(src_seq, dst_slot):
        def body(g, carry):
            for cp in page_copies(src_seq, dst_slot, g):
                cp.start()
            return carry
        lax.fori_loop(0, n_pages, body, 0, unroll=4)

    @pl.when(seq == 0)
    def _():
        fetch(0, 0)

    @pl.when(seq + 1 < pl.num_programs(0))
    def _():
        fetch(seq + 1, 1 - slot)

    c_kv, k_pe, q_heads = _mla_project(cq_ref, ckv_ref, sm_ref, cos_ref, sin_ref, gq_ref, gkv_ref,
                                       wn_ref, wpe_ref, wuk_ref)
    ckv_out_ref[0] = c_kv
    kpe_out_ref[0] = k_pe[:, :MLA_ROPE]
    knew = jnp.concatenate([c_kv, k_pe], axis=1).astype(BF16)
    q = jnp.concatenate(q_heads, axis=0)
    q_lat, q_pe = q[:, :MLA_KV_RANK], q[:, MLA_KV_RANK:MLA_KV_RANK + MLA_ROPE]
    causal = _iota2(qrows, rows, 1) <= (_iota2(qrows, rows, 0) & (rows - 1))
    s_new = jnp.where(causal, lax.dot_general(q, knew, _NT, preferred_element_type=F32), NEG_BIG)

    pltpu.make_async_copy(lat_buf.at[slot], lat_buf.at[slot], sems.at[0, slot]).wait()
    pltpu.make_async_copy(pe_buf.at[slot], pe_buf.at[slot], sems.at[1, slot]).wait()

    def to_bf16(g, carry):
        latb_ref[pl.ds(pl.multiple_of(g * PAGE_SIZE, PAGE_SIZE), PAGE_SIZE), :] = lat_buf[slot, g].astype(BF16)
        return carry
    lax.fori_loop(0, n_pages, to_bf16, 0, unroll=8)

    m = jnp.max(s_new, axis=-1, keepdims=True)
    for c in range(n_chunks):
        s_lat = lax.dot_general(q_lat, latb_ref[c * chunk:(c + 1) * chunk, :], _NT, preferred_element_type=F32)
        s_pe = jnp.dot(q_pe, pe_buf[slot, :, c * chunk:(c + 1) * chunk].astype(BF16), preferred_element_type=F32)
        s = s_lat + s_pe
        s_ref[:, c * chunk:(c + 1) * chunk] = s
        m = jnp.maximum(m, jnp.max(s, axis=-1, keepdims=True))

    p_new = jnp.exp(s_new - m)
    l = jnp.sum(p_new, axis=-1, keepdims=True)
    acc = jnp.dot(p_new.astype(BF16), knew[:, :MLA_KV_RANK], preferred_element_type=F32)
    for c in range(n_chunks):
        pr = jnp.exp(s_ref[:, c * chunk:(c + 1) * chunk] - m)
        l = l + jnp.sum(pr, axis=-1, keepdims=True)
        acc = acc + jnp.dot(pr.astype(BF16), latb_ref[c * chunk:(c + 1) * chunk, :], preferred_element_type=F32)
    o_lat = acc / l
    outs = [_mm(o_lat[h * rows:(h + 1) * rows], wuv_ref[h]) for h in range(MLA_HEADS)]
    o_ref[...] = jnp.concatenate(outs, axis=1)


def _mla_decode(p, cache_ckv, cache_kpe, layer, page_table, q_norm_g, kv_norm_g, w_nope, w_pe, w_uk, w_uv,
                n_seq, rows, pages_per_chunk=16):
    n_pages = page_table.shape[1]
    pages_per_chunk = min(pages_per_chunk, n_pages)
    assert n_pages % pages_per_chunk == 0
    past_len = n_pages * PAGE_SIZE
    cos, sin = _rope_tables(past_len + jnp.arange(rows))

    def const(shape):
        return pl.BlockSpec(shape, lambda s, pt: (0,) * len(shape))

    def seq(name):
        lo, hi = _DST_COLS[name]
        return pl.BlockSpec((rows, hi - lo), lambda s, pt, c=_col_block(name): (s, c))

    in_specs = [seq('c_q'), seq('c_kv'), pl.BlockSpec((rows, V7X_LANES), lambda s, pt: (s, SMALL_BLOCK)),
                const((rows, V7X_LANES)), const((rows, V7X_LANES)),
                const((1, MLA_Q_RANK)), const((1, MLA_KV_RANK)),
                const(w_nope.shape), const(w_pe.shape), const(w_uk.shape), const(w_uv.shape),
                pl.BlockSpec(memory_space=pl.ANY), pl.BlockSpec(memory_space=pl.ANY)]
    qrows = MLA_HEADS * rows
    return pl.pallas_call(
        functools.partial(_mla_decode_kernel, rows=rows, layer=layer, n_pages=n_pages,
                          pages_per_chunk=pages_per_chunk),
        grid_spec=pltpu.PrefetchScalarGridSpec(
            num_scalar_prefetch=1,
            grid=(n_seq,),
            in_specs=in_specs,
            out_specs=[pl.BlockSpec((rows, GROUP_W), lambda s, pt: (s, 0)),
                       pl.BlockSpec((1, rows, MLA_KV_RANK), lambda s, pt: (s, 0, 0)),
                       pl.BlockSpec((1, rows, MLA_ROPE), lambda s, pt: (s, 0, 0))],
            scratch_shapes=[pltpu.VMEM((2, n_pages, PAGE_SIZE, MLA_KV_RANK), F32),
                            pltpu.VMEM((2, MLA_ROPE, past_len), F32),
                            pltpu.SemaphoreType.DMA((2, 2)),
                            pltpu.VMEM((qrows, past_len), F32),
                            pltpu.VMEM((past_len, MLA_KV_RANK), BF16)]),
        out_shape=[jax.ShapeDtypeStruct((n_seq * rows, GROUP_W), F32),
                   jax.ShapeDtypeStruct((n_seq, rows, MLA_KV_RANK), F32),
                   jax.ShapeDtypeStruct((n_seq, rows, MLA_ROPE), F32)],
        compiler_params=_cparams("arbitrary"),
        name="mla_decode",
    )(page_table, p, p, p, cos, sin, q_norm_g.reshape(1, MLA_Q_RANK), kv_norm_g.reshape(1, MLA_KV_RANK),
      w_nope, w_pe, w_uk, w_uv, cache_ckv, cache_kpe)


def _last_rows(p, n_seq, seq_len, n_valid, name):
    lo, hi = _DST_COLS[name]
    return p.reshape(n_seq, seq_len, PROJ_COLS)[:, n_valid - (CONV_K - 1):n_valid, lo:hi]


def _trunk_layer(x, w, layer, n_seq, seq_len, rows, n_valid, gdn_conv, gdn_s, ssm_conv, ssm_h, paged):
    g = w['norm_g']
    assert n_valid == seq_len or seq_len == rows
    step_valid = min(n_valid, rows)
    x = _ffn(x, g[0:1], g[1:2], w['ffn_w_in'][0], w['ffn_w_out'][0])
    p = _proj(x, g[2:3], w['w_in'])
    if seq_len == rows:
        gdn_shape = (rows * SAMPLE_SEQS_PER_STEP, rows, n_valid)
        mlp_shape = (1, n_seq * seq_len, min(SEQ_TILE, n_seq * seq_len), rows)
        ssd_rows = rows * SAMPLE_SEQS_PER_STEP
    else:
        gdn_shape = (rows, GDN_CHUNK, GDN_CHUNK)
        mlp_shape = (n_seq, seq_len, rows, MLP_CHUNK)
        ssd_rows = rows
    out_a, gdn_s_new = _gdn(p, gdn_conv, gdn_s, w['gdn_conv_w'], w['gdn_a_log'], w['gdn_dt_bias'],
                            w['gdn_norm_g'], n_seq, seq_len, *gdn_shape)
    out_b, v_b = _gmlp(p, w['mlp_ln_g'], w['mlp_ln_b'], w['mlp_ws'], w['mlp_bs'], *mlp_shape)
    out_c, ssm_h_new = _ssd(p, ssm_conv, ssm_h, w['ssm_conv_w'], w['ssm_conv_b'], w['ssm_a_log'],
                            w['ssm_dt_bias'], w['ssm_d'], w['ssm_norm_g'], n_seq, seq_len, ssd_rows, step_valid)
    mla_w = (w['mla_q_norm_g'], w['mla_kv_norm_g'], w['mla_w_nope'], w['mla_w_pe'], w['mla_w_uk'], w['mla_w_uv'])
    if paged is None:
        out_d, c_kv, k_pe = _mla_prompt(p, *mla_w, n_seq, seq_len, MLA_Q_TILE, MLA_KV_BLOCK)
    else:
        cache_ckv, cache_kpe, page_table = paged
        out_d, c_kv, k_pe = _mla_decode(p, cache_ckv, cache_kpe, layer, page_table, *mla_w, n_seq, rows)
    x = _ffn(x, g[4:5], g[5:6], w['ffn_w_in'][1], w['ffn_w_out'][1],
             mix=(out_a, out_b, out_c, out_d), w_o=w['w_out'], g_mix=g[3:4])
    gdn_conv_new = _last_rows(p, n_seq, seq_len, n_valid, 'qkv')
    ssm_conv_new = _last_rows(p, n_seq, seq_len, n_valid, 'xbc')
    v_b = v_b.reshape(n_seq, seq_len, GROUP_W)[:, :n_valid]
    return x, (c_kv[:, :n_valid], k_pe[:, :n_valid], gdn_s_new, gdn_conv_new, ssm_h_new, ssm_conv_new, v_b)


def kernel(x_prompt, x_sample, cache_ckv, cache_kpe, page_table, state_gdn_s, state_gdn_conv,
           state_ssm_h, state_ssm_conv, norm_g, ffn_w_in, ffn_w_out, w_in, w_out,
           gdn_conv_w, gdn_a_log, gdn_dt_bias, gdn_norm_g, mlp_ln_g, mlp_ln_b, mlp_ws, mlp_bs,
           ssm_conv_w, ssm_conv_b, ssm_a_log, ssm_dt_bias, ssm_d, ssm_norm_g,
           mla_q_norm_g, mla_w_uq, mla_kv_norm_g, mla_w_uk, mla_w_uv):
    bp, lp, _ = x_prompt.shape
    bs, ls, _ = x_sample.shape
    assert lp % SEQ_TILE == 0 and CONV_K - 1 <= ls <= SAMPLE_ROWS
    w_in_p = jnp.pad(w_in[:, :, _PERM], ((0, 0), (0, 0), (0, PROJ_COLS - PROJ_USED)))
    cache_kpe_t = jnp.swapaxes(cache_kpe, 2, 3)
    weights = dict(norm_g=norm_g, ffn_w_in=ffn_w_in.astype(BF16), ffn_w_out=ffn_w_out.astype(BF16),
                   w_in=w_in_p.astype(BF16), w_out=w_out.astype(BF16),
                   gdn_conv_w=gdn_conv_w, gdn_a_log=gdn_a_log, gdn_dt_bias=gdn_dt_bias, gdn_norm_g=gdn_norm_g,
                   mlp_ln_g=mlp_ln_g, mlp_ln_b=mlp_ln_b, mlp_ws=mlp_ws, mlp_bs=mlp_bs,
                   ssm_conv_w=ssm_conv_w, ssm_conv_b=ssm_conv_b, ssm_a_log=ssm_a_log,
                   ssm_dt_bias=ssm_dt_bias, ssm_d=ssm_d, ssm_norm_g=ssm_norm_g,
                   mla_q_norm_g=mla_q_norm_g, mla_kv_norm_g=mla_kv_norm_g)
    xp = x_prompt.reshape(bp * lp, D_MODEL)
    xs = jnp.pad(x_sample, ((0, 0), (0, SAMPLE_ROWS - ls), (0, 0))).reshape(bs * SAMPLE_ROWS, D_MODEL)
    zeros_p = (jnp.zeros((bp, CONV_K - 1, 3 * GROUP_W), F32), jnp.zeros((bp, GDN_HEADS, GDN_DK, GDN_DV), F32),
               jnp.zeros((bp, CONV_K - 1, SSM_CONV_W), F32), jnp.zeros((bp, SSM_HEADS, SSM_HEAD_DIM, SSM_STATE), F32))
    st_p, st_s = [], []
    for l in range(DEPTH):
        wl = {name: arr[l] for name, arr in weights.items()}
        wl['mla_w_nope'], wl['mla_w_pe'], wl['mla_w_uk'], wl['mla_w_uv'] = _mla_weights(
            mla_w_uq[l], mla_w_uk[l], mla_w_uv[l])
        xp, st = _trunk_layer(xp, wl, l, bp, lp, SEQ_TILE, lp, *zeros_p, None)
        st_p.append(st)
        xs, st = _trunk_layer(xs, wl, l, bs, SAMPLE_ROWS, SAMPLE_ROWS, ls,
                              state_gdn_conv[l], state_gdn_s[l], state_ssm_conv[l], state_ssm_h[l],
                              (cache_ckv, cache_kpe_t, page_table))
        st_s.append(st)

    def stack(states, i):
        return jnp.stack([s[i] for s in states])

    y_prompt = xp.reshape(bp, lp, D_MODEL)
    y_sample = xs.reshape(bs, SAMPLE_ROWS, D_MODEL)[:, :ls]
    return (y_prompt, y_sample,
            stack(st_p, 0), stack(st_p, 1), stack(st_p, 2), stack(st_p, 3), stack(st_p, 4), stack(st_p, 5),
            stack(st_s, 0), stack(st_s, 1), stack(st_s, 2), stack(st_s, 3), stack(st_s, 4), stack(st_s, 5),
            stack(st_s, 6))
```

```python
import functools

import numpy as np
import jax
import jax.numpy as jnp
from jax import lax
from jax.experimental import pallas as pl
from jax.experimental.pallas import tpu as pltpu

F32 = jnp.float32
BF16 = jnp.bfloat16

D_MODEL = 1024
DEPTH = 2
PAGE_SIZE = 128
GROUP_W = 256
CONV_K = 4
FFN_DIM = 2816
EPS = 1e-6

GDN_HEADS = 4
GDN_DK = 64
GDN_DV = 64
MLP_CHUNK = 128
MLP_GROUPS = 4
MLP_GW = 64
SSM_HEADS = 4
SSM_HEAD_DIM = 64
SSM_GROUPS = 2
SSM_STATE = 128
SSM_CONV_W = 768
MLA_HEADS = 4
MLA_NOPE = 64
MLA_ROPE = 32
MLA_V_DIM = 64
MLA_Q_RANK = 256
MLA_KV_RANK = 128
MLA_SCALE = (MLA_NOPE + MLA_ROPE) ** -0.5
ROPE_THETA = 10000.0

V7X_LANES = 128
V7X_SUBLANES = 8
V7X_MXU_DIM = 256
V7X_VMEM_LIMIT_BYTES = 56 * 1024 * 1024

_SRC_COLS = dict(
    qkv=(0, 768), z_a=(768, 1024), a_a=(1024, 1028), b_a=(1028, 1032),
    u=(1032, 1288), v=(1288, 1544),
    z_c=(1544, 1800), xbc=(1800, 2568), dt=(2568, 2572),
    c_q=(2572, 2828), c_kv=(2828, 2956), k_pe=(2956, 2988))
_DST_ORDER = ('qkv', 'xbc', 'z_a', 'u', 'v', 'z_c', 'c_q', 'c_kv', 'k_pe', 'a_a', 'b_a', 'dt')
_DST_COLS = {}
_off = 0
for _name in _DST_ORDER:
    _w = _SRC_COLS[_name][1] - _SRC_COLS[_name][0]
    _DST_COLS[_name] = (_off, _off + _w)
    _off += _w
PROJ_USED = _off
PROJ_COLS = -(-PROJ_USED // V7X_LANES) * V7X_LANES
_PERM = np.concatenate([np.arange(*_SRC_COLS[n]) for n in _DST_ORDER])
SMALL_BLOCK = _DST_COLS['k_pe'][0] // V7X_LANES
LANE_A = _DST_COLS['a_a'][0] % V7X_LANES
LANE_B = _DST_COLS['b_a'][0] % V7X_LANES
LANE_DT = _DST_COLS['dt'][0] % V7X_LANES


def _col_block(name):
    lo, hi = _DST_COLS[name]
    assert lo % (hi - lo) == 0
    return lo // (hi - lo)


FFN_CHUNK = V7X_MXU_DIM
TOKEN_TILE = 512
SEQ_TILE = 256
MLA_Q_TILE = 512
MLA_KV_BLOCK = 512
SAMPLE_ROWS = V7X_SUBLANES
SAMPLE_SEQS_PER_STEP = 8
GDN_CHUNK = 64
GDN_HEAD_GROUP = 4
MLA_VT_ROWS = MLA_KV_RANK + 16
NEG_BIG = -1e30


def _cparams(*sem):
    return pltpu.CompilerParams(dimension_semantics=sem, vmem_limit_bytes=V7X_VMEM_LIMIT_BYTES)


def _rms(x, g):
    return x * lax.rsqrt(jnp.mean(x * x, axis=-1, keepdims=True) + EPS) * g


def _silu(x):
    return x * jax.nn.sigmoid(x)


def _softplus(x):
    return jnp.maximum(x, 0.0) + jnp.log(1.0 + jnp.exp(-jnp.abs(x)))


_NN = (((1,), (0,)), ((), ()))
_NT = (((1,), (1,)), ((), ()))
_TN = (((0,), (0,)), ((), ()))


def _mm(a, b, dims=_NN):
    return lax.dot_general(a.astype(BF16), b.astype(BF16), dims, preferred_element_type=F32)


def _split3(x):
    x0 = x.astype(BF16)
    r1 = x - x0.astype(F32)
    x1 = r1.astype(BF16)
    x2 = (r1 - x1.astype(F32)).astype(BF16)
    return x0, x1, x2


def _mm_exact_lhs(a_bf16, b):
    b0, b1, b2 = _split3(b)
    dot = lambda y: lax.dot_general(a_bf16, y, _NN, preferred_element_type=F32)
    return dot(b0) + (dot(b1) + dot(b2))


def _iota2(n, m, axis):
    return lax.broadcasted_iota(jnp.int32, (n, m), axis)


def _resident(shape):
    return pl.BlockSpec(shape, lambda *_: (0,) * len(shape), pipeline_mode=pl.Buffered(1))


def _ffn_body(x, gpre_ref, gpost_ref, win_ref, wout_ref, o_ref, acc_ref):
    xn = _rms(x, gpre_ref[...]).astype(BF16)
    n_chunks = FFN_DIM // FFN_CHUNK
    for c in range(n_chunks):
        lo = c * FFN_CHUNK
        gate = jnp.dot(xn, win_ref[:, lo:lo + FFN_CHUNK], preferred_element_type=F32)
        up = jnp.dot(xn, win_ref[:, FFN_DIM + lo:FFN_DIM + lo + FFN_CHUNK], preferred_element_type=F32)
        h = (_silu(gate) * up).astype(BF16)
        part = jnp.dot(h, wout_ref[lo:lo + FFN_CHUNK, :], preferred_element_type=F32)
        if c == 0:
            acc_ref[...] = part
        else:
            acc_ref[...] += part
    o_ref[...] = x + 0.5 * _rms(acc_ref[...], gpost_ref[...])


def _ffn_kernel(x_ref, gpre_ref, gpost_ref, win_ref, wout_ref, o_ref, acc_ref):
    _ffn_body(x_ref[...], gpre_ref, gpost_ref, win_ref, wout_ref, o_ref, acc_ref)


def _mix_ffn_kernel(x_ref, ma_ref, mb_ref, mc_ref, md_ref, wo_ref, gmix_ref, gpre_ref, gpost_ref,
                    win_ref, wout_ref, o_ref, acc_ref):
    y = None
    for i, m_ref in enumerate((ma_ref, mb_ref, mc_ref, md_ref)):
        part = jnp.dot(m_ref[...].astype(BF16), wo_ref[i * GROUP_W:(i + 1) * GROUP_W, :],
                       preferred_element_type=F32)
        y = part if y is None else y + part
    x = x_ref[...] + _rms(y, gmix_ref[...])
    _ffn_body(x, gpre_ref, gpost_ref, win_ref, wout_ref, o_ref, acc_ref)


def _ffn(x, g_pre, g_post, w_in, w_out, mix=None, w_o=None, g_mix=None):
    t = x.shape[0]
    tm = min(TOKEN_TILE, t)
    assert t % tm == 0
    tok = pl.BlockSpec((tm, D_MODEL), lambda i: (i, 0))
    vec = _resident((1, D_MODEL))
    w_specs = [_resident((D_MODEL, 2 * FFN_DIM)), _resident((FFN_DIM, D_MODEL))]
    if mix is None:
        kern, args = _ffn_kernel, (x, g_pre, g_post, w_in, w_out)
        in_specs = [tok, vec, vec] + w_specs
    else:
        kern, args = _mix_ffn_kernel, (x, *mix, w_o, g_mix, g_pre, g_post, w_in, w_out)
        part = pl.BlockSpec((tm, GROUP_W), lambda i: (i, 0))
        in_specs = [tok, part, part, part, part, _resident((D_MODEL, D_MODEL)), vec, vec, vec] + w_specs
    return pl.pallas_call(
        kern,
        grid=(t // tm,),
        in_specs=in_specs,
        out_specs=tok,
        out_shape=jax.ShapeDtypeStruct((t, D_MODEL), F32),
        scratch_shapes=[pltpu.VMEM((tm, D_MODEL), F32)],
        compiler_params=_cparams("arbitrary"),
        name="mix_ffn" if mix is not None else "ffn",
    )(*args)


def _proj_kernel(x_ref, g_ref, w_ref, o_ref):
    xn = _rms(x_ref[...], g_ref[...]).astype(BF16)
    o_ref[...] = jnp.dot(xn, w_ref[...], preferred_element_type=F32)


def _proj(x, g, w):
    t = x.shape[0]
    tm = min(TOKEN_TILE, t)
    assert t % tm == 0
    return pl.pallas_call(
        _proj_kernel,
        grid=(t // tm,),
        in_specs=[pl.BlockSpec((tm, D_MODEL), lambda i: (i, 0)),
                  _resident((1, D_MODEL)), _resident((D_MODEL, PROJ_COLS))],
        out_specs=pl.BlockSpec((tm, PROJ_COLS), lambda i: (i, 0)),
        out_shape=jax.ShapeDtypeStruct((t, PROJ_COLS), F32),
        compiler_params=_cparams("arbitrary"),
        name="in_proj",
    )(x, g, w)


def _seq_spec(rows, name, n_steps):
    lo, hi = _DST_COLS[name]
    return pl.BlockSpec((rows, hi - lo), lambda b, n, c=_col_block(name): (b * n_steps + n, c))


def _small_spec(rows, n_steps):
    return pl.BlockSpec((rows, V7X_LANES), lambda b, n: (b * n_steps + n, SMALL_BLOCK))


def _out_spec(rows, width, n_steps):
    return pl.BlockSpec((rows, width), lambda b, n: (b * n_steps + n, 0))


def _causal_conv(x, halo_ref, conv0_ref, conv_out_ref, w_ref, rows, seq_rows, n_valid):
    first = V7X_SUBLANES - (CONV_K - 1)

    def taps(i):
        y = w_ref[0:1, :] * halo_ref[i, first:first + seq_rows, :]
        for j in range(1, CONV_K):
            y = y + w_ref[j:j + 1, :] * halo_ref[i, first + j:first + j + seq_rows, :]
        return y

    if seq_rows == rows:
        @pl.when(pl.program_id(1) == 0)
        def _():
            halo_ref[0, first:V7X_SUBLANES, :] = conv0_ref[0]

        halo_ref[0, V7X_SUBLANES:V7X_SUBLANES + rows, :] = x
        y = taps(0)
        halo_ref[0, 0:V7X_SUBLANES, :] = x[rows - V7X_SUBLANES:rows, :]
        conv_out_ref[0] = x[rows - (CONV_K - 1):rows, :]
        return y
    parts = []
    for i in range(rows // seq_rows):
        halo_ref[i, first:V7X_SUBLANES, :] = conv0_ref[i]
        halo_ref[i, V7X_SUBLANES:V7X_SUBLANES + seq_rows, :] = x[i * seq_rows:(i + 1) * seq_rows, :]
        parts.append(taps(i))
        conv_out_ref[i] = x[i * seq_rows + n_valid - (CONV_K - 1):i * seq_rows + n_valid, :]
    return jnp.concatenate(parts, axis=0)


def _level_masks(rows):
    i = np.arange(rows)[:, None]
    j = np.arange(rows)[None, :]
    out = []
    s = 1
    while s < rows:
        out.append(((i // (2 * s) == j // (2 * s)) & (i % (2 * s) >= s) & (j % (2 * s) < s)).astype(np.float32))
        s *= 2
    return np.stack(out)


def _unit_lower_inverse(a_strict, lv_ref, eye):
    t = eye - a_strict * lv_ref[0]
    for lv in range(1, lv_ref.shape[0]):
        a_s = a_strict * lv_ref[lv]
        t = t - _mm(_mm(t, a_s), t)
    return t


def _gdn_kernel(qkv_ref, z_ref, sm_ref, conv0_ref, s0_ref, cw_ref, lane_ref, g_ref, lv_ref,
                o_ref, sfin_ref, conv_out_ref, halo_ref, s_ref, *, rows, chunk, n_valid, chained):
    n_chunks = rows // chunk
    shift = chunk.bit_length() - 1
    if chained:
        @pl.when(pl.program_id(1) == 0)
        def _():
            s_ref[...] = s0_ref[0]

    act = _silu(_causal_conv(qkv_ref[...], halo_ref, conv0_ref, conv_out_ref, cw_ref, rows,
                             rows if chained else chunk, n_valid))
    sm = sm_ref[...]
    row = _iota2(rows, rows, 0)
    col = _iota2(rows, rows, 1)
    same_chunk = lax.shift_right_logical(row, shift) == lax.shift_right_logical(col, shift)
    tril_blocks = jnp.where(same_chunk, (col <= row).astype(F32), 0.0).astype(BF16)
    valid = (_iota2(rows, V7X_LANES, 0) & (chunk - 1)) < n_valid
    log_alpha = jnp.where(valid, -jnp.exp(lane_ref[0:1, :]) * _softplus(sm + lane_ref[1:2, :]), 0.0)
    beta = jnp.where(valid, jax.nn.sigmoid(sm), 0.0)
    gcum = _mm_exact_lhs(tril_blocks, log_alpha)
    gcum_t = gcum.T
    rc = _iota2(chunk, chunk, 0)
    cc = _iota2(chunk, chunk, 1)
    tril = cc <= rc
    strict = cc < rc
    eye = (cc == rc).astype(F32)
    z = z_ref[...]
    chunks = range(n_chunks)
    rsl = [slice(c * chunk, (c + 1) * chunk) for c in chunks]
    outs = {}
    finals = [[None] * GDN_HEADS for _ in range(1 if chained else n_chunks)]
    for first in range(0, GDN_HEADS, GDN_HEAD_GROUP):
        heads = range(first, first + GDN_HEAD_GROUP)
        chains = [(h, c) for h in heads for c in chunks]
        q, k, kb, rhs, qd, gcol = {}, {}, {}, {}, {}, {}
        for h in heads:
            lane = LANE_A + h
            qh = act[:, h * GDN_DK:(h + 1) * GDN_DK]
            kh = act[:, GROUP_W + h * GDN_DK:GROUP_W + (h + 1) * GDN_DK]
            vh = act[:, 2 * GROUP_W + h * GDN_DV:2 * GROUP_W + (h + 1) * GDN_DV]
            qh = qh * lax.rsqrt(jnp.sum(qh * qh, axis=-1, keepdims=True) + EPS) * GDN_DK ** -0.5
            kh = kh * lax.rsqrt(jnp.sum(kh * kh, axis=-1, keepdims=True) + EPS)
            g_h = gcum[:, lane:lane + 1]
            b_h = beta[:, LANE_B + h:LANE_B + h + 1]
            e_h = jnp.exp(g_h)
            kb_h = kh * b_h
            rhs_h = jnp.concatenate([kb_h * e_h, vh * b_h], axis=1)
            qd_h = qh * e_h
            for c in chunks:
                q[h, c], k[h, c], kb[h, c] = qh[rsl[c]], kh[rsl[c]], kb_h[rsl[c]]
                rhs[h, c], qd[h, c], gcol[h, c] = rhs_h[rsl[c]], qd_h[rsl[c]], g_h[rsl[c]]
        decay = {(h, c): jnp.exp(jnp.where(tril, gcol[h, c] - gcum_t[LANE_A + h:LANE_A + h + 1, rsl[c]], NEG_BIG))
                 for h, c in chains}
        kk = {i: _mm(kb[i], k[i], _NT) for i in chains}
        qk = {i: _mm(q[i], k[i], _NT) * decay[i] for i in chains}
        a_strict = {i: jnp.where(strict, kk[i] * decay[i], 0.0) for i in chains}
        t_inv = {i: eye - a_strict[i] * lv_ref[0] for i in chains}
        for lv in range(1, lv_ref.shape[0]):
            half = {i: _mm(t_inv[i], a_strict[i] * lv_ref[lv]) for i in chains}
            t_inv = {i: t_inv[i] - _mm(half[i], t_inv[i]) for i in chains}
        sol = {i: _mm(t_inv[i], rhs[i]) for i in chains}
        glast = {i: gcol[i][chunk - 1:chunk] for i in chains}
        k_dec = {i: k[i] * jnp.exp(glast[i] - gcol[i]) for i in chains}
        state = {h: s_ref[h] for h in heads} if chained else None
        for c in chunks:
            cur = state if chained else {h: s0_ref[c, h] for h in heads}
            w_s = {h: _mm(sol[h, c][:, :GDN_DK], cur[h]) for h in heads}
            o_s = {h: _mm(qd[h, c], cur[h]) for h in heads}
            u_new = {h: sol[h, c][:, GDN_DK:] - w_s[h] for h in heads}
            for h in heads:
                outs[h, c] = o_s[h] + _mm(qk[h, c], u_new[h])
            nxt = {h: cur[h] * jnp.exp(glast[h, c]) + _mm(k_dec[h, c], u_new[h], _TN) for h in heads}
            if chained:
                state = nxt
            else:
                for h in heads:
                    finals[c][h] = nxt[h]
        if chained:
            for h in heads:
                finals[0][h] = state[h]
    head_outs = []
    for h in range(GDN_HEADS):
        o = outs[h, 0] if n_chunks == 1 else jnp.concatenate([outs[h, c] for c in chunks], axis=0)
        head_outs.append(_rms(o, g_ref[...]) * _silu(z[:, h * GDN_DV:(h + 1) * GDN_DV]))
    o_ref[...] = jnp.concatenate(head_outs, axis=1)
    new_states = jnp.stack([jnp.stack(per_seq) for per_seq in finals])
    sfin_ref[...] = new_states
    if chained:
        s_ref[...] = new_states[0]


def _gdn(p, conv0, s0, conv_w, a_log, dt_bias, norm_g, n_seq, seq_len, rows, chunk, n_valid):
    chained = seq_len > chunk
    seqs_per_step = 1 if chained else rows // chunk
    n_steps = seq_len // rows if chained else 1
    n_blocks = n_seq // seqs_per_step
    assert n_blocks * seqs_per_step == n_seq and chunk & (chunk - 1) == 0
    lane = jnp.zeros((2, V7X_LANES), F32)
    lane = lane.at[0, LANE_A:LANE_A + GDN_HEADS].set(a_log).at[1, LANE_A:LANE_A + GDN_HEADS].set(dt_bias)
    levels = jnp.asarray(_level_masks(chunk))
    halo_rows = V7X_SUBLANES + (rows if chained else chunk)
    return pl.pallas_call(
        functools.partial(_gdn_kernel, rows=rows, chunk=chunk, n_valid=n_valid, chained=chained),
        grid=(n_blocks, n_steps),
        in_specs=[_seq_spec(rows, 'qkv', n_steps), _seq_spec(rows, 'z_a', n_steps), _small_spec(rows, n_steps),
                  pl.BlockSpec((seqs_per_step, CONV_K - 1, 3 * GROUP_W), lambda b, n: (b, 0, 0)),
                  pl.BlockSpec((seqs_per_step, GDN_HEADS, GDN_DK, GDN_DV), lambda b, n: (b, 0, 0, 0)),
                  _resident((CONV_K, 3 * GROUP_W)), _resident((2, V7X_LANES)), _resident((1, GDN_DV)),
                  _resident(levels.shape)],
        out_specs=[_out_spec(rows, GROUP_W, n_steps),
                   pl.BlockSpec((seqs_per_step, GDN_HEADS, GDN_DK, GDN_DV), lambda b, n: (b, 0, 0, 0)),
                   pl.BlockSpec((seqs_per_step, CONV_K - 1, 3 * GROUP_W), lambda b, n: (b, 0, 0))],
        out_shape=[jax.ShapeDtypeStruct((n_seq * seq_len, GROUP_W), F32),
                   jax.ShapeDtypeStruct((n_seq, GDN_HEADS, GDN_DK, GDN_DV), F32),
                   jax.ShapeDtypeStruct((n_seq, CONV_K - 1, 3 * GROUP_W), F32)],
        scratch_shapes=[pltpu.VMEM((seqs_per_step, halo_rows, 3 * GROUP_W), F32),
                        pltpu.VMEM((GDN_HEADS, GDN_DK, GDN_DV), F32)],
        compiler_params=_cparams("arbitrary", "arbitrary"),
        name="gdn",
    )(p, p, p, conv0, s0, conv_w, lane, norm_g.reshape(1, GDN_DV), levels)


def _gmlp_kernel(u_ref, v_ref, lng_ref, lnb_ref, ws_ref, bias_ref, o_ref, vb_ref, *, rows, chunk):
    u = jax.nn.gelu(u_ref[...], approximate=True)
    v = jax.nn.gelu(v_ref[...], approximate=True)
    mu = jnp.mean(v, axis=-1, keepdims=True)
    var = jnp.mean(jnp.square(v - mu), axis=-1, keepdims=True)
    v = (v - mu) * lax.rsqrt(var + EPS) * lng_ref[...] + lnb_ref[...]
    vb_ref[...] = v
    tril = _iota2(chunk, chunk, 1) <= _iota2(chunk, chunk, 0)
    ws = [jnp.where(tril, ws_ref[g], 0.0).astype(BF16) for g in range(MLP_GROUPS)]
    for c in range(rows // chunk):
        rs = slice(c * chunk, (c + 1) * chunk)
        for g in range(MLP_GROUPS):
            ls = slice(g * MLP_GW, (g + 1) * MLP_GW)
            mixed = _mm(ws[g], v[rs, ls]) + bias_ref[:, ls]
            o_ref[rs, ls] = u[rs, ls] * mixed


def _gmlp(p, ln_g, ln_b, ws, bs, n_seq, seq_len, rows, chunk):
    n_steps = seq_len // rows
    bias = jnp.repeat(bs[:, :chunk].T, MLP_GW, axis=1)
    return pl.pallas_call(
        functools.partial(_gmlp_kernel, rows=rows, chunk=chunk),
        grid=(n_seq, n_steps),
        in_specs=[_seq_spec(rows, 'u', n_steps), _seq_spec(rows, 'v', n_steps),
                  _resident((1, GROUP_W)), _resident((1, GROUP_W)),
                  _resident((MLP_GROUPS, chunk, chunk)), _resident((chunk, GROUP_W))],
        out_specs=[_out_spec(rows, GROUP_W, n_steps), _out_spec(rows, GROUP_W, n_steps)],
        out_shape=[jax.ShapeDtypeStruct((n_seq * seq_len, GROUP_W), F32)] * 2,
        compiler_params=_cparams("arbitrary", "arbitrary"),
        name="gmlp",
    )(p, p, ln_g.reshape(1, GROUP_W), ln_b.reshape(1, GROUP_W), ws[:, :chunk, :chunk], bias)


def _ssd_kernel(xbc_ref, z_ref, sm_ref, conv0_ref, h0_ref, cw_ref, cb_ref, lane_ref, g_ref,
                o_ref, hfin_ref, conv_out_ref, halo_ref, h_ref, *, rows, seq_rows, n_valid):
    chained = seq_rows == rows
    n_seqs = rows // seq_rows
    shift = seq_rows.bit_length() - 1
    if chained:
        @pl.when(pl.program_id(1) == 0)
        def _():
            h_ref[...] = h0_ref[0]

    act = _silu(_causal_conv(xbc_ref[...], halo_ref, conv0_ref, conv_out_ref, cw_ref, rows, seq_rows, n_valid)
                + cb_ref[...])
    sm = sm_ref[...]
    row = _iota2(rows, rows, 0)
    col = _iota2(rows, rows, 1)
    tril = (col <= row) & (lax.shift_right_logical(row, shift) == lax.shift_right_logical(col, shift))
    valid = (_iota2(rows, V7X_LANES, 0) & (seq_rows - 1)) < n_valid
    dt = jnp.where(valid, _softplus(sm + lane_ref[1:2, :]), 0.0)
    acum = _mm_exact_lhs(jnp.where(tril, 1.0, 0.0).astype(BF16), dt * -jnp.exp(lane_ref[0:1, :]))
    acum_t = acum.T
    heads = range(SSM_HEADS)
    seqs = range(n_seqs)
    rsl = [slice(i * seq_rows, (i + 1) * seq_rows) for i in seqs]
    group_of = [h // (SSM_HEADS // SSM_GROUPS) for h in heads]
    b_g = [act[:, GROUP_W + g * SSM_STATE:GROUP_W + (g + 1) * SSM_STATE] for g in range(SSM_GROUPS)]
    c_g = [act[:, GROUP_W + (SSM_GROUPS + g) * SSM_STATE:GROUP_W + (SSM_GROUPS + g + 1) * SSM_STATE]
           for g in range(SSM_GROUPS)]
    cb = [_mm(c_g[g], b_g[g], _NT) for g in range(SSM_GROUPS)]
    acol = [acum[:, LANE_DT + h:LANE_DT + h + 1] for h in heads]
    decay = [jnp.exp(jnp.where(tril, acol[h] - acum_t[LANE_DT + h:LANE_DT + h + 1, :], NEG_BIG)) for h in heads]
    x = [act[:, h * SSM_HEAD_DIM:(h + 1) * SSM_HEAD_DIM] for h in heads]
    xdt = [x[h] * dt[:, LANE_DT + h:LANE_DT + h + 1] for h in heads]
    c_dec = [c_g[group_of[h]] * jnp.exp(acol[h]) for h in heads]
    y_intra = [_mm(cb[group_of[h]] * decay[h], xdt[h]) for h in heads]
    state = {(h, i): (h_ref[h] if chained else h0_ref[i, h]) for h in heads for i in seqs}
    alast = {(h, i): acol[h][(i + 1) * seq_rows - 1:(i + 1) * seq_rows] for h in heads for i in seqs}
    y_inter = {(h, i): _mm(c_dec[h][rsl[i]], state[h, i], _NT) for h in heads for i in seqs}
    h_new = {(h, i): state[h, i] * jnp.exp(alast[h, i])
             + _mm(xdt[h][rsl[i]], b_g[group_of[h]][rsl[i]] * jnp.exp(alast[h, i] - acol[h][rsl[i]]), _TN)
             for h in heads for i in seqs}
    y = jnp.concatenate(
        [y_intra[h] + (y_inter[h, 0] if chained else jnp.concatenate([y_inter[h, i] for i in seqs], axis=0))
         + lane_ref[2:3, LANE_DT + h:LANE_DT + h + 1] * x[h] for h in heads], axis=1)
    new_states = jnp.stack([jnp.stack([h_new[h, i] for h in heads]) for i in seqs])
    hfin_ref[...] = new_states
    if chained:
        h_ref[...] = new_states[0]
    o_ref[...] = _rms(y * _silu(z_ref[...]), g_ref[...])


def _ssd(p, conv0, h0, conv_w, conv_b, a_log, dt_bias, d_skip, norm_g, n_seq, seq_len, rows, n_valid):
    chained = seq_len >= rows
    seq_rows = rows if chained else seq_len
    seqs_per_step = rows // seq_rows
    n_steps = seq_len // rows if chained else 1
    n_blocks = n_seq // seqs_per_step
    assert n_blocks * seqs_per_step == n_seq and seq_rows & (seq_rows - 1) == 0
    lane = jnp.zeros((3, V7X_LANES), F32)
    lane = (lane.at[0, LANE_DT:LANE_DT + SSM_HEADS].set(a_log)
            .at[1, LANE_DT:LANE_DT + SSM_HEADS].set(dt_bias)
            .at[2, LANE_DT:LANE_DT + SSM_HEADS].set(d_skip))
    return pl.pallas_call(
        functools.partial(_ssd_kernel, rows=rows, seq_rows=seq_rows, n_valid=n_valid),
        grid=(n_blocks, n_steps),
        in_specs=[_seq_spec(rows, 'xbc', n_steps), _seq_spec(rows, 'z_c', n_steps), _small_spec(rows, n_steps),
                  pl.BlockSpec((seqs_per_step, CONV_K - 1, SSM_CONV_W), lambda b, n: (b, 0, 0)),
                  pl.BlockSpec((seqs_per_step, SSM_HEADS, SSM_HEAD_DIM, SSM_STATE), lambda b, n: (b, 0, 0, 0)),
                  _resident((CONV_K, SSM_CONV_W)), _resident((1, SSM_CONV_W)), _resident((3, V7X_LANES)),
                  _resident((1, GROUP_W))],
        out_specs=[_out_spec(rows, GROUP_W, n_steps),
                   pl.BlockSpec((seqs_per_step, SSM_HEADS, SSM_HEAD_DIM, SSM_STATE), lambda b, n: (b, 0, 0, 0)),
                   pl.BlockSpec((seqs_per_step, CONV_K - 1, SSM_CONV_W), lambda b, n: (b, 0, 0))],
        out_shape=[jax.ShapeDtypeStruct((n_seq * seq_len, GROUP_W), F32),
                   jax.ShapeDtypeStruct((n_seq, SSM_HEADS, SSM_HEAD_DIM, SSM_STATE), F32),
                   jax.ShapeDtypeStruct((n_seq, CONV_K - 1, SSM_CONV_W), F32)],
        scratch_shapes=[pltpu.VMEM((seqs_per_step, V7X_SUBLANES + seq_rows, SSM_CONV_W), F32),
                        pltpu.VMEM((SSM_HEADS, SSM_HEAD_DIM, SSM_STATE), F32)],
        compiler_params=_cparams("arbitrary", "arbitrary"),
        name="ssd",
    )(p, p, p, conv0, h0, conv_w, conv_b.reshape(1, SSM_CONV_W), lane, norm_g.reshape(1, GROUP_W))


def _rope_lanes(x, cos, sin_signed):
    lane = _iota2(x.shape[0], V7X_LANES, 1)
    half = MLA_ROPE // 2
    swapped = jnp.where(lane % MLA_ROPE < half,
                        pltpu.roll(x, V7X_LANES - half, axis=1), pltpu.roll(x, half, axis=1))
    return x * cos + swapped * sin_signed


def _rope_tables(pos):
    half = MLA_ROPE // 2
    inv_freq = ROPE_THETA ** (-jnp.arange(half, dtype=F32) / half)
    ang = pos.astype(F32)[:, None] * inv_freq[None, :]
    cos, sin = jnp.cos(ang), jnp.sin(ang)
    reps = V7X_LANES // MLA_ROPE
    return jnp.tile(jnp.concatenate([cos, cos], axis=1), (1, reps)), jnp.tile(jnp.concatenate([-sin, sin], axis=1), (1, reps))


def _mla_project(cq_ref, ckv_ref, sm_ref, cos_ref, sin_ref, gq_ref, gkv_ref, wn_ref, wpe_ref, wuk_ref):
    c_q = _rms(cq_ref[...], gq_ref[...]).astype(BF16)
    c_kv = _rms(ckv_ref[...], gkv_ref[...])
    cos, sin = cos_ref[...], sin_ref[...]
    lane = _iota2(c_kv.shape[0], V7X_LANES, 1)
    k_pe = _rope_lanes(jnp.where(lane < MLA_ROPE, sm_ref[...], 0.0), cos, sin)
    q_nope = jnp.dot(c_q, wn_ref[...], preferred_element_type=F32)
    q_heads = []
    for h in range(MLA_HEADS):
        q_lat = _mm(q_nope[:, h * MLA_NOPE:(h + 1) * MLA_NOPE], wuk_ref[h], _NT)
        q_pe = _rope_lanes(jnp.dot(c_q, wpe_ref[h], preferred_element_type=F32), cos, sin)
        q_heads.append((jnp.concatenate([q_lat, q_pe], axis=1) * MLA_SCALE).astype(BF16))
    return c_kv, k_pe, q_heads


def _softmax_step(q, keys, m, l, acc, mask=None):
    s = lax.dot_general(q, keys, _NT, preferred_element_type=F32)
    if mask is not None:
        s = jnp.where(mask, s, NEG_BIG)
    m_new = jnp.maximum(m, jnp.max(s, axis=-1, keepdims=True))
    p = jnp.exp(s - m_new)
    alpha = jnp.exp(m - m_new)
    l = alpha * l + jnp.sum(p, axis=-1, keepdims=True)
    acc = alpha * acc + jnp.dot(p.astype(BF16), keys[:, :MLA_KV_RANK], preferred_element_type=F32)
    return m_new, l, acc


def _mla_prompt_kernel(cq_ref, ckv_ref, sm_ref, cos_ref, sin_ref, gq_ref, gkv_ref, wn_ref, wpe_ref, wuk_ref,
                       wuv_ref, o_ref, ckv_out_ref, kpe_out_ref, keys_ref, vt_ref, q_ref, m_ref, acc_ref,
                       *, rows, kv_block):
    n = pl.program_id(1)
    qrows = MLA_HEADS * rows
    c_kv, k_pe, q_heads = _mla_project(cq_ref, ckv_ref, sm_ref, cos_ref, sin_ref, gq_ref, gkv_ref,
                                       wn_ref, wpe_ref, wuk_ref)
    ckv_out_ref[0] = c_kv
    kpe_out_ref[0] = k_pe[:, :MLA_ROPE]
    base = pl.multiple_of(n * rows, rows)
    keys_ref[pl.ds(base, rows), :] = jnp.concatenate([c_kv, k_pe], axis=1).astype(BF16)
    vt_ref[0:MLA_KV_RANK, pl.ds(base, rows)] = c_kv.T.astype(BF16)
    vt_ref[MLA_KV_RANK:MLA_VT_ROWS, pl.ds(base, rows)] = jnp.ones((MLA_VT_ROWS - MLA_KV_RANK, rows), BF16)
    q_ref[...] = jnp.concatenate(q_heads, axis=0)
    m_ref[...] = jnp.full((1, qrows), NEG_BIG, F32)
    acc_ref[...] = jnp.zeros((MLA_VT_ROWS, qrows), F32)

    def attend(off, mask):
        s = lax.dot_general(keys_ref[pl.ds(off, kv_block), :], q_ref[...], _NT, preferred_element_type=F32)
        if mask is not None:
            s = jnp.where(mask, s, NEG_BIG)
        m_old = m_ref[...]
        m_new = jnp.maximum(m_old, jnp.max(s, axis=0, keepdims=True))
        p = jnp.exp((s - m_new).astype(BF16))
        alpha = jnp.exp(m_old - m_new)
        acc_ref[...] = alpha * acc_ref[...] + jnp.dot(vt_ref[:, pl.ds(off, kv_block)], p,
                                                      preferred_element_type=F32)
        m_ref[...] = m_new

    def body(j, carry):
        attend(pl.multiple_of(j * kv_block, kv_block), None)
        return carry

    blocks_per_step = rows // kv_block
    lax.fori_loop(0, n * blocks_per_step, body, 0)
    q_pos = _iota2(kv_block, qrows, 1) & (rows - 1)
    for d in range(blocks_per_step):
        attend(base + d * kv_block, d * kv_block + _iota2(kv_block, qrows, 0) <= q_pos)
    o_lat_t = acc_ref[0:MLA_KV_RANK, :] / acc_ref[MLA_KV_RANK:MLA_KV_RANK + 1, :]
    outs = [_mm(o_lat_t[:, h * rows:(h + 1) * rows], wuv_ref[h], _TN) for h in range(MLA_HEADS)]
    o_ref[...] = jnp.concatenate(outs, axis=1)


def _mla_weights(w_uq, w_uk, w_uv):
    w4 = w_uq.reshape(MLA_Q_RANK, MLA_HEADS, MLA_NOPE + MLA_ROPE)
    w_nope = w4[:, :, :MLA_NOPE].reshape(MLA_Q_RANK, MLA_HEADS * MLA_NOPE).astype(BF16)
    w_pe = jnp.pad(w4[:, :, MLA_NOPE:].transpose(1, 0, 2),
                   ((0, 0), (0, 0), (0, V7X_LANES - MLA_ROPE))).astype(BF16)
    return w_nope, w_pe, w_uk.astype(BF16), w_uv.astype(BF16)


def _mla_prompt(p, q_norm_g, kv_norm_g, w_nope, w_pe, w_uk, w_uv, n_seq, seq_len, rows, kv_block):
    n_steps = seq_len // rows
    assert seq_len % rows == 0 and rows % kv_block == 0 and rows & (rows - 1) == 0
    cos, sin = _rope_tables(jnp.arange(seq_len))
    tab = pl.BlockSpec((rows, V7X_LANES), lambda b, n: (n, 0))
    return pl.pallas_call(
        functools.partial(_mla_prompt_kernel, rows=rows, kv_block=kv_block),
        grid=(n_seq, n_steps),
        in_specs=[_seq_spec(rows, 'c_q', n_steps), _seq_spec(rows, 'c_kv', n_steps), _small_spec(rows, n_steps),
                  tab, tab, _resident((1, MLA_Q_RANK)), _resident((1, MLA_KV_RANK)),
                  _resident(w_nope.shape), _resident(w_pe.shape), _resident(w_uk.shape), _resident(w_uv.shape)],
        out_specs=[_out_spec(rows, GROUP_W, n_steps),
                   pl.BlockSpec((1, rows, MLA_KV_RANK), lambda b, n: (b, n, 0)),
                   pl.BlockSpec((1, rows, MLA_ROPE), lambda b, n: (b, n, 0))],
        out_shape=[jax.ShapeDtypeStruct((n_seq * seq_len, GROUP_W), F32),
                   jax.ShapeDtypeStruct((n_seq, seq_len, MLA_KV_RANK), F32),
                   jax.ShapeDtypeStruct((n_seq, seq_len, MLA_ROPE), F32)],
        scratch_shapes=[pltpu.VMEM((seq_len, 2 * V7X_LANES), BF16),
                        pltpu.VMEM((MLA_VT_ROWS, seq_len), BF16),
                        pltpu.VMEM((MLA_HEADS * rows, 2 * V7X_LANES), BF16),
                        pltpu.VMEM((1, MLA_HEADS * rows), F32),
                        pltpu.VMEM((MLA_VT_ROWS, MLA_HEADS * rows), F32)],
        compiler_params=_cparams("arbitrary", "arbitrary"),
        name="mla_prompt",
    )(p, p, p, cos, sin, q_norm_g.reshape(1, MLA_Q_RANK), kv_norm_g.reshape(1, MLA_KV_RANK),
      w_nope, w_pe, w_uk, w_uv)


def _mla_decode_kernel(pt_ref, cq_ref, ckv_ref, sm_ref, cos_ref, sin_ref, gq_ref, gkv_ref, wn_ref, wpe_ref,
                       wuk_ref, wuv_ref, ckv_hbm, kpe_hbm, o_ref, ckv_out_ref, kpe_out_ref,
                       lat_buf, pe_buf, sems, s_ref, latb_ref, *, rows, layer, n_pages, pages_per_chunk):
    seq = pl.program_id(0)
    slot = lax.rem(seq, 2)
    qrows = MLA_HEADS * rows
    chunk = pages_per_chunk * PAGE_SIZE
    n_chunks = n_pages // pages_per_chunk

    def page_copies(src_seq, dst_slot, g):
        page = pt_ref[src_seq, g]
        return (pltpu.make_async_copy(ckv_hbm.at[layer, page], lat_buf.at[dst_slot, g], sems.at[0, dst_slot]),
                pltpu.make_async_copy(kpe_hbm.at[layer, page],
                                      pe_buf.at[dst_slot, :, pl.ds(pl.multiple_of(g * PAGE_SIZE, PAGE_SIZE), PAGE_SIZE)],
                                      sems.at[1, dst_slot]))

    def fetch(src_seq, dst_slot):
        def body(g, carry):
            for cp in page_copies(src_seq, dst_slot, g):
                cp.start()
            return carry
        lax.fori_loop(0, n_pages, body, 0, unroll=8)

    @pl.when(seq == 0)
    def _():
        fetch(0, 0)

    @pl.when(seq + 1 < pl.num_programs(0))
    def _():
        fetch(seq + 1, 1 - slot)

    c_kv, k_pe, q_heads = _mla_project(cq_ref, ckv_ref, sm_ref, cos_ref, sin_ref, gq_ref, gkv_ref,
                                       wn_ref, wpe_ref, wuk_ref)
    ckv_out_ref[0] = c_kv
    kpe_out_ref[0] = k_pe[:, :MLA_ROPE]
    knew = jnp.concatenate([c_kv, k_pe], axis=1).astype(BF16)
    q = jnp.concatenate(q_heads, axis=0)
    q_lat, q_pe = q[:, :MLA_KV_RANK], q[:, MLA_KV_RANK:MLA_KV_RANK + MLA_ROPE]
    causal = _iota2(qrows, rows, 1) <= (_iota2(qrows, rows, 0) & (rows - 1))
    s_new = jnp.where(causal, lax.dot_general(q, knew, _NT, preferred_element_type=F32), NEG_BIG)

    pltpu.make_async_copy(lat_buf.at[slot], lat_buf.at[slot], sems.at[0, slot]).wait()
    pltpu.make_async_copy(pe_buf.at[slot], pe_buf.at[slot], sems.at[1, slot]).wait()

    def to_bf16(g, carry):
        latb_ref[pl.ds(pl.multiple_of(g * PAGE_SIZE, PAGE_SIZE), PAGE_SIZE), :] = lat_buf[slot, g].astype(BF16)
        return carry
    lax.fori_loop(0, n_pages, to_bf16, 0, unroll=8)

    m = jnp.max(s_new, axis=-1, keepdims=True)
    for c in range(n_chunks):
        s_lat = lax.dot_general(q_lat, latb_ref[c * chunk:(c + 1) * chunk, :], _NT, preferred_element_type=F32)
        s_pe = jnp.dot(q_pe, pe_buf[slot, :, c * chunk:(c + 1) * chunk].astype(BF16), preferred_element_type=F32)
        s = s_lat + s_pe
        s_ref[:, c * chunk:(c + 1) * chunk] = s
        m = jnp.maximum(m, jnp.max(s, axis=-1, keepdims=True))

    p_new = jnp.exp(s_new - m)
    l = jnp.sum(p_new, axis=-1, keepdims=True)
    acc = jnp.dot(p_new.astype(BF16), knew[:, :MLA_KV_RANK], preferred_element_type=F32)
    for c in range(n_chunks):
        pr = jnp.exp(s_ref[:, c * chunk:(c + 1) * chunk] - m)
        l = l + jnp.sum(pr, axis=-1, keepdims=True)
        acc = acc + jnp.dot(pr.astype(BF16), latb_ref[c * chunk:(c + 1) * chunk, :], preferred_element_type=F32)
    o_lat = acc / l
    outs = [_mm(o_lat[h * rows:(h + 1) * rows], wuv_ref[h]) for h in range(MLA_HEADS)]
    o_ref[...] = jnp.concatenate(outs, axis=1)


def _mla_decode(p, cache_ckv, cache_kpe, layer, page_table, q_norm_g, kv_norm_g, w_nope, w_pe, w_uk, w_uv,
                n_seq, rows, pages_per_chunk=16):
    n_pages = page_table.shape[1]
    pages_per_chunk = min(pages_per_chunk, n_pages)
    assert n_pages % pages_per_chunk == 0
    past_len = n_pages * PAGE_SIZE
    cos, sin = _rope_tables(past_len + jnp.arange(rows))

    def const(shape):
        return pl.BlockSpec(shape, lambda s, pt: (0,) * len(shape))

    def seq(name):
        lo, hi = _DST_COLS[name]
        return pl.BlockSpec((rows, hi - lo), lambda s, pt, c=_col_block(name): (s, c))

    in_specs = [seq('c_q'), seq('c_kv'), pl.BlockSpec((rows, V7X_LANES), lambda s, pt: (s, SMALL_BLOCK)),
                const((rows, V7X_LANES)), const((rows, V7X_LANES)),
                const((1, MLA_Q_RANK)), const((1, MLA_KV_RANK)),
                const(w_nope.shape), const(w_pe.shape), const(w_uk.shape), const(w_uv.shape),
                pl.BlockSpec(memory_space=pl.ANY), pl.BlockSpec(memory_space=pl.ANY)]
    qrows = MLA_HEADS * rows
    return pl.pallas_call(
        functools.partial(_mla_decode_kernel, rows=rows, layer=layer, n_pages=n_pages,
                          pages_per_chunk=pages_per_chunk),
        grid_spec=pltpu.PrefetchScalarGridSpec(
            num_scalar_prefetch=1,
            grid=(n_seq,),
            in_specs=in_specs,
            out_specs=[pl.BlockSpec((rows, GROUP_W), lambda s, pt: (s, 0)),
                       pl.BlockSpec((1, rows, MLA_KV_RANK), lambda s, pt: (s, 0, 0)),
                       pl.BlockSpec((1, rows, MLA_ROPE), lambda s, pt: (s, 0, 0))],
            scratch_shapes=[pltpu.VMEM((2, n_pages, PAGE_SIZE, MLA_KV_RANK), F32),
                            pltpu.VMEM((2, MLA_ROPE, past_len), F32),
                            pltpu.SemaphoreType.DMA((2, 2)),
                            pltpu.VMEM((qrows, past_len), F32),
                            pltpu.VMEM((past_len, MLA_KV_RANK), BF16)]),
        out_shape=[jax.ShapeDtypeStruct((n_seq * rows, GROUP_W), F32),
                   jax.ShapeDtypeStruct((n_seq, rows, MLA_KV_RANK), F32),
                   jax.ShapeDtypeStruct((n_seq, rows, MLA_ROPE), F32)],
        compiler_params=_cparams("arbitrary"),
        name="mla_decode",
    )(page_table, p, p, p, cos, sin, q_norm_g.reshape(1, MLA_Q_RANK), kv_norm_g.reshape(1, MLA_KV_RANK),
      w_nope, w_pe, w_uk, w_uv, cache_ckv, cache_kpe)


def _trunk_layer(x, w, layer, n_seq, seq_len, rows, n_valid, gdn_conv, gdn_s, ssm_conv, ssm_h, paged):
    g = w['norm_g']
    assert n_valid == seq_len or seq_len == rows
    step_valid = min(n_valid, rows)
    x = _ffn(x, g[0:1], g[1:2], w['ffn_w_in'][0], w['ffn_w_out'][0])
    p = _proj(x, g[2:3], w['w_in'])
    if seq_len == rows:
        gdn_shape = (rows * SAMPLE_SEQS_PER_STEP, rows, n_valid)
        mlp_shape = (1, n_seq * seq_len, min(SEQ_TILE, n_seq * seq_len), rows)
        ssd_rows = rows * SAMPLE_SEQS_PER_STEP
    else:
        gdn_shape = (rows, GDN_CHUNK, GDN_CHUNK)
        mlp_shape = (n_seq, seq_len, rows, MLP_CHUNK)
        ssd_rows = rows
    out_a, gdn_s_new, gdn_conv_new = _gdn(p, gdn_conv, gdn_s, w['gdn_conv_w'], w['gdn_a_log'], w['gdn_dt_bias'],
                            w['gdn_norm_g'], n_seq, seq_len, *gdn_shape)
    out_b, v_b = _gmlp(p, w['mlp_ln_g'], w['mlp_ln_b'], w['mlp_ws'], w['mlp_bs'], *mlp_shape)
    out_c, ssm_h_new, ssm_conv_new = _ssd(p, ssm_conv, ssm_h, w['ssm_conv_w'], w['ssm_conv_b'], w['ssm_a_log'],
                            w['ssm_dt_bias'], w['ssm_d'], w['ssm_norm_g'], n_seq, seq_len, ssd_rows, step_valid)
    mla_w = (w['mla_q_norm_g'], w['mla_kv_norm_g'], w['mla_w_nope'], w['mla_w_pe'], w['mla_w_uk'], w['mla_w_uv'])
    if paged is None:
        out_d, c_kv, k_pe = _mla_prompt(p, *mla_w, n_seq, seq_len, MLA_Q_TILE, MLA_KV_BLOCK)
    else:
        cache_ckv, cache_kpe, page_table = paged
        out_d, c_kv, k_pe = _mla_decode(p, cache_ckv, cache_kpe, layer, page_table, *mla_w, n_seq, rows)
    x = _ffn(x, g[4:5], g[5:6], w['ffn_w_in'][1], w['ffn_w_out'][1],
             mix=(out_a, out_b, out_c, out_d), w_o=w['w_out'], g_mix=g[3:4])
    v_b = v_b.reshape(n_seq, seq_len, GROUP_W)[:, :n_valid]
    return x, (c_kv[:, :n_valid], k_pe[:, :n_valid], gdn_s_new, gdn_conv_new, ssm_h_new, ssm_conv_new, v_b)


def kernel(x_prompt, x_sample, cache_ckv, cache_kpe, page_table, state_gdn_s, state_gdn_conv,
           state_ssm_h, state_ssm_conv, norm_g, ffn_w_in, ffn_w_out, w_in, w_out,
           gdn_conv_w, gdn_a_log, gdn_dt_bias, gdn_norm_g, mlp_ln_g, mlp_ln_b, mlp_ws, mlp_bs,
           ssm_conv_w, ssm_conv_b, ssm_a_log, ssm_dt_bias, ssm_d, ssm_norm_g,
           mla_q_norm_g, mla_w_uq, mla_kv_norm_g, mla_w_uk, mla_w_uv):
    bp, lp, _ = x_prompt.shape
    bs, ls, _ = x_sample.shape
    assert lp % SEQ_TILE == 0 and CONV_K - 1 <= ls <= SAMPLE_ROWS
    w_in_p = jnp.concatenate(
        [w_in[:, :, _SRC_COLS[name][0]:_SRC_COLS[name][1]] for name in _DST_ORDER]
        + [jnp.zeros(w_in.shape[:2] + (PROJ_COLS - PROJ_USED,), w_in.dtype)], axis=2)
    cache_kpe_t = jnp.swapaxes(cache_kpe, 2, 3)
    weights = dict(norm_g=norm_g, ffn_w_in=ffn_w_in.astype(BF16), ffn_w_out=ffn_w_out.astype(BF16),
                   w_in=w_in_p.astype(BF16), w_out=w_out.astype(BF16),
                   gdn_conv_w=gdn_conv_w, gdn_a_log=gdn_a_log, gdn_dt_bias=gdn_dt_bias, gdn_norm_g=gdn_norm_g,
                   mlp_ln_g=mlp_ln_g, mlp_ln_b=mlp_ln_b, mlp_ws=mlp_ws, mlp_bs=mlp_bs,
                   ssm_conv_w=ssm_conv_w, ssm_conv_b=ssm_conv_b, ssm_a_log=ssm_a_log,
                   ssm_dt_bias=ssm_dt_bias, ssm_d=ssm_d, ssm_norm_g=ssm_norm_g,
                   mla_q_norm_g=mla_q_norm_g, mla_kv_norm_g=mla_kv_norm_g)
    xp = x_prompt.reshape(bp * lp, D_MODEL)
    xs = jnp.pad(x_sample, ((0, 0), (0, SAMPLE_ROWS - ls), (0, 0))).reshape(bs * SAMPLE_ROWS, D_MODEL)
    zeros_p = (jnp.zeros((bp, CONV_K - 1, 3 * GROUP_W), F32), jnp.zeros((bp, GDN_HEADS, GDN_DK, GDN_DV), F32),
               jnp.zeros((bp, CONV_K - 1, SSM_CONV_W), F32), jnp.zeros((bp, SSM_HEADS, SSM_HEAD_DIM, SSM_STATE), F32))
    st_p, st_s = [], []
    for l in range(DEPTH):
        wl = {name: arr[l] for name, arr in weights.items()}
        wl['mla_w_nope'], wl['mla_w_pe'], wl['mla_w_uk'], wl['mla_w_uv'] = _mla_weights(
            mla_w_uq[l], mla_w_uk[l], mla_w_uv[l])
        xp, st = _trunk_layer(xp, wl, l, bp, lp, SEQ_TILE, lp, *zeros_p, None)
        st_p.append(st)
        xs, st = _trunk_layer(xs, wl, l, bs, SAMPLE_ROWS, SAMPLE_ROWS, ls,
                              state_gdn_conv[l], state_gdn_s[l], state_ssm_conv[l], state_ssm_h[l],
                              (cache_ckv, cache_kpe_t, page_table))
        st_s.append(st)

    def stack(states, i):
        return jnp.stack([s[i] for s in states])

    y_prompt = xp.reshape(bp, lp, D_MODEL)
    y_sample = xs.reshape(bs, SAMPLE_ROWS, D_MODEL)[:, :ls]
    return (y_prompt, y_sample,
            stack(st_p, 0), stack(st_p, 1), stack(st_p, 2), stack(st_p, 3), stack(st_p, 4), stack(st_p, 5),
            stack(st_s, 0), stack(st_s, 1), stack(st_s, 2), stack(st_s, 3), stack(st_s, 4), stack(st_s, 5),
            stack(st_s, 6))
```

```python
import functools

import numpy as np
import jax
import jax.numpy as jnp
from jax import lax
from jax.experimental import pallas as pl
from jax.experimental.pallas import tpu as pltpu

F32 = jnp.float32
BF16 = jnp.bfloat16

D_MODEL = 1024
DEPTH = 2
PAGE_SIZE = 128
GROUP_W = 256
CONV_K = 4
FFN_DIM = 2816
EPS = 1e-6

GDN_HEADS = 4
GDN_DK = 64
GDN_DV = 64
MLP_CHUNK = 128
MLP_GROUPS = 4
MLP_GW = 64
SSM_HEADS = 4
SSM_HEAD_DIM = 64
SSM_GROUPS = 2
SSM_STATE = 128
SSM_CONV_W = 768
MLA_HEADS = 4
MLA_NOPE = 64
MLA_ROPE = 32
MLA_V_DIM = 64
MLA_Q_RANK = 256
MLA_KV_RANK = 128
MLA_SCALE = (MLA_NOPE + MLA_ROPE) ** -0.5
ROPE_THETA = 10000.0

V7X_LANES = 128
V7X_SUBLANES = 8
V7X_MXU_DIM = 256
V7X_VMEM_LIMIT_BYTES = 56 * 1024 * 1024

_SRC_COLS = dict(
    qkv=(0, 768), z_a=(768, 1024), a_a=(1024, 1028), b_a=(1028, 1032),
    u=(1032, 1288), v=(1288, 1544),
    z_c=(1544, 1800), xbc=(1800, 2568), dt=(2568, 2572),
    c_q=(2572, 2828), c_kv=(2828, 2956), k_pe=(2956, 2988))
_DST_ORDER = ('qkv', 'xbc', 'z_a', 'u', 'v', 'z_c', 'c_q', 'c_kv', 'k_pe', 'a_a', 'b_a', 'dt')
_DST_COLS = {}
_off = 0
for _name in _DST_ORDER:
    _w = _SRC_COLS[_name][1] - _SRC_COLS[_name][0]
    _DST_COLS[_name] = (_off, _off + _w)
    _off += _w
PROJ_USED = _off
PROJ_COLS = -(-PROJ_USED // V7X_LANES) * V7X_LANES
_PERM = np.concatenate([np.arange(*_SRC_COLS[n]) for n in _DST_ORDER])
SMALL_BLOCK = _DST_COLS['k_pe'][0] // V7X_LANES
LANE_A = _DST_COLS['a_a'][0] % V7X_LANES
LANE_B = _DST_COLS['b_a'][0] % V7X_LANES
LANE_DT = _DST_COLS['dt'][0] % V7X_LANES


def _col_block(name):
    lo, hi = _DST_COLS[name]
    assert lo % (hi - lo) == 0
    return lo // (hi - lo)


FFN_CHUNK = V7X_MXU_DIM
TOKEN_TILE = 512
SEQ_TILE = 256
MLA_Q_TILE = 512
MLA_KV_BLOCK = 512
SAMPLE_ROWS = V7X_SUBLANES
SAMPLE_SEQS_PER_STEP = 8
GDN_CHUNK = 64
GDN_HEAD_GROUP = 4
MLA_VT_ROWS = MLA_KV_RANK + 16
NEG_BIG = -1e30


def _cparams(*sem):
    return pltpu.CompilerParams(dimension_semantics=sem, vmem_limit_bytes=V7X_VMEM_LIMIT_BYTES)


def _rms(x, g):
    return x * lax.rsqrt(jnp.mean(x * x, axis=-1, keepdims=True) + EPS) * g


def _silu(x):
    return x * jax.nn.sigmoid(x)


def _softplus(x):
    return jnp.maximum(x, 0.0) + jnp.log(1.0 + jnp.exp(-jnp.abs(x)))


_NN = (((1,), (0,)), ((), ()))
_NT = (((1,), (1,)), ((), ()))
_TN = (((0,), (0,)), ((), ()))


def _mm(a, b, dims=_NN):
    return lax.dot_general(a.astype(BF16), b.astype(BF16), dims, preferred_element_type=F32)


def _split3(x):
    x0 = x.astype(BF16)
    r1 = x - x0.astype(F32)
    x1 = r1.astype(BF16)
    x2 = (r1 - x1.astype(F32)).astype(BF16)
    return x0, x1, x2


def _mm_exact_lhs(a_bf16, b):
    b0, b1, b2 = _split3(b)
    dot = lambda y: lax.dot_general(a_bf16, y, _NN, preferred_element_type=F32)
    return dot(b0) + (dot(b1) + dot(b2))


def _iota2(n, m, axis):
    return lax.broadcasted_iota(jnp.int32, (n, m), axis)


def _resident(shape):
    return pl.BlockSpec(shape, lambda *_: (0,) * len(shape), pipeline_mode=pl.Buffered(1))


def _ffn_body(x, gpre_ref, gpost_ref, win_ref, wout_ref, o_ref, acc_ref):
    xn = _rms(x, gpre_ref[...]).astype(BF16)
    n_chunks = FFN_DIM // FFN_CHUNK
    for c in range(n_chunks):
        lo = c * FFN_CHUNK
        gate = jnp.dot(xn, win_ref[:, lo:lo + FFN_CHUNK], preferred_element_type=F32)
        up = jnp.dot(xn, win_ref[:, FFN_DIM + lo:FFN_DIM + lo + FFN_CHUNK], preferred_element_type=F32)
        h = (_silu(gate) * up).astype(BF16)
        part = jnp.dot(h, wout_ref[lo:lo + FFN_CHUNK, :], preferred_element_type=F32)
        if c == 0:
            acc_ref[...] = part
        else:
            acc_ref[...] += part
    o_ref[...] = x + 0.5 * _rms(acc_ref[...], gpost_ref[...])


def _ffn_kernel(x_ref, gpre_ref, gpost_ref, win_ref, wout_ref, o_ref, acc_ref):
    _ffn_body(x_ref[...], gpre_ref, gpost_ref, win_ref, wout_ref, o_ref, acc_ref)


def _mix_ffn_kernel(x_ref, ma_ref, mb_ref, mc_ref, md_ref, wo_ref, gmix_ref, gpre_ref, gpost_ref,
                    win_ref, wout_ref, o_ref, acc_ref):
    y = None
    for i, m_ref in enumerate((ma_ref, mb_ref, mc_ref, md_ref)):
        part = jnp.dot(m_ref[...].astype(BF16), wo_ref[i * GROUP_W:(i + 1) * GROUP_W, :],
                       preferred_element_type=F32)
        y = part if y is None else y + part
    x = x_ref[...] + _rms(y, gmix_ref[...])
    _ffn_body(x, gpre_ref, gpost_ref, win_ref, wout_ref, o_ref, acc_ref)


def _ffn(x, g_pre, g_post, w_in, w_out, mix=None, w_o=None, g_mix=None):
    t = x.shape[0]
    tm = min(TOKEN_TILE, t)
    assert t % tm == 0
    tok = pl.BlockSpec((tm, D_MODEL), lambda i: (i, 0))
    vec = _resident((1, D_MODEL))
    w_specs = [_resident((D_MODEL, 2 * FFN_DIM)), _resident((FFN_DIM, D_MODEL))]
    if mix is None:
        kern, args = _ffn_kernel, (x, g_pre, g_post, w_in, w_out)
        in_specs = [tok, vec, vec] + w_specs
    else:
        kern, args = _mix_ffn_kernel, (x, *mix, w_o, g_mix, g_pre, g_post, w_in, w_out)
        part = pl.BlockSpec((tm, GROUP_W), lambda i: (i, 0))
        in_specs = [tok, part, part, part, part, _resident((D_MODEL, D_MODEL)), vec, vec, vec] + w_specs
    return pl.pallas_call(
        kern,
        grid=(t // tm,),
        in_specs=in_specs,
        out_specs=tok,
        out_shape=jax.ShapeDtypeStruct((t, D_MODEL), F32),
        scratch_shapes=[pltpu.VMEM((tm, D_MODEL), F32)],
        compiler_params=_cparams("arbitrary"),
        name="mix_ffn" if mix is not None else "ffn",
    )(*args)


def _proj_kernel(x_ref, g_ref, w_ref, o_ref):
    xn = _rms(x_ref[...], g_ref[...]).astype(BF16)
    o_ref[...] = jnp.dot(xn, w_ref[...], preferred_element_type=F32)


def _proj(x, g, w):
    t = x.shape[0]
    tm = min(TOKEN_TILE, t)
    assert t % tm == 0
    return pl.pallas_call(
        _proj_kernel,
        grid=(t // tm,),
        in_specs=[pl.BlockSpec((tm, D_MODEL), lambda i: (i, 0)),
                  _resident((1, D_MODEL)), _resident((D_MODEL, PROJ_COLS))],
        out_specs=pl.BlockSpec((tm, PROJ_COLS), lambda i: (i, 0)),
        out_shape=jax.ShapeDtypeStruct((t, PROJ_COLS), F32),
        compiler_params=_cparams("arbitrary"),
        name="in_proj",
    )(x, g, w)


def _seq_spec(rows, name, n_steps):
    lo, hi = _DST_COLS[name]
    return pl.BlockSpec((rows, hi - lo), lambda b, n, c=_col_block(name): (b * n_steps + n, c))


def _small_spec(rows, n_steps):
    return pl.BlockSpec((rows, V7X_LANES), lambda b, n: (b * n_steps + n, SMALL_BLOCK))


def _out_spec(rows, width, n_steps):
    return pl.BlockSpec((rows, width), lambda b, n: (b * n_steps + n, 0))


def _causal_conv(x, halo_ref, conv0_ref, conv_out_ref, w_ref, rows, seq_rows, n_valid):
    first = V7X_SUBLANES - (CONV_K - 1)

    def taps(i):
        y = w_ref[0:1, :] * halo_ref[i, first:first + seq_rows, :]
        for j in range(1, CONV_K):
            y = y + w_ref[j:j + 1, :] * halo_ref[i, first + j:first + j + seq_rows, :]
        return y

    if seq_rows == rows:
        @pl.when(pl.program_id(1) == 0)
        def _():
            halo_ref[0, first:V7X_SUBLANES, :] = conv0_ref[0]

        halo_ref[0, V7X_SUBLANES:V7X_SUBLANES + rows, :] = x
        y = taps(0)
        halo_ref[0, 0:V7X_SUBLANES, :] = x[rows - V7X_SUBLANES:rows, :]
        conv_out_ref[0] = x[rows - (CONV_K - 1):rows, :]
        return y
    parts = []
    for i in range(rows // seq_rows):
        halo_ref[i, first:V7X_SUBLANES, :] = conv0_ref[i]
        halo_ref[i, V7X_SUBLANES:V7X_SUBLANES + seq_rows, :] = x[i * seq_rows:(i + 1) * seq_rows, :]
        parts.append(taps(i))
        conv_out_ref[i] = x[i * seq_rows + n_valid - (CONV_K - 1):i * seq_rows + n_valid, :]
    return jnp.concatenate(parts, axis=0)


def _level_masks(rows):
    i = np.arange(rows)[:, None]
    j = np.arange(rows)[None, :]
    out = []
    s = 1
    while s < rows:
        out.append(((i // (2 * s) == j // (2 * s)) & (i % (2 * s) >= s) & (j % (2 * s) < s)).astype(np.float32))
        s *= 2
    return np.stack(out)


def _unit_lower_inverse(a_strict, lv_ref, eye):
    t = eye - a_strict * lv_ref[0]
    for lv in range(1, lv_ref.shape[0]):
        a_s = a_strict * lv_ref[lv]
        t = t - _mm(_mm(t, a_s), t)
    return t


def _gdn_kernel(qkv_ref, z_ref, sm_ref, conv0_ref, s0_ref, cw_ref, lane_ref, g_ref, lv_ref,
                o_ref, sfin_ref, conv_out_ref, halo_ref, s_ref, *, rows, chunk, n_valid, chained):
    n_chunks = rows // chunk
    shift = chunk.bit_length() - 1
    if chained:
        @pl.when(pl.program_id(1) == 0)
        def _():
            s_ref[...] = s0_ref[0]

    act = _silu(_causal_conv(qkv_ref[...], halo_ref, conv0_ref, conv_out_ref, cw_ref, rows,
                             rows if chained else chunk, n_valid))
    sm = sm_ref[...]
    row = _iota2(rows, rows, 0)
    col = _iota2(rows, rows, 1)
    same_chunk = lax.shift_right_logical(row, shift) == lax.shift_right_logical(col, shift)
    tril_blocks = jnp.where(same_chunk, (col <= row).astype(F32), 0.0).astype(BF16)
    valid = (_iota2(rows, V7X_LANES, 0) & (chunk - 1)) < n_valid
    log_alpha = jnp.where(valid, -jnp.exp(lane_ref[0:1, :]) * _softplus(sm + lane_ref[1:2, :]), 0.0)
    beta = jnp.where(valid, jax.nn.sigmoid(sm), 0.0)
    gcum = _mm_exact_lhs(tril_blocks, log_alpha)
    gcum_t = gcum.T
    rc = _iota2(chunk, chunk, 0)
    cc = _iota2(chunk, chunk, 1)
    tril = cc <= rc
    strict = cc < rc
    eye = (cc == rc).astype(F32)
    z = z_ref[...]
    chunks = range(n_chunks)
    rsl = [slice(c * chunk, (c + 1) * chunk) for c in chunks]
    outs = {}
    finals = [[None] * GDN_HEADS for _ in range(1 if chained else n_chunks)]
    for first in range(0, GDN_HEADS, GDN_HEAD_GROUP):
        heads = range(first, first + GDN_HEAD_GROUP)
        chains = [(h, c) for h in heads for c in chunks]
        q, k, kb, rhs, qd, gcol = {}, {}, {}, {}, {}, {}
        for h in heads:
            lane = LANE_A + h
            qh = act[:, h * GDN_DK:(h + 1) * GDN_DK]
            kh = act[:, GROUP_W + h * GDN_DK:GROUP_W + (h + 1) * GDN_DK]
            vh = act[:, 2 * GROUP_W + h * GDN_DV:2 * GROUP_W + (h + 1) * GDN_DV]
            qh = qh * lax.rsqrt(jnp.sum(qh * qh, axis=-1, keepdims=True) + EPS) * GDN_DK ** -0.5
            kh = kh * lax.rsqrt(jnp.sum(kh * kh, axis=-1, keepdims=True) + EPS)
            g_h = gcum[:, lane:lane + 1]
            b_h = beta[:, LANE_B + h:LANE_B + h + 1]
            e_h = jnp.exp(g_h)
            kb_h = kh * b_h
            rhs_h = jnp.concatenate([kb_h * e_h, vh * b_h], axis=1)
            qd_h = qh * e_h
            for c in chunks:
                q[h, c], k[h, c], kb[h, c] = qh[rsl[c]], kh[rsl[c]], kb_h[rsl[c]]
                rhs[h, c], qd[h, c], gcol[h, c] = rhs_h[rsl[c]], qd_h[rsl[c]], g_h[rsl[c]]
        decay = {(h, c): jnp.exp(jnp.where(tril, gcol[h, c] - gcum_t[LANE_A + h:LANE_A + h + 1, rsl[c]], NEG_BIG))
                 for h, c in chains}
        kk = {i: _mm(kb[i], k[i], _NT) for i in chains}
        qk = {i: _mm(q[i], k[i], _NT) * decay[i] for i in chains}
        a_strict = {i: jnp.where(strict, kk[i] * decay[i], 0.0) for i in chains}
        t_inv = {i: eye - a_strict[i] * lv_ref[0] for i in chains}
        for lv in range(1, lv_ref.shape[0]):
            half = {i: _mm(t_inv[i], a_strict[i] * lv_ref[lv]) for i in chains}
            t_inv = {i: t_inv[i] - _mm(half[i], t_inv[i]) for i in chains}
        sol = {i: _mm(t_inv[i], rhs[i]) for i in chains}
        glast = {i: gcol[i][chunk - 1:chunk] for i in chains}
        k_dec = {i: k[i] * jnp.exp(glast[i] - gcol[i]) for i in chains}
        state = {h: s_ref[h] for h in heads} if chained else None
        for c in chunks:
            cur = state if chained else {h: s0_ref[c, h] for h in heads}
            w_s = {h: _mm(sol[h, c][:, :GDN_DK], cur[h]) for h in heads}
            o_s = {h: _mm(qd[h, c], cur[h]) for h in heads}
            u_new = {h: sol[h, c][:, GDN_DK:] - w_s[h] for h in heads}
            for h in heads:
                outs[h, c] = o_s[h] + _mm(qk[h, c], u_new[h])
            nxt = {h: cur[h] * jnp.exp(glast[h, c]) + _mm(k_dec[h, c], u_new[h], _TN) for h in heads}
            if chained:
                state = nxt
            else:
                for h in heads:
                    finals[c][h] = nxt[h]
        if chained:
            for h in heads:
                finals[0][h] = state[h]
    head_outs = []
    for h in range(GDN_HEADS):
        o = outs[h, 0] if n_chunks == 1 else jnp.concatenate([outs[h, c] for c in chunks], axis=0)
        head_outs.append(_rms(o, g_ref[...]) * _silu(z[:, h * GDN_DV:(h + 1) * GDN_DV]))
    o_ref[...] = jnp.concatenate(head_outs, axis=1)
    new_states = jnp.stack([jnp.stack(per_seq) for per_seq in finals])
    sfin_ref[...] = new_states
    if chained:
        s_ref[...] = new_states[0]


def _gdn(p, conv0, s0, conv_w, a_log, dt_bias, norm_g, n_seq, seq_len, rows, chunk, n_valid):
    chained = seq_len > chunk
    seqs_per_step = 1 if chained else rows // chunk
    n_steps = seq_len // rows if chained else 1
    n_blocks = n_seq // seqs_per_step
    assert n_blocks * seqs_per_step == n_seq and chunk & (chunk - 1) == 0
    lane = jnp.zeros((2, V7X_LANES), F32)
    lane = lane.at[0, LANE_A:LANE_A + GDN_HEADS].set(a_log).at[1, LANE_A:LANE_A + GDN_HEADS].set(dt_bias)
    levels = jnp.asarray(_level_masks(chunk))
    halo_rows = V7X_SUBLANES + (rows if chained else chunk)
    return pl.pallas_call(
        functools.partial(_gdn_kernel, rows=rows, chunk=chunk, n_valid=n_valid, chained=chained),
        grid=(n_blocks, n_steps),
        in_specs=[_seq_spec(rows, 'qkv', n_steps), _seq_spec(rows, 'z_a', n_steps), _small_spec(rows, n_steps),
                  pl.BlockSpec((seqs_per_step, CONV_K - 1, 3 * GROUP_W), lambda b, n: (b, 0, 0)),
                  pl.BlockSpec((seqs_per_step, GDN_HEADS, GDN_DK, GDN_DV), lambda b, n: (b, 0, 0, 0)),
                  _resident((CONV_K, 3 * GROUP_W)), _resident((2, V7X_LANES)), _resident((1, GDN_DV)),
                  _resident(levels.shape)],
        out_specs=[_out_spec(rows, GROUP_W, n_steps),
                   pl.BlockSpec((seqs_per_step, GDN_HEADS, GDN_DK, GDN_DV), lambda b, n: (b, 0, 0, 0)),
                   pl.BlockSpec((seqs_per_step, CONV_K - 1, 3 * GROUP_W), lambda b, n: (b, 0, 0))],
        out_shape=[jax.ShapeDtypeStruct((n_seq * seq_len, GROUP_W), F32),
                   jax.ShapeDtypeStruct((n_seq, GDN_HEADS, GDN_DK, GDN_DV), F32),
                   jax.ShapeDtypeStruct((n_seq, CONV_K - 1, 3 * GROUP_W), F32)],
        scratch_shapes=[pltpu.VMEM((seqs_per_step, halo_rows, 3 * GROUP_W), F32),
                        pltpu.VMEM((GDN_HEADS, GDN_DK, GDN_DV), F32)],
        compiler_params=_cparams("arbitrary", "arbitrary"),
        name="gdn",
    )(p, p, p, conv0, s0, conv_w, lane, norm_g.reshape(1, GDN_DV), levels)


def _gmlp_kernel(u_ref, v_ref, lng_ref, lnb_ref, ws_ref, bias_ref, o_ref, vb_ref, *, rows, chunk):
    u = jax.nn.gelu(u_ref[...], approximate=True)
    v = jax.nn.gelu(v_ref[...], approximate=True)
    mu = jnp.mean(v, axis=-1, keepdims=True)
    var = jnp.mean(jnp.square(v - mu), axis=-1, keepdims=True)
    v = (v - mu) * lax.rsqrt(var + EPS) * lng_ref[...] + lnb_ref[...]
    vb_ref[...] = v
    tril = _iota2(chunk, chunk, 1) <= _iota2(chunk, chunk, 0)
    ws = [jnp.where(tril, ws_ref[g], 0.0).astype(BF16) for g in range(MLP_GROUPS)]
    for c in range(rows // chunk):
        rs = slice(c * chunk, (c + 1) * chunk)
        for g in range(MLP_GROUPS):
            ls = slice(g * MLP_GW, (g + 1) * MLP_GW)
            mixed = _mm(ws[g], v[rs, ls]) + bias_ref[:, ls]
            o_ref[rs, ls] = u[rs, ls] * mixed


def _gmlp(p, ln_g, ln_b, ws, bs, n_seq, seq_len, rows, chunk):
    n_steps = seq_len // rows
    bias = jnp.repeat(bs[:, :chunk].T, MLP_GW, axis=1)
    return pl.pallas_call(
        functools.partial(_gmlp_kernel, rows=rows, chunk=chunk),
        grid=(n_seq, n_steps),
        in_specs=[_seq_spec(rows, 'u', n_steps), _seq_spec(rows, 'v', n_steps),
                  _resident((1, GROUP_W)), _resident((1, GROUP_W)),
                  _resident((MLP_GROUPS, chunk, chunk)), _resident((chunk, GROUP_W))],
        out_specs=[_out_spec(rows, GROUP_W, n_steps), _out_spec(rows, GROUP_W, n_steps)],
        out_shape=[jax.ShapeDtypeStruct((n_seq * seq_len, GROUP_W), F32)] * 2,
        compiler_params=_cparams("arbitrary", "arbitrary"),
        name="gmlp",
    )(p, p, ln_g.reshape(1, GROUP_W), ln_b.reshape(1, GROUP_W), ws[:, :chunk, :chunk], bias)


def _ssd_kernel(xbc_ref, z_ref, sm_ref, conv0_ref, h0_ref, cw_ref, cb_ref, lane_ref, g_ref,
                o_ref, hfin_ref, conv_out_ref, halo_ref, h_ref, *, rows, seq_rows, n_valid):
    chained = seq_rows == rows
    n_seqs = rows // seq_rows
    shift = seq_rows.bit_length() - 1
    if chained:
        @pl.when(pl.program_id(1) == 0)
        def _():
            h_ref[...] = h0_ref[0]

    act = _silu(_causal_conv(xbc_ref[...], halo_ref, conv0_ref, conv_out_ref, cw_ref, rows, seq_rows, n_valid)
                + cb_ref[...])
    sm = sm_ref[...]
    row = _iota2(rows, rows, 0)
    col = _iota2(rows, rows, 1)
    tril = (col <= row) & (lax.shift_right_logical(row, shift) == lax.shift_right_logical(col, shift))
    valid = (_iota2(rows, V7X_LANES, 0) & (seq_rows - 1)) < n_valid
    dt = jnp.where(valid, _softplus(sm + lane_ref[1:2, :]), 0.0)
    acum = _mm_exact_lhs(jnp.where(tril, 1.0, 0.0).astype(BF16), dt * -jnp.exp(lane_ref[0:1, :]))
    acum_t = acum.T
    heads = range(SSM_HEADS)
    seqs = range(n_seqs)
    rsl = [slice(i * seq_rows, (i + 1) * seq_rows) for i in seqs]
    group_of = [h // (SSM_HEADS // SSM_GROUPS) for h in heads]
    b_g = [act[:, GROUP_W + g * SSM_STATE:GROUP_W + (g + 1) * SSM_STATE] for g in range(SSM_GROUPS)]
    c_g = [act[:, GROUP_W + (SSM_GROUPS + g) * SSM_STATE:GROUP_W + (SSM_GROUPS + g + 1) * SSM_STATE]
           for g in range(SSM_GROUPS)]
    cb = [_mm(c_g[g], b_g[g], _NT) for g in range(SSM_GROUPS)]
    acol = [acum[:, LANE_DT + h:LANE_DT + h + 1] for h in heads]
    decay = [jnp.exp(jnp.where(tril, acol[h] - acum_t[LANE_DT + h:LANE_DT + h + 1, :], NEG_BIG)) for h in heads]
    x = [act[:, h * SSM_HEAD_DIM:(h + 1) * SSM_HEAD_DIM] for h in heads]
    xdt = [x[h] * dt[:, LANE_DT + h:LANE_DT + h + 1] for h in heads]
    c_dec = [c_g[group_of[h]] * jnp.exp(acol[h]) for h in heads]
    y_intra = [_mm(cb[group_of[h]] * decay[h], xdt[h]) for h in heads]
    state = {(h, i): (h_ref[h] if chained else h0_ref[i, h]) for h in heads for i in seqs}
    alast = {(h, i): acol[h][(i + 1) * seq_rows - 1:(i + 1) * seq_rows] for h in heads for i in seqs}
    y_inter = {(h, i): _mm(c_dec[h][rsl[i]], state[h, i], _NT) for h in heads for i in seqs}
    h_new = {(h, i): state[h, i] * jnp.exp(alast[h, i])
             + _mm(xdt[h][rsl[i]], b_g[group_of[h]][rsl[i]] * jnp.exp(alast[h, i] - acol[h][rsl[i]]), _TN)
             for h in heads for i in seqs}
    y = jnp.concatenate(
        [y_intra[h] + (y_inter[h, 0] if chained else jnp.concatenate([y_inter[h, i] for i in seqs], axis=0))
         + lane_ref[2:3, LANE_DT + h:LANE_DT + h + 1] * x[h] for h in heads], axis=1)
    new_states = jnp.stack([jnp.stack([h_new[h, i] for h in heads]) for i in seqs])
    hfin_ref[...] = new_states
    if chained:
        h_ref[...] = new_states[0]
    o_ref[...] = _rms(y * _silu(z_ref[...]), g_ref[...])


def _ssd(p, conv0, h0, conv_w, conv_b, a_log, dt_bias, d_skip, norm_g, n_seq, seq_len, rows, n_valid):
    chained = seq_len >= rows
    seq_rows = rows if chained else seq_len
    seqs_per_step = rows // seq_rows
    n_steps = seq_len // rows if chained else 1
    n_blocks = n_seq // seqs_per_step
    assert n_blocks * seqs_per_step == n_seq and seq_rows & (seq_rows - 1) == 0
    lane = jnp.zeros((3, V7X_LANES), F32)
    lane = (lane.at[0, LANE_DT:LANE_DT + SSM_HEADS].set(a_log)
            .at[1, LANE_DT:LANE_DT + SSM_HEADS].set(dt_bias)
            .at[2, LANE_DT:LANE_DT + SSM_HEADS].set(d_skip))
    return pl.pallas_call(
        functools.partial(_ssd_kernel, rows=rows, seq_rows=seq_rows, n_valid=n_valid),
        grid=(n_blocks, n_steps),
        in_specs=[_seq_spec(rows, 'xbc', n_steps), _seq_spec(rows, 'z_c', n_steps), _small_spec(rows, n_steps),
                  pl.BlockSpec((seqs_per_step, CONV_K - 1, SSM_CONV_W), lambda b, n: (b, 0, 0)),
                  pl.BlockSpec((seqs_per_step, SSM_HEADS, SSM_HEAD_DIM, SSM_STATE), lambda b, n: (b, 0, 0, 0)),
                  _resident((CONV_K, SSM_CONV_W)), _resident((1, SSM_CONV_W)), _resident((3, V7X_LANES)),
                  _resident((1, GROUP_W))],
        out_specs=[_out_spec(rows, GROUP_W, n_steps),
                   pl.BlockSpec((seqs_per_step, SSM_HEADS, SSM_HEAD_DIM, SSM_STATE), lambda b, n: (b, 0, 0, 0)),
                   pl.BlockSpec((seqs_per_step, CONV_K - 1, SSM_CONV_W), lambda b, n: (b, 0, 0))],
        out_shape=[jax.ShapeDtypeStruct((n_seq * seq_len, GROUP_W), F32),
                   jax.ShapeDtypeStruct((n_seq, SSM_HEADS, SSM_HEAD_DIM, SSM_STATE), F32),
                   jax.ShapeDtypeStruct((n_seq, CONV_K - 1, SSM_CONV_W), F32)],
        scratch_shapes=[pltpu.VMEM((seqs_per_step, V7X_SUBLANES + seq_rows, SSM_CONV_W), F32),
                        pltpu.VMEM((SSM_HEADS, SSM_HEAD_DIM, SSM_STATE), F32)],
        compiler_params=_cparams("arbitrary", "arbitrary"),
        name="ssd",
    )(p, p, p, conv0, h0, conv_w, conv_b.reshape(1, SSM_CONV_W), lane, norm_g.reshape(1, GROUP_W))


def _rope_lanes(x, cos, sin_signed):
    lane = _iota2(x.shape[0], V7X_LANES, 1)
    half = MLA_ROPE // 2
    swapped = jnp.where(lane % MLA_ROPE < half,
                        pltpu.roll(x, V7X_LANES - half, axis=1), pltpu.roll(x, half, axis=1))
    return x * cos + swapped * sin_signed


def _rope_tables(pos):
    half = MLA_ROPE // 2
    inv_freq = ROPE_THETA ** (-jnp.arange(half, dtype=F32) / half)
    ang = pos.astype(F32)[:, None] * inv_freq[None, :]
    cos, sin = jnp.cos(ang), jnp.sin(ang)
    reps = V7X_LANES // MLA_ROPE
    return jnp.tile(jnp.concatenate([cos, cos], axis=1), (1, reps)), jnp.tile(jnp.concatenate([-sin, sin], axis=1), (1, reps))


def _mla_project(cq_ref, ckv_ref, sm_ref, cos_ref, sin_ref, gq_ref, gkv_ref, wn_ref, wpe_ref, wuk_ref):
    c_q = _rms(cq_ref[...], gq_ref[...]).astype(BF16)
    c_kv = _rms(ckv_ref[...], gkv_ref[...])
    cos, sin = cos_ref[...], sin_ref[...]
    lane = _iota2(c_kv.shape[0], V7X_LANES, 1)
    k_pe = _rope_lanes(jnp.where(lane < MLA_ROPE, sm_ref[...], 0.0), cos, sin)
    q_nope = jnp.dot(c_q, wn_ref[...], preferred_element_type=F32)
    q_heads = []
    for h in range(MLA_HEADS):
        q_lat = _mm(q_nope[:, h * MLA_NOPE:(h + 1) * MLA_NOPE], wuk_ref[h], _NT)
        q_pe = _rope_lanes(jnp.dot(c_q, wpe_ref[h], preferred_element_type=F32), cos, sin)
        q_heads.append((jnp.concatenate([q_lat, q_pe], axis=1) * MLA_SCALE).astype(BF16))
    return c_kv, k_pe, q_heads


def _softmax_step(q, keys, m, l, acc, mask=None):
    s = lax.dot_general(q, keys, _NT, preferred_element_type=F32)
    if mask is not None:
        s = jnp.where(mask, s, NEG_BIG)
    m_new = jnp.maximum(m, jnp.max(s, axis=-1, keepdims=True))
    p = jnp.exp(s - m_new)
    alpha = jnp.exp(m - m_new)
    l = alpha * l + jnp.sum(p, axis=-1, keepdims=True)
    acc = alpha * acc + jnp.dot(p.astype(BF16), keys[:, :MLA_KV_RANK], preferred_element_type=F32)
    return m_new, l, acc


def _mla_prompt_kernel(cq_ref, ckv_ref, sm_ref, cos_ref, sin_ref, gq_ref, gkv_ref, wn_ref, wpe_ref, wuk_ref,
                       wuv_ref, o_ref, ckv_out_ref, kpe_out_ref, keys_ref, vt_ref, q_ref, m_ref, acc_ref,
                       *, rows, kv_block):
    n = pl.program_id(1)
    qrows = MLA_HEADS * rows
    c_kv, k_pe, q_heads = _mla_project(cq_ref, ckv_ref, sm_ref, cos_ref, sin_ref, gq_ref, gkv_ref,
                                       wn_ref, wpe_ref, wuk_ref)
    ckv_out_ref[0] = c_kv
    kpe_out_ref[0] = k_pe[:, :MLA_ROPE]
    base = pl.multiple_of(n * rows, rows)
    keys_ref[pl.ds(base, rows), :] = jnp.concatenate([c_kv, k_pe], axis=1).astype(BF16)
    vt_ref[0:MLA_KV_RANK, pl.ds(base, rows)] = c_kv.T.astype(BF16)
    vt_ref[MLA_KV_RANK:MLA_VT_ROWS, pl.ds(base, rows)] = jnp.ones((MLA_VT_ROWS - MLA_KV_RANK, rows), BF16)
    q_ref[...] = jnp.concatenate(q_heads, axis=0)
    m_ref[...] = jnp.full((1, qrows), NEG_BIG, F32)
    acc_ref[...] = jnp.zeros((MLA_VT_ROWS, qrows), F32)

    def attend(off, mask):
        s = lax.dot_general(keys_ref[pl.ds(off, kv_block), :], q_ref[...], _NT, preferred_element_type=F32)
        if mask is not None:
            s = jnp.where(mask, s, NEG_BIG)
        m_old = m_ref[...]
        m_new = jnp.maximum(m_old, jnp.max(s, axis=0, keepdims=True))
        p = jnp.exp((s - m_new).astype(BF16))
        alpha = jnp.exp(m_old - m_new)
        acc_ref[...] = alpha * acc_ref[...] + jnp.dot(vt_ref[:, pl.ds(off, kv_block)], p,
                                                      preferred_element_type=F32)
        m_ref[...] = m_new

    def body(j, carry):
        attend(pl.multiple_of(j * kv_block, kv_block), None)
        return carry

    blocks_per_step = rows // kv_block
    lax.fori_loop(0, n * blocks_per_step, body, 0)
    q_pos = _iota2(kv_block, qrows, 1) & (rows - 1)
    for d in range(blocks_per_step):
        attend(base + d * kv_block, d * kv_block + _iota2(kv_block, qrows, 0) <= q_pos)
    o_lat_t = acc_ref[0:MLA_KV_RANK, :] / acc_ref[MLA_KV_RANK:MLA_KV_RANK + 1, :]
    outs = [_mm(o_lat_t[:, h * rows:(h + 1) * rows], wuv_ref[h], _TN) for h in range(MLA_HEADS)]
    o_ref[...] = jnp.concatenate(outs, axis=1)


def _mla_weights(w_uq, w_uk, w_uv):
    w4 = w_uq.reshape(MLA_Q_RANK, MLA_HEADS, MLA_NOPE + MLA_ROPE)
    w_nope = w4[:, :, :MLA_NOPE].reshape(MLA_Q_RANK, MLA_HEADS * MLA_NOPE).astype(BF16)
    w_pe = jnp.pad(w4[:, :, MLA_NOPE:].transpose(1, 0, 2),
                   ((0, 0), (0, 0), (0, V7X_LANES - MLA_ROPE))).astype(BF16)
    return w_nope, w_pe, w_uk.astype(BF16), w_uv.astype(BF16)


def _mla_prompt(p, q_norm_g, kv_norm_g, w_nope, w_pe, w_uk, w_uv, n_seq, seq_len, rows, kv_block):
    n_steps = seq_len // rows
    assert seq_len % rows == 0 and rows % kv_block == 0 and rows & (rows - 1) == 0
    cos, sin = _rope_tables(jnp.arange(seq_len))
    tab = pl.BlockSpec((rows, V7X_LANES), lambda b, n: (n, 0))
    return pl.pallas_call(
        functools.partial(_mla_prompt_kernel, rows=rows, kv_block=kv_block),
        grid=(n_seq, n_steps),
        in_specs=[_seq_spec(rows, 'c_q', n_steps), _seq_spec(rows, 'c_kv', n_steps), _small_spec(rows, n_steps),
                  tab, tab, _resident((1, MLA_Q_RANK)), _resident((1, MLA_KV_RANK)),
                  _resident(w_nope.shape), _resident(w_pe.shape), _resident(w_uk.shape), _resident(w_uv.shape)],
        out_specs=[_out_spec(rows, GROUP_W, n_steps),
                   pl.BlockSpec((1, rows, MLA_KV_RANK), lambda b, n: (b, n, 0)),
                   pl.BlockSpec((1, rows, MLA_ROPE), lambda b, n: (b, n, 0))],
        out_shape=[jax.ShapeDtypeStruct((n_seq * seq_len, GROUP_W), F32),
                   jax.ShapeDtypeStruct((n_seq, seq_len, MLA_KV_RANK), F32),
                   jax.ShapeDtypeStruct((n_seq, seq_len, MLA_ROPE), F32)],
        scratch_shapes=[pltpu.VMEM((seq_len, 2 * V7X_LANES), BF16),
                        pltpu.VMEM((MLA_VT_ROWS, seq_len), BF16),
                        pltpu.VMEM((MLA_HEADS * rows, 2 * V7X_LANES), BF16),
                        pltpu.VMEM((1, MLA_HEADS * rows), F32),
                        pltpu.VMEM((MLA_VT_ROWS, MLA_HEADS * rows), F32)],
        compiler_params=_cparams("arbitrary", "arbitrary"),
        name="mla_prompt",
    )(p, p, p, cos, sin, q_norm_g.reshape(1, MLA_Q_RANK), kv_norm_g.reshape(1, MLA_KV_RANK),
      w_nope, w_pe, w_uk, w_uv)


def _mla_sample_project_kernel(cq_ref, ckv_ref, sm_ref, cos_ref, sin_ref, gq_ref, gkv_ref, wn_ref, wpe_ref, wuk_ref,
                               q_out_ref, knew_out_ref, ckv_out_ref, kpe_out_ref):
    c_kv, k_pe, q_heads = _mla_project(cq_ref, ckv_ref, sm_ref, cos_ref, sin_ref, gq_ref, gkv_ref,
                                       wn_ref, wpe_ref, wuk_ref)
    ckv_out_ref[...] = c_kv
    kpe_out_ref[...] = k_pe[:, :MLA_ROPE]
    knew_out_ref[...] = jnp.concatenate([c_kv, k_pe], axis=1)
    for h in range(MLA_HEADS):
        q_out_ref[h] = q_heads[h].astype(F32)


def _mla_sample_project(p, q_norm_g, kv_norm_g, w_nope, w_pe, w_uk, n_seq, rows, past_len):
    n_tok = n_seq * rows
    tile = min(SEQ_TILE, n_tok)
    assert n_tok % tile == 0 and tile % rows == 0
    cos, sin = _rope_tables(past_len + jnp.arange(rows))
    cos, sin = jnp.tile(cos, (tile // rows, 1)), jnp.tile(sin, (tile // rows, 1))

    def piece(name):
        lo, hi = _DST_COLS[name]
        return pl.BlockSpec((tile, hi - lo), lambda i, c=_col_block(name): (i, c))

    width = 2 * V7X_LANES
    return pl.pallas_call(
        _mla_sample_project_kernel,
        grid=(n_tok // tile,),
        in_specs=[piece('c_q'), piece('c_kv'), pl.BlockSpec((tile, V7X_LANES), lambda i: (i, SMALL_BLOCK)),
                  _resident((tile, V7X_LANES)), _resident((tile, V7X_LANES)),
                  _resident((1, MLA_Q_RANK)), _resident((1, MLA_KV_RANK)),
                  _resident(w_nope.shape), _resident(w_pe.shape), _resident(w_uk.shape)],
        out_specs=[pl.BlockSpec((MLA_HEADS, tile, width), lambda i: (0, i, 0)),
                   pl.BlockSpec((tile, width), lambda i: (i, 0)),
                   pl.BlockSpec((tile, MLA_KV_RANK), lambda i: (i, 0)),
                   pl.BlockSpec((tile, MLA_ROPE), lambda i: (i, 0))],
        out_shape=[jax.ShapeDtypeStruct((MLA_HEADS, n_tok, width), F32),
                   jax.ShapeDtypeStruct((n_tok, width), F32),
                   jax.ShapeDtypeStruct((n_tok, MLA_KV_RANK), F32),
                   jax.ShapeDtypeStruct((n_tok, MLA_ROPE), F32)],
        compiler_params=_cparams("arbitrary"),
        name="mla_sample_project",
    )(p, p, p, cos, sin, q_norm_g.reshape(1, MLA_Q_RANK), kv_norm_g.reshape(1, MLA_KV_RANK), w_nope, w_pe, w_uk)


def _mla_decode_kernel(pt_ref, q_ref, knew_ref, wuv_ref, ckv_hbm, kpe_hbm, o_ref,
                       lat_buf, pe_buf, sems, s_ref, latb_ref, *, rows, layer, n_pages, pages_per_chunk):
    seq = pl.program_id(0)
    slot = lax.rem(seq, 2)
    qrows = MLA_HEADS * rows
    chunk = pages_per_chunk * PAGE_SIZE
    n_chunks = n_pages // pages_per_chunk

    def page_copies(src_seq, dst_slot, g):
        page = pt_ref[src_seq, g]
        return (pltpu.make_async_copy(ckv_hbm.at[layer, page], lat_buf.at[dst_slot, g], sems.at[0, dst_slot]),
                pltpu.make_async_copy(kpe_hbm.at[layer, page],
                                      pe_buf.at[dst_slot, :, pl.ds(pl.multiple_of(g * PAGE_SIZE, PAGE_SIZE), PAGE_SIZE)],
                                      sems.at[1, dst_slot]))

    def fetch(src_seq, dst_slot):
        def body(g, carry):
            for cp in page_copies(src_seq, dst_slot, g):
                cp.start()
            return carry
        lax.fori_loop(0, n_pages, body, 0, unroll=8)

    @pl.when(seq == 0)
    def _():
        fetch(0, 0)

    @pl.when(seq + 1 < pl.num_programs(0))
    def _():
        fetch(seq + 1, 1 - slot)

    knew = knew_ref[...].astype(BF16)
    q = jnp.concatenate([q_ref[h] for h in range(MLA_HEADS)], axis=0).astype(BF16)
    q_lat, q_pe = q[:, :MLA_KV_RANK], q[:, MLA_KV_RANK:MLA_KV_RANK + MLA_ROPE]
    causal = _iota2(qrows, rows, 1) <= (_iota2(qrows, rows, 0) & (rows - 1))
    s_new = jnp.where(causal, lax.dot_general(q, knew, _NT, preferred_element_type=F32), NEG_BIG)

    pltpu.make_async_copy(lat_buf.at[slot], lat_buf.at[slot], sems.at[0, slot]).wait()
    pltpu.make_async_copy(pe_buf.at[slot], pe_buf.at[slot], sems.at[1, slot]).wait()

    def to_bf16(g, carry):
        latb_ref[pl.ds(pl.multiple_of(g * PAGE_SIZE, PAGE_SIZE), PAGE_SIZE), :] = lat_buf[slot, g].astype(BF16)
        return carry
    lax.fori_loop(0, n_pages, to_bf16, 0, unroll=8)

    m = jnp.max(s_new, axis=-1, keepdims=True)
    for c in range(n_chunks):
        s_lat = lax.dot_general(q_lat, latb_ref[c * chunk:(c + 1) * chunk, :], _NT, preferred_element_type=F32)
        s_pe = jnp.dot(q_pe, pe_buf[slot, :, c * chunk:(c + 1) * chunk].astype(BF16), preferred_element_type=F32)
        s = s_lat + s_pe
        s_ref[:, c * chunk:(c + 1) * chunk] = s
        m = jnp.maximum(m, jnp.max(s, axis=-1, keepdims=True))

    p_new = jnp.exp(s_new - m)
    l = jnp.sum(p_new, axis=-1, keepdims=True)
    acc = jnp.dot(p_new.astype(BF16), knew[:, :MLA_KV_RANK], preferred_element_type=F32)
    for c in range(n_chunks):
        pr = jnp.exp(s_ref[:, c * chunk:(c + 1) * chunk] - m)
        l = l + jnp.sum(pr, axis=-1, keepdims=True)
        acc = acc + jnp.dot(pr.astype(BF16), latb_ref[c * chunk:(c + 1) * chunk, :], preferred_element_type=F32)
    o_lat = acc / l
    outs = [_mm(o_lat[h * rows:(h + 1) * rows], wuv_ref[h]) for h in range(MLA_HEADS)]
    o_ref[...] = jnp.concatenate(outs, axis=1)


def _mla_decode(p, cache_ckv, cache_kpe, layer, page_table, q_norm_g, kv_norm_g, w_nope, w_pe, w_uk, w_uv,
                n_seq, rows, pages_per_chunk=16):
    n_pages = page_table.shape[1]
    pages_per_chunk = min(pages_per_chunk, n_pages)
    assert n_pages % pages_per_chunk == 0
    past_len = n_pages * PAGE_SIZE
    q_new, k_new, c_kv, k_pe = _mla_sample_project(p, q_norm_g, kv_norm_g, w_nope, w_pe, w_uk, n_seq, rows, past_len)
    width = 2 * V7X_LANES
    in_specs = [pl.BlockSpec((MLA_HEADS, rows, width), lambda s, pt: (0, s, 0)),
                pl.BlockSpec((rows, width), lambda s, pt: (s, 0)),
                pl.BlockSpec(w_uv.shape, lambda s, pt: (0,) * w_uv.ndim),
                pl.BlockSpec(memory_space=pl.ANY), pl.BlockSpec(memory_space=pl.ANY)]
    qrows = MLA_HEADS * rows
    out_d, = pl.pallas_call(
        functools.partial(_mla_decode_kernel, rows=rows, layer=layer, n_pages=n_pages,
                          pages_per_chunk=pages_per_chunk),
        grid_spec=pltpu.PrefetchScalarGridSpec(
            num_scalar_prefetch=1,
            grid=(n_seq,),
            in_specs=in_specs,
            out_specs=[pl.BlockSpec((rows, GROUP_W), lambda s, pt: (s, 0))],
            scratch_shapes=[pltpu.VMEM((2, n_pages, PAGE_SIZE, MLA_KV_RANK), F32),
                            pltpu.VMEM((2, MLA_ROPE, past_len), F32),
                            pltpu.SemaphoreType.DMA((2, 2)),
                            pltpu.VMEM((qrows, past_len), F32),
                            pltpu.VMEM((past_len, MLA_KV_RANK), BF16)]),
        out_shape=[jax.ShapeDtypeStruct((n_seq * rows, GROUP_W), F32)],
        compiler_params=_cparams("arbitrary"),
        name="mla_decode",
    )(page_table, q_new, k_new, w_uv, cache_ckv, cache_kpe)
    return out_d, c_kv.reshape(n_seq, rows, MLA_KV_RANK), k_pe.reshape(n_seq, rows, MLA_ROPE)


def _trunk_layer(x, w, layer, n_seq, seq_len, rows, n_valid, gdn_conv, gdn_s, ssm_conv, ssm_h, paged):
    g = w['norm_g']
    assert n_valid == seq_len or seq_len == rows
    step_valid = min(n_valid, rows)
    x = _ffn(x, g[0:1], g[1:2], w['ffn_w_in'][0], w['ffn_w_out'][0])
    p = _proj(x, g[2:3], w['w_in'])
    if seq_len == rows:
        gdn_shape = (rows * SAMPLE_SEQS_PER_STEP, rows, n_valid)
        mlp_shape = (1, n_seq * seq_len, min(SEQ_TILE, n_seq * seq_len), rows)
        ssd_rows = rows * SAMPLE_SEQS_PER_STEP
    else:
        gdn_shape = (rows, GDN_CHUNK, GDN_CHUNK)
        mlp_shape = (n_seq, seq_len, rows, MLP_CHUNK)
        ssd_rows = rows
    out_a, gdn_s_new, gdn_conv_new = _gdn(p, gdn_conv, gdn_s, w['gdn_conv_w'], w['gdn_a_log'], w['gdn_dt_bias'],
                            w['gdn_norm_g'], n_seq, seq_len, *gdn_shape)
    out_b, v_b = _gmlp(p, w['mlp_ln_g'], w['mlp_ln_b'], w['mlp_ws'], w['mlp_bs'], *mlp_shape)
    out_c, ssm_h_new, ssm_conv_new = _ssd(p, ssm_conv, ssm_h, w['ssm_conv_w'], w['ssm_conv_b'], w['ssm_a_log'],
                            w['ssm_dt_bias'], w['ssm_d'], w['ssm_norm_g'], n_seq, seq_len, ssd_rows, step_valid)
    mla_w = (w['mla_q_norm_g'], w['mla_kv_norm_g'], w['mla_w_nope'], w['mla_w_pe'], w['mla_w_uk'], w['mla_w_uv'])
    if paged is None:
        out_d, c_kv, k_pe = _mla_prompt(p, *mla_w, n_seq, seq_len, MLA_Q_TILE, MLA_KV_BLOCK)
    else:
        cache_ckv, cache_kpe, page_table = paged
        out_d, c_kv, k_pe = _mla_decode(p, cache_ckv, cache_kpe, layer, page_table, *mla_w, n_seq, rows)
    x = _ffn(x, g[4:5], g[5:6], w['ffn_w_in'][1], w['ffn_w_out'][1],
             mix=(out_a, out_b, out_c, out_d), w_o=w['w_out'], g_mix=g[3:4])
    v_b = v_b.reshape(n_seq, seq_len, GROUP_W)[:, :n_valid]
    return x, (c_kv[:, :n_valid], k_pe[:, :n_valid], gdn_s_new, gdn_conv_new, ssm_h_new, ssm_conv_new, v_b)


def kernel(x_prompt, x_sample, cache_ckv, cache_kpe, page_table, state_gdn_s, state_gdn_conv,
           state_ssm_h, state_ssm_conv, norm_g, ffn_w_in, ffn_w_out, w_in, w_out,
           gdn_conv_w, gdn_a_log, gdn_dt_bias, gdn_norm_g, mlp_ln_g, mlp_ln_b, mlp_ws, mlp_bs,
           ssm_conv_w, ssm_conv_b, ssm_a_log, ssm_dt_bias, ssm_d, ssm_norm_g,
           mla_q_norm_g, mla_w_uq, mla_kv_norm_g, mla_w_uk, mla_w_uv):
    bp, lp, _ = x_prompt.shape
    bs, ls, _ = x_sample.shape
    assert lp % SEQ_TILE == 0 and CONV_K - 1 <= ls <= SAMPLE_ROWS
    w_in_p = jnp.concatenate(
        [w_in[:, :, _SRC_COLS[name][0]:_SRC_COLS[name][1]] for name in _DST_ORDER]
        + [jnp.zeros(w_in.shape[:2] + (PROJ_COLS - PROJ_USED,), w_in.dtype)], axis=2)
    cache_kpe_t = jnp.swapaxes(cache_kpe, 2, 3)
    weights = dict(norm_g=norm_g, ffn_w_in=ffn_w_in.astype(BF16), ffn_w_out=ffn_w_out.astype(BF16),
                   w_in=w_in_p.astype(BF16), w_out=w_out.astype(BF16),
                   gdn_conv_w=gdn_conv_w, gdn_a_log=gdn_a_log, gdn_dt_bias=gdn_dt_bias, gdn_norm_g=gdn_norm_g,
                   mlp_ln_g=mlp_ln_g, mlp_ln_b=mlp_ln_b, mlp_ws=mlp_ws, mlp_bs=mlp_bs,
                   ssm_conv_w=ssm_conv_w, ssm_conv_b=ssm_conv_b, ssm_a_log=ssm_a_log,
                   ssm_dt_bias=ssm_dt_bias, ssm_d=ssm_d, ssm_norm_g=ssm_norm_g,
                   mla_q_norm_g=mla_q_norm_g, mla_kv_norm_g=mla_kv_norm_g)
    xp = x_prompt.reshape(bp * lp, D_MODEL)
    xs = jnp.pad(x_sample, ((0, 0), (0, SAMPLE_ROWS - ls), (0, 0))).reshape(bs * SAMPLE_ROWS, D_MODEL)
    zeros_p = (jnp.zeros((bp, CONV_K - 1, 3 * GROUP_W), F32), jnp.zeros((bp, GDN_HEADS, GDN_DK, GDN_DV), F32),
               jnp.zeros((bp, CONV_K - 1, SSM_CONV_W), F32), jnp.zeros((bp, SSM_HEADS, SSM_HEAD_DIM, SSM_STATE), F32))
    st_p, st_s = [], []
    for l in range(DEPTH):
        wl = {name: arr[l] for name, arr in weights.items()}
        wl['mla_w_nope'], wl['mla_w_pe'], wl['mla_w_uk'], wl['mla_w_uv'] = _mla_weights(
            mla_w_uq[l], mla_w_uk[l], mla_w_uv[l])
        xp, st = _trunk_layer(xp, wl, l, bp, lp, SEQ_TILE, lp, *zeros_p, None)
        st_p.append(st)
        xs, st = _trunk_layer(xs, wl, l, bs, SAMPLE_ROWS, SAMPLE_ROWS, ls,
                              state_gdn_conv[l], state_gdn_s[l], state_ssm_conv[l], state_ssm_h[l],
                              (cache_ckv, cache_kpe_t, page_table))
        st_s.append(st)

    def stack(states, i):
        return jnp.stack([s[i] for s in states])

    y_prompt = xp.reshape(bp, lp, D_MODEL)
    y_sample = xs.reshape(bs, SAMPLE_ROWS, D_MODEL)[:, :ls]
    return (y_prompt, y_sample,
            stack(st_p, 0), stack(st_p, 1), stack(st_p, 2), stack(st_p, 3), stack(st_p, 4), stack(st_p, 5),
            stack(st_s, 0), stack(st_s, 1), stack(st_s, 2), stack(st_s, 3), stack(st_s, 4), stack(st_s, 5),
            stack(st_s, 6))
```

```python
import functools

import numpy as np
import jax
import jax.numpy as jnp
from jax import lax
from jax.experimental import pallas as pl
from jax.experimental.pallas import tpu as pltpu

F32 = jnp.float32
BF16 = jnp.bfloat16

D_MODEL = 1024
DEPTH = 2
PAGE_SIZE = 128
GROUP_W = 256
CONV_K = 4
FFN_DIM = 2816
EPS = 1e-6

GDN_HEADS = 4
GDN_DK = 64
GDN_DV = 64
MLP_CHUNK = 128
MLP_GROUPS = 4
MLP_GW = 64
SSM_HEADS = 4
SSM_HEAD_DIM = 64
SSM_GROUPS = 2
SSM_STATE = 128
SSM_CONV_W = 768
MLA_HEADS = 4
MLA_NOPE = 64
MLA_ROPE = 32
MLA_V_DIM = 64
MLA_Q_RANK = 256
MLA_KV_RANK = 128
MLA_SCALE = (MLA_NOPE + MLA_ROPE) ** -0.5
ROPE_THETA = 10000.0

V7X_LANES = 128
V7X_SUBLANES = 8
V7X_MXU_DIM = 256
V7X_VMEM_LIMIT_BYTES = 56 * 1024 * 1024

_SRC_COLS = dict(
    qkv=(0, 768), z_a=(768, 1024), a_a=(1024, 1028), b_a=(1028, 1032),
    u=(1032, 1288), v=(1288, 1544),
    z_c=(1544, 1800), xbc=(1800, 2568), dt=(2568, 2572),
    c_q=(2572, 2828), c_kv=(2828, 2956), k_pe=(2956, 2988))
_DST_ORDER = ('qkv', 'xbc', 'z_a', 'u', 'v', 'z_c', 'c_q', 'c_kv', 'k_pe', 'a_a', 'b_a', 'dt')
_DST_COLS = {}
_off = 0
for _name in _DST_ORDER:
    _w = _SRC_COLS[_name][1] - _SRC_COLS[_name][0]
    _DST_COLS[_name] = (_off, _off + _w)
    _off += _w
PROJ_USED = _off
PROJ_COLS = -(-PROJ_USED // V7X_LANES) * V7X_LANES
_PERM = np.concatenate([np.arange(*_SRC_COLS[n]) for n in _DST_ORDER])
SMALL_BLOCK = _DST_COLS['k_pe'][0] // V7X_LANES
LANE_A = _DST_COLS['a_a'][0] % V7X_LANES
LANE_B = _DST_COLS['b_a'][0] % V7X_LANES
LANE_DT = _DST_COLS['dt'][0] % V7X_LANES


def _col_block(name):
    lo, hi = _DST_COLS[name]
    assert lo % (hi - lo) == 0
    return lo // (hi - lo)


FFN_CHUNK = V7X_MXU_DIM
TOKEN_TILE = 512
SEQ_TILE = 256
MLA_Q_TILE = 512
MLA_KV_BLOCK = 512
SAMPLE_ROWS = V7X_SUBLANES
SAMPLE_SEQS_PER_STEP = 8
GDN_CHUNK = 64
GDN_HEAD_GROUP = 4
MLA_VT_ROWS = MLA_KV_RANK + 16
NEG_BIG = -1e30


def _cparams(*sem):
    return pltpu.CompilerParams(dimension_semantics=sem, vmem_limit_bytes=V7X_VMEM_LIMIT_BYTES)


def _rms(x, g):
    return x * lax.rsqrt(jnp.mean(x * x, axis=-1, keepdims=True) + EPS) * g


def _silu(x):
    return x * jax.nn.sigmoid(x)


def _softplus(x):
    return jnp.maximum(x, 0.0) + jnp.log(1.0 + jnp.exp(-jnp.abs(x)))


_NN = (((1,), (0,)), ((), ()))
_NT = (((1,), (1,)), ((), ()))
_TN = (((0,), (0,)), ((), ()))


def _mm(a, b, dims=_NN):
    return lax.dot_general(a.astype(BF16), b.astype(BF16), dims, preferred_element_type=F32)


def _split3(x):
    x0 = x.astype(BF16)
    r1 = x - x0.astype(F32)
    x1 = r1.astype(BF16)
    x2 = (r1 - x1.astype(F32)).astype(BF16)
    return x0, x1, x2


def _mm_exact_lhs(a_bf16, b):
    b0, b1, b2 = _split3(b)
    dot = lambda y: lax.dot_general(a_bf16, y, _NN, preferred_element_type=F32)
    return dot(b0) + (dot(b1) + dot(b2))


def _iota2(n, m, axis):
    return lax.broadcasted_iota(jnp.int32, (n, m), axis)


def _resident(shape):
    return pl.BlockSpec(shape, lambda *_: (0,) * len(shape), pipeline_mode=pl.Buffered(1))


def _ffn_body(x, gpre_ref, gpost_ref, win_ref, wout_ref, o_ref, acc_ref):
    xn = _rms(x, gpre_ref[...]).astype(BF16)
    n_chunks = FFN_DIM // FFN_CHUNK
    for c in range(n_chunks):
        lo = c * FFN_CHUNK
        gate = jnp.dot(xn, win_ref[:, lo:lo + FFN_CHUNK], preferred_element_type=F32)
        up = jnp.dot(xn, win_ref[:, FFN_DIM + lo:FFN_DIM + lo + FFN_CHUNK], preferred_element_type=F32)
        h = (_silu(gate) * up).astype(BF16)
        part = jnp.dot(h, wout_ref[lo:lo + FFN_CHUNK, :], preferred_element_type=F32)
        if c == 0:
            acc_ref[...] = part
        else:
            acc_ref[...] += part
    o_ref[...] = x + 0.5 * _rms(acc_ref[...], gpost_ref[...])


def _ffn_kernel(x_ref, gpre_ref, gpost_ref, win_ref, wout_ref, o_ref, acc_ref):
    _ffn_body(x_ref[...], gpre_ref, gpost_ref, win_ref, wout_ref, o_ref, acc_ref)


def _mix_ffn_kernel(x_ref, ma_ref, mb_ref, mc_ref, md_ref, wo_ref, gmix_ref, gpre_ref, gpost_ref,
                    win_ref, wout_ref, o_ref, acc_ref):
    y = None
    for i, m_ref in enumerate((ma_ref, mb_ref, mc_ref, md_ref)):
        part = jnp.dot(m_ref[...].astype(BF16), wo_ref[i * GROUP_W:(i + 1) * GROUP_W, :],
                       preferred_element_type=F32)
        y = part if y is None else y + part
    x = x_ref[...] + _rms(y, gmix_ref[...])
    _ffn_body(x, gpre_ref, gpost_ref, win_ref, wout_ref, o_ref, acc_ref)


def _ffn(x, g_pre, g_post, w_in, w_out, mix=None, w_o=None, g_mix=None):
    t = x.shape[0]
    tm = min(TOKEN_TILE, t)
    assert t % tm == 0
    tok = pl.BlockSpec((tm, D_MODEL), lambda i: (i, 0))
    vec = _resident((1, D_MODEL))
    w_specs = [_resident((D_MODEL, 2 * FFN_DIM)), _resident((FFN_DIM, D_MODEL))]
    if mix is None:
        kern, args = _ffn_kernel, (x, g_pre, g_post, w_in, w_out)
        in_specs = [tok, vec, vec] + w_specs
    else:
        kern, args = _mix_ffn_kernel, (x, *mix, w_o, g_mix, g_pre, g_post, w_in, w_out)
        part = pl.BlockSpec((tm, GROUP_W), lambda i: (i, 0))
        in_specs = [tok, part, part, part, part, _resident((D_MODEL, D_MODEL)), vec, vec, vec] + w_specs
    return pl.pallas_call(
        kern,
        grid=(t // tm,),
        in_specs=in_specs,
        out_specs=tok,
        out_shape=jax.ShapeDtypeStruct((t, D_MODEL), F32),
        scratch_shapes=[pltpu.VMEM((tm, D_MODEL), F32)],
        compiler_params=_cparams("arbitrary"),
        name="mix_ffn" if mix is not None else "ffn",
    )(*args)


def _proj_kernel(x_ref, g_ref, w_ref, o_ref):
    xn = _rms(x_ref[...], g_ref[...]).astype(BF16)
    o_ref[...] = jnp.dot(xn, w_ref[...], preferred_element_type=F32)


def _proj(x, g, w):
    t = x.shape[0]
    tm = min(TOKEN_TILE, t)
    assert t % tm == 0
    return pl.pallas_call(
        _proj_kernel,
        grid=(t // tm,),
        in_specs=[pl.BlockSpec((tm, D_MODEL), lambda i: (i, 0)),
                  _resident((1, D_MODEL)), _resident((D_MODEL, PROJ_COLS))],
        out_specs=pl.BlockSpec((tm, PROJ_COLS), lambda i: (i, 0)),
        out_shape=jax.ShapeDtypeStruct((t, PROJ_COLS), F32),
        compiler_params=_cparams("arbitrary"),
        name="in_proj",
    )(x, g, w)


def _seq_spec(rows, name, n_steps):
    lo, hi = _DST_COLS[name]
    return pl.BlockSpec((rows, hi - lo), lambda b, n, c=_col_block(name): (b * n_steps + n, c))


def _small_spec(rows, n_steps):
    return pl.BlockSpec((rows, V7X_LANES), lambda b, n: (b * n_steps + n, SMALL_BLOCK))


def _out_spec(rows, width, n_steps):
    return pl.BlockSpec((rows, width), lambda b, n: (b * n_steps + n, 0))


def _causal_conv(x, halo_ref, conv0_ref, conv_out_ref, w_ref, rows, seq_rows, n_valid):
    first = V7X_SUBLANES - (CONV_K - 1)

    def taps(i):
        y = w_ref[0:1, :] * halo_ref[i, first:first + seq_rows, :]
        for j in range(1, CONV_K):
            y = y + w_ref[j:j + 1, :] * halo_ref[i, first + j:first + j + seq_rows, :]
        return y

    if seq_rows == rows:
        @pl.when(pl.program_id(1) == 0)
        def _():
            halo_ref[0, first:V7X_SUBLANES, :] = conv0_ref[0]

        halo_ref[0, V7X_SUBLANES:V7X_SUBLANES + rows, :] = x
        y = taps(0)
        halo_ref[0, 0:V7X_SUBLANES, :] = x[rows - V7X_SUBLANES:rows, :]
        conv_out_ref[0] = x[rows - (CONV_K - 1):rows, :]
        return y
    parts = []
    for i in range(rows // seq_rows):
        halo_ref[i, first:V7X_SUBLANES, :] = conv0_ref[i]
        halo_ref[i, V7X_SUBLANES:V7X_SUBLANES + seq_rows, :] = x[i * seq_rows:(i + 1) * seq_rows, :]
        parts.append(taps(i))
        conv_out_ref[i] = x[i * seq_rows + n_valid - (CONV_K - 1):i * seq_rows + n_valid, :]
    return jnp.concatenate(parts, axis=0)


def _level_masks(rows):
    i = np.arange(rows)[:, None]
    j = np.arange(rows)[None, :]
    out = []
    s = 1
    while s < rows:
        out.append(((i // (2 * s) == j // (2 * s)) & (i % (2 * s) >= s) & (j % (2 * s) < s)).astype(np.float32))
        s *= 2
    return np.stack(out)


def _unit_lower_inverse(a_strict, lv_ref, eye):
    t = eye - a_strict * lv_ref[0]
    for lv in range(1, lv_ref.shape[0]):
        a_s = a_strict * lv_ref[lv]
        t = t - _mm(_mm(t, a_s), t)
    return t


def _gdn_kernel(qkv_ref, z_ref, sm_ref, conv0_ref, s0_ref, cw_ref, lane_ref, g_ref, lv_ref,
                o_ref, sfin_ref, conv_out_ref, halo_ref, s_ref, *, rows, chunk, n_valid, chained):
    n_chunks = rows // chunk
    shift = chunk.bit_length() - 1
    if chained:
        @pl.when(pl.program_id(1) == 0)
        def _():
            s_ref[...] = s0_ref[0]

    act = _silu(_causal_conv(qkv_ref[...], halo_ref, conv0_ref, conv_out_ref, cw_ref, rows,
                             rows if chained else chunk, n_valid))
    sm = sm_ref[...]
    row = _iota2(rows, rows, 0)
    col = _iota2(rows, rows, 1)
    same_chunk = lax.shift_right_logical(row, shift) == lax.shift_right_logical(col, shift)
    tril_blocks = jnp.where(same_chunk, (col <= row).astype(F32), 0.0).astype(BF16)
    valid = (_iota2(rows, V7X_LANES, 0) & (chunk - 1)) < n_valid
    log_alpha = jnp.where(valid, -jnp.exp(lane_ref[0:1, :]) * _softplus(sm + lane_ref[1:2, :]), 0.0)
    beta = jnp.where(valid, jax.nn.sigmoid(sm), 0.0)
    gcum = _mm_exact_lhs(tril_blocks, log_alpha)
    gcum_t = gcum.T
    rc = _iota2(chunk, chunk, 0)
    cc = _iota2(chunk, chunk, 1)
    tril = cc <= rc
    strict = cc < rc
    eye = (cc == rc).astype(F32)
    z = z_ref[...]
    chunks = range(n_chunks)
    rsl = [slice(c * chunk, (c + 1) * chunk) for c in chunks]
    outs = {}
    finals = [[None] * GDN_HEADS for _ in range(1 if chained else n_chunks)]
    for first in range(0, GDN_HEADS, GDN_HEAD_GROUP):
        heads = range(first, first + GDN_HEAD_GROUP)
        chains = [(h, c) for h in heads for c in chunks]
        q, k, kb, rhs, qd, gcol = {}, {}, {}, {}, {}, {}
        for h in heads:
            lane = LANE_A + h
            qh = act[:, h * GDN_DK:(h + 1) * GDN_DK]
            kh = act[:, GROUP_W + h * GDN_DK:GROUP_W + (h + 1) * GDN_DK]
            vh = act[:, 2 * GROUP_W + h * GDN_DV:2 * GROUP_W + (h + 1) * GDN_DV]
            qh = qh * lax.rsqrt(jnp.sum(qh * qh, axis=-1, keepdims=True) + EPS) * GDN_DK ** -0.5
            kh = kh * lax.rsqrt(jnp.sum(kh * kh, axis=-1, keepdims=True) + EPS)
            g_h = gcum[:, lane:lane + 1]
            b_h = beta[:, LANE_B + h:LANE_B + h + 1]
            e_h = jnp.exp(g_h)
            kb_h = kh * b_h
            rhs_h = jnp.concatenate([kb_h * e_h, vh * b_h], axis=1)
            qd_h = qh * e_h
            for c in chunks:
                q[h, c], k[h, c], kb[h, c] = qh[rsl[c]], kh[rsl[c]], kb_h[rsl[c]]
                rhs[h, c], qd[h, c], gcol[h, c] = rhs_h[rsl[c]], qd_h[rsl[c]], g_h[rsl[c]]
        decay = {(h, c): jnp.exp(jnp.where(tril, gcol[h, c] - gcum_t[LANE_A + h:LANE_A + h + 1, rsl[c]], NEG_BIG))
                 for h, c in chains}
        kk = {i: _mm(kb[i], k[i], _NT) for i in chains}
        qk = {i: _mm(q[i], k[i], _NT) * decay[i] for i in chains}
        a_strict = {i: jnp.where(strict, kk[i] * decay[i], 0.0) for i in chains}
        t_inv = {i: eye - a_strict[i] * lv_ref[0] for i in chains}
        for lv in range(1, lv_ref.shape[0]):
            half = {i: _mm(t_inv[i], a_strict[i] * lv_ref[lv]) for i in chains}
            t_inv = {i: t_inv[i] - _mm(half[i], t_inv[i]) for i in chains}
        sol = {i: _mm(t_inv[i], rhs[i]) for i in chains}
        glast = {i: gcol[i][chunk - 1:chunk] for i in chains}
        k_dec = {i: k[i] * jnp.exp(glast[i] - gcol[i]) for i in chains}
        state = {h: s_ref[h] for h in heads} if chained else None
        for c in chunks:
            cur = state if chained else {h: s0_ref[c, h] for h in heads}
            w_s = {h: _mm(sol[h, c][:, :GDN_DK], cur[h]) for h in heads}
            o_s = {h: _mm(qd[h, c], cur[h]) for h in heads}
            u_new = {h: sol[h, c][:, GDN_DK:] - w_s[h] for h in heads}
            for h in heads:
                outs[h, c] = o_s[h] + _mm(qk[h, c], u_new[h])
            nxt = {h: cur[h] * jnp.exp(glast[h, c]) + _mm(k_dec[h, c], u_new[h], _TN) for h in heads}
            if chained:
                state = nxt
            else:
                for h in heads:
                    finals[c][h] = nxt[h]
        if chained:
            for h in heads:
                finals[0][h] = state[h]
    head_outs = []
    for h in range(GDN_HEADS):
        o = outs[h, 0] if n_chunks == 1 else jnp.concatenate([outs[h, c] for c in chunks], axis=0)
        head_outs.append(_rms(o, g_ref[...]) * _silu(z[:, h * GDN_DV:(h + 1) * GDN_DV]))
    o_ref[...] = jnp.concatenate(head_outs, axis=1)
    new_states = jnp.stack([jnp.stack(per_seq) for per_seq in finals])
    sfin_ref[...] = new_states
    if chained:
        s_ref[...] = new_states[0]


def _gdn(p, conv0, s0, conv_w, a_log, dt_bias, norm_g, n_seq, seq_len, rows, chunk, n_valid):
    chained = seq_len > chunk
    seqs_per_step = 1 if chained else rows // chunk
    n_steps = seq_len // rows if chained else 1
    n_blocks = n_seq // seqs_per_step
    assert n_blocks * seqs_per_step == n_seq and chunk & (chunk - 1) == 0
    lane = jnp.zeros((2, V7X_LANES), F32)
    lane = lane.at[0, LANE_A:LANE_A + GDN_HEADS].set(a_log).at[1, LANE_A:LANE_A + GDN_HEADS].set(dt_bias)
    levels = jnp.asarray(_level_masks(chunk))
    halo_rows = V7X_SUBLANES + (rows if chained else chunk)
    return pl.pallas_call(
        functools.partial(_gdn_kernel, rows=rows, chunk=chunk, n_valid=n_valid, chained=chained),
        grid=(n_blocks, n_steps),
        in_specs=[_seq_spec(rows, 'qkv', n_steps), _seq_spec(rows, 'z_a', n_steps), _small_spec(rows, n_steps),
                  pl.BlockSpec((seqs_per_step, CONV_K - 1, 3 * GROUP_W), lambda b, n: (b, 0, 0)),
                  pl.BlockSpec((seqs_per_step, GDN_HEADS, GDN_DK, GDN_DV), lambda b, n: (b, 0, 0, 0)),
                  _resident((CONV_K, 3 * GROUP_W)), _resident((2, V7X_LANES)), _resident((1, GDN_DV)),
                  _resident(levels.shape)],
        out_specs=[_out_spec(rows, GROUP_W, n_steps),
                   pl.BlockSpec((seqs_per_step, GDN_HEADS, GDN_DK, GDN_DV), lambda b, n: (b, 0, 0, 0)),
                   pl.BlockSpec((seqs_per_step, CONV_K - 1, 3 * GROUP_W), lambda b, n: (b, 0, 0))],
        out_shape=[jax.ShapeDtypeStruct((n_seq * seq_len, GROUP_W), F32),
                   jax.ShapeDtypeStruct((n_seq, GDN_HEADS, GDN_DK, GDN_DV), F32),
                   jax.ShapeDtypeStruct((n_seq, CONV_K - 1, 3 * GROUP_W), F32)],
        scratch_shapes=[pltpu.VMEM((seqs_per_step, halo_rows, 3 * GROUP_W), F32),
                        pltpu.VMEM((GDN_HEADS, GDN_DK, GDN_DV), F32)],
        compiler_params=_cparams("arbitrary", "arbitrary"),
        name="gdn",
    )(p, p, p, conv0, s0, conv_w, lane, norm_g.reshape(1, GDN_DV), levels)


def _gmlp_kernel(u_ref, v_ref, lng_ref, lnb_ref, ws_ref, bias_ref, o_ref, vb_ref, *, rows, chunk):
    u = jax.nn.gelu(u_ref[...], approximate=True)
    v = jax.nn.gelu(v_ref[...], approximate=True)
    mu = jnp.mean(v, axis=-1, keepdims=True)
    var = jnp.mean(jnp.square(v - mu), axis=-1, keepdims=True)
    v = (v - mu) * lax.rsqrt(var + EPS) * lng_ref[...] + lnb_ref[...]
    vb_ref[...] = v
    tril = _iota2(chunk, chunk, 1) <= _iota2(chunk, chunk, 0)
    ws = [jnp.where(tril, ws_ref[g], 0.0).astype(BF16) for g in range(MLP_GROUPS)]
    for c in range(rows // chunk):
        rs = slice(c * chunk, (c + 1) * chunk)
        for g in range(MLP_GROUPS):
            ls = slice(g * MLP_GW, (g + 1) * MLP_GW)
            mixed = _mm(ws[g], v[rs, ls]) + bias_ref[:, ls]
            o_ref[rs, ls] = u[rs, ls] * mixed


def _gmlp(p, ln_g, ln_b, ws, bs, n_seq, seq_len, rows, chunk):
    n_steps = seq_len // rows
    bias = jnp.repeat(bs[:, :chunk].T, MLP_GW, axis=1)
    return pl.pallas_call(
        functools.partial(_gmlp_kernel, rows=rows, chunk=chunk),
        grid=(n_seq, n_steps),
        in_specs=[_seq_spec(rows, 'u', n_steps), _seq_spec(rows, 'v', n_steps),
                  _resident((1, GROUP_W)), _resident((1, GROUP_W)),
                  _resident((MLP_GROUPS, chunk, chunk)), _resident((chunk, GROUP_W))],
        out_specs=[_out_spec(rows, GROUP_W, n_steps), _out_spec(rows, GROUP_W, n_steps)],
        out_shape=[jax.ShapeDtypeStruct((n_seq * seq_len, GROUP_W), F32)] * 2,
        compiler_params=_cparams("arbitrary", "arbitrary"),
        name="gmlp",
    )(p, p, ln_g.reshape(1, GROUP_W), ln_b.reshape(1, GROUP_W), ws[:, :chunk, :chunk], bias)


def _ssd_kernel(xbc_ref, z_ref, sm_ref, conv0_ref, h0_ref, cw_ref, cb_ref, lane_ref, g_ref,
                o_ref, hfin_ref, conv_out_ref, halo_ref, h_ref, *, rows, seq_rows, n_valid):
    chained = seq_rows == rows
    n_seqs = rows // seq_rows
    shift = seq_rows.bit_length() - 1
    if chained:
        @pl.when(pl.program_id(1) == 0)
        def _():
            h_ref[...] = h0_ref[0]

    act = _silu(_causal_conv(xbc_ref[...], halo_ref, conv0_ref, conv_out_ref, cw_ref, rows, seq_rows, n_valid)
                + cb_ref[...])
    sm = sm_ref[...]
    row = _iota2(rows, rows, 0)
    col = _iota2(rows, rows, 1)
    tril = (col <= row) & (lax.shift_right_logical(row, shift) == lax.shift_right_logical(col, shift))
    valid = (_iota2(rows, V7X_LANES, 0) & (seq_rows - 1)) < n_valid
    dt = jnp.where(valid, _softplus(sm + lane_ref[1:2, :]), 0.0)
    acum = _mm_exact_lhs(jnp.where(tril, 1.0, 0.0).astype(BF16), dt * -jnp.exp(lane_ref[0:1, :]))
    acum_t = acum.T
    heads = range(SSM_HEADS)
    seqs = range(n_seqs)
    rsl = [slice(i * seq_rows, (i + 1) * seq_rows) for i in seqs]
    group_of = [h // (SSM_HEADS // SSM_GROUPS) for h in heads]
    b_g = [act[:, GROUP_W + g * SSM_STATE:GROUP_W + (g + 1) * SSM_STATE] for g in range(SSM_GROUPS)]
    c_g = [act[:, GROUP_W + (SSM_GROUPS + g) * SSM_STATE:GROUP_W + (SSM_GROUPS + g + 1) * SSM_STATE]
           for g in range(SSM_GROUPS)]
    cb = [_mm(c_g[g], b_g[g], _NT) for g in range(SSM_GROUPS)]
    acol = [acum[:, LANE_DT + h:LANE_DT + h + 1] for h in heads]
    decay = [jnp.exp(jnp.where(tril, acol[h] - acum_t[LANE_DT + h:LANE_DT + h + 1, :], NEG_BIG)) for h in heads]
    x = [act[:, h * SSM_HEAD_DIM:(h + 1) * SSM_HEAD_DIM] for h in heads]
    xdt = [x[h] * dt[:, LANE_DT + h:LANE_DT + h + 1] for h in heads]
    c_dec = [c_g[group_of[h]] * jnp.exp(acol[h]) for h in heads]
    y_intra = [_mm(cb[group_of[h]] * decay[h], xdt[h]) for h in heads]
    state = {(h, i): (h_ref[h] if chained else h0_ref[i, h]) for h in heads for i in seqs}
    alast = {(h, i): acol[h][(i + 1) * seq_rows - 1:(i + 1) * seq_rows] for h in heads for i in seqs}
    y_inter = {(h, i): _mm(c_dec[h][rsl[i]], state[h, i], _NT) for h in heads for i in seqs}
    h_new = {(h, i): state[h, i] * jnp.exp(alast[h, i])
             + _mm(xdt[h][rsl[i]], b_g[group_of[h]][rsl[i]] * jnp.exp(alast[h, i] - acol[h][rsl[i]]), _TN)
             for h in heads for i in seqs}
    y = jnp.concatenate(
        [y_intra[h] + (y_inter[h, 0] if chained else jnp.concatenate([y_inter[h, i] for i in seqs], axis=0))
         + lane_ref[2:3, LANE_DT + h:LANE_DT + h + 1] * x[h] for h in heads], axis=1)
    new_states = jnp.stack([jnp.stack([h_new[h, i] for h in heads]) for i in seqs])
    hfin_ref[...] = new_states
    if chained:
        h_ref[...] = new_states[0]
    o_ref[...] = _rms(y * _silu(z_ref[...]), g_ref[...])


def _ssd(p, conv0, h0, conv_w, conv_b, a_log, dt_bias, d_skip, norm_g, n_seq, seq_len, rows, n_valid):
    chained = seq_len >= rows
    seq_rows = rows if chained else seq_len
    seqs_per_step = rows // seq_rows
    n_steps = seq_len // rows if chained else 1
    n_blocks = n_seq // seqs_per_step
    assert n_blocks * seqs_per_step == n_seq and seq_rows & (seq_rows - 1) == 0
    lane = jnp.zeros((3, V7X_LANES), F32)
    lane = (lane.at[0, LANE_DT:LANE_DT + SSM_HEADS].set(a_log)
            .at[1, LANE_DT:LANE_DT + SSM_HEADS].set(dt_bias)
            .at[2, LANE_DT:LANE_DT + SSM_HEADS].set(d_skip))
    return pl.pallas_call(
        functools.partial(_ssd_kernel, rows=rows, seq_rows=seq_rows, n_valid=n_valid),
        grid=(n_blocks, n_steps),
        in_specs=[_seq_spec(rows, 'xbc', n_steps), _seq_spec(rows, 'z_c', n_steps), _small_spec(rows, n_steps),
                  pl.BlockSpec((seqs_per_step, CONV_K - 1, SSM_CONV_W), lambda b, n: (b, 0, 0)),
                  pl.BlockSpec((seqs_per_step, SSM_HEADS, SSM_HEAD_DIM, SSM_STATE), lambda b, n: (b, 0, 0, 0)),
                  _resident((CONV_K, SSM_CONV_W)), _resident((1, SSM_CONV_W)), _resident((3, V7X_LANES)),
                  _resident((1, GROUP_W))],
        out_specs=[_out_spec(rows, GROUP_W, n_steps),
                   pl.BlockSpec((seqs_per_step, SSM_HEADS, SSM_HEAD_DIM, SSM_STATE), lambda b, n: (b, 0, 0, 0)),
                   pl.BlockSpec((seqs_per_step, CONV_K - 1, SSM_CONV_W), lambda b, n: (b, 0, 0))],
        out_shape=[jax.ShapeDtypeStruct((n_seq * seq_len, GROUP_W), F32),
                   jax.ShapeDtypeStruct((n_seq, SSM_HEADS, SSM_HEAD_DIM, SSM_STATE), F32),
                   jax.ShapeDtypeStruct((n_seq, CONV_K - 1, SSM_CONV_W), F32)],
        scratch_shapes=[pltpu.VMEM((seqs_per_step, V7X_SUBLANES + seq_rows, SSM_CONV_W), F32),
                        pltpu.VMEM((SSM_HEADS, SSM_HEAD_DIM, SSM_STATE), F32)],
        compiler_params=_cparams("arbitrary", "arbitrary"),
        name="ssd",
    )(p, p, p, conv0, h0, conv_w, conv_b.reshape(1, SSM_CONV_W), lane, norm_g.reshape(1, GROUP_W))


def _rope_lanes(x, cos, sin_signed):
    lane = _iota2(x.shape[0], V7X_LANES, 1)
    half = MLA_ROPE // 2
    swapped = jnp.where(lane % MLA_ROPE < half,
                        pltpu.roll(x, V7X_LANES - half, axis=1), pltpu.roll(x, half, axis=1))
    return x * cos + swapped * sin_signed


def _rope_tables(pos):
    half = MLA_ROPE // 2
    inv_freq = ROPE_THETA ** (-jnp.arange(half, dtype=F32) / half)
    ang = pos.astype(F32)[:, None] * inv_freq[None, :]
    cos, sin = jnp.cos(ang), jnp.sin(ang)
    reps = V7X_LANES // MLA_ROPE
    return jnp.tile(jnp.concatenate([cos, cos], axis=1), (1, reps)), jnp.tile(jnp.concatenate([-sin, sin], axis=1), (1, reps))


def _mla_project(cq_ref, ckv_ref, sm_ref, cos_ref, sin_ref, gq_ref, gkv_ref, wn_ref, wpe_ref, wuk_ref):
    c_q = _rms(cq_ref[...], gq_ref[...]).astype(BF16)
    c_kv = _rms(ckv_ref[...], gkv_ref[...])
    cos, sin = cos_ref[...], sin_ref[...]
    lane = _iota2(c_kv.shape[0], V7X_LANES, 1)
    k_pe = _rope_lanes(jnp.where(lane < MLA_ROPE, sm_ref[...], 0.0), cos, sin)
    q_nope = jnp.dot(c_q, wn_ref[...], preferred_element_type=F32)
    q_heads = []
    for h in range(MLA_HEADS):
        q_lat = _mm(q_nope[:, h * MLA_NOPE:(h + 1) * MLA_NOPE], wuk_ref[h], _NT)
        q_pe = _rope_lanes(jnp.dot(c_q, wpe_ref[h], preferred_element_type=F32), cos, sin)
        q_heads.append((jnp.concatenate([q_lat, q_pe], axis=1) * MLA_SCALE).astype(BF16))
    return c_kv, k_pe, q_heads


def _softmax_step(q, keys, m, l, acc, mask=None):
    s = lax.dot_general(q, keys, _NT, preferred_element_type=F32)
    if mask is not None:
        s = jnp.where(mask, s, NEG_BIG)
    m_new = jnp.maximum(m, jnp.max(s, axis=-1, keepdims=True))
    p = jnp.exp(s - m_new)
    alpha = jnp.exp(m - m_new)
    l = alpha * l + jnp.sum(p, axis=-1, keepdims=True)
    acc = alpha * acc + jnp.dot(p.astype(BF16), keys[:, :MLA_KV_RANK], preferred_element_type=F32)
    return m_new, l, acc


def _mla_prompt_kernel(cq_ref, ckv_ref, sm_ref, cos_ref, sin_ref, gq_ref, gkv_ref, wn_ref, wpe_ref, wuk_ref,
                       wuv_ref, o_ref, ckv_out_ref, kpe_out_ref, keys_ref, vt_ref, q_ref, m_ref, acc_ref,
                       *, rows, kv_block):
    n = pl.program_id(1)
    qrows = MLA_HEADS * rows
    c_kv, k_pe, q_heads = _mla_project(cq_ref, ckv_ref, sm_ref, cos_ref, sin_ref, gq_ref, gkv_ref,
                                       wn_ref, wpe_ref, wuk_ref)
    ckv_out_ref[0] = c_kv
    kpe_out_ref[0] = k_pe[:, :MLA_ROPE]
    base = pl.multiple_of(n * rows, rows)
    keys_ref[pl.ds(base, rows), :] = jnp.concatenate([c_kv, k_pe], axis=1).astype(BF16)
    vt_ref[0:MLA_KV_RANK, pl.ds(base, rows)] = c_kv.T.astype(BF16)
    vt_ref[MLA_KV_RANK:MLA_VT_ROWS, pl.ds(base, rows)] = jnp.ones((MLA_VT_ROWS - MLA_KV_RANK, rows), BF16)
    q_ref[...] = jnp.concatenate(q_heads, axis=0)
    m_ref[...] = jnp.full((1, qrows), NEG_BIG, F32)
    acc_ref[...] = jnp.zeros((MLA_VT_ROWS, qrows), F32)

    def attend(off, mask):
        s = lax.dot_general(keys_ref[pl.ds(off, kv_block), :], q_ref[...], _NT, preferred_element_type=F32)
        if mask is not None:
            s = jnp.where(mask, s, NEG_BIG)
        m_old = m_ref[...]
        m_new = jnp.maximum(m_old, jnp.max(s, axis=0, keepdims=True))
        p = jnp.exp((s - m_new).astype(BF16))
        alpha = jnp.exp(m_old - m_new)
        acc_ref[...] = alpha * acc_ref[...] + jnp.dot(vt_ref[:, pl.ds(off, kv_block)], p,
                                                      preferred_element_type=F32)
        m_ref[...] = m_new

    def body(j, carry):
        attend(pl.multiple_of(j * kv_block, kv_block), None)
        return carry

    blocks_per_step = rows // kv_block
    lax.fori_loop(0, n * blocks_per_step, body, 0)
    q_pos = _iota2(kv_block, qrows, 1) & (rows - 1)
    for d in range(blocks_per_step):
        attend(base + d * kv_block, d * kv_block + _iota2(kv_block, qrows, 0) <= q_pos)
    o_lat_t = acc_ref[0:MLA_KV_RANK, :] / acc_ref[MLA_KV_RANK:MLA_KV_RANK + 1, :]
    outs = [_mm(o_lat_t[:, h * rows:(h + 1) * rows], wuv_ref[h], _TN) for h in range(MLA_HEADS)]
    o_ref[...] = jnp.concatenate(outs, axis=1)


def _mla_weights(w_uq, w_uk, w_uv):
    w4 = w_uq.reshape(MLA_Q_RANK, MLA_HEADS, MLA_NOPE + MLA_ROPE)
    w_nope = w4[:, :, :MLA_NOPE].reshape(MLA_Q_RANK, MLA_HEADS * MLA_NOPE).astype(BF16)
    w_pe = jnp.pad(w4[:, :, MLA_NOPE:].transpose(1, 0, 2),
                   ((0, 0), (0, 0), (0, V7X_LANES - MLA_ROPE))).astype(BF16)
    return w_nope, w_pe, w_uk.astype(BF16), w_uv.astype(BF16)


def _mla_prompt(p, q_norm_g, kv_norm_g, w_nope, w_pe, w_uk, w_uv, n_seq, seq_len, rows, kv_block):
    n_steps = seq_len // rows
    assert seq_len % rows == 0 and rows % kv_block == 0 and rows & (rows - 1) == 0
    cos, sin = _rope_tables(jnp.arange(seq_len))
    tab = pl.BlockSpec((rows, V7X_LANES), lambda b, n: (n, 0))
    return pl.pallas_call(
        functools.partial(_mla_prompt_kernel, rows=rows, kv_block=kv_block),
        grid=(n_seq, n_steps),
        in_specs=[_seq_spec(rows, 'c_q', n_steps), _seq_spec(rows, 'c_kv', n_steps), _small_spec(rows, n_steps),
                  tab, tab, _resident((1, MLA_Q_RANK)), _resident((1, MLA_KV_RANK)),
                  _resident(w_nope.shape), _resident(w_pe.shape), _resident(w_uk.shape), _resident(w_uv.shape)],
        out_specs=[_out_spec(rows, GROUP_W, n_steps),
                   pl.BlockSpec((1, rows, MLA_KV_RANK), lambda b, n: (b, n, 0)),
                   pl.BlockSpec((1, rows, MLA_ROPE), lambda b, n: (b, n, 0))],
        out_shape=[jax.ShapeDtypeStruct((n_seq * seq_len, GROUP_W), F32),
                   jax.ShapeDtypeStruct((n_seq, seq_len, MLA_KV_RANK), F32),
                   jax.ShapeDtypeStruct((n_seq, seq_len, MLA_ROPE), F32)],
        scratch_shapes=[pltpu.VMEM((seq_len, 2 * V7X_LANES), BF16),
                        pltpu.VMEM((MLA_VT_ROWS, seq_len), BF16),
                        pltpu.VMEM((MLA_HEADS * rows, 2 * V7X_LANES), BF16),
                        pltpu.VMEM((1, MLA_HEADS * rows), F32),
                        pltpu.VMEM((MLA_VT_ROWS, MLA_HEADS * rows), F32)],
        compiler_params=_cparams("arbitrary", "arbitrary"),
        name="mla_prompt",
    )(p, p, p, cos, sin, q_norm_g.reshape(1, MLA_Q_RANK), kv_norm_g.reshape(1, MLA_KV_RANK),
      w_nope, w_pe, w_uk, w_uv)


def _mla_sample_project_kernel(cq_ref, ckv_ref, sm_ref, cos_ref, sin_ref, gq_ref, gkv_ref, wn_ref, wpe_ref, wuk_ref,
                               q_out_ref, knew_out_ref, ckv_out_ref, kpe_out_ref):
    c_kv, k_pe, q_heads = _mla_project(cq_ref, ckv_ref, sm_ref, cos_ref, sin_ref, gq_ref, gkv_ref,
                                       wn_ref, wpe_ref, wuk_ref)
    ckv_out_ref[...] = c_kv
    kpe_out_ref[...] = k_pe[:, :MLA_ROPE]
    knew_out_ref[...] = jnp.concatenate([c_kv, k_pe], axis=1)
    for h in range(MLA_HEADS):
        q_out_ref[h] = q_heads[h].astype(F32)


def _mla_sample_project(p, q_norm_g, kv_norm_g, w_nope, w_pe, w_uk, n_seq, rows, past_len):
    n_tok = n_seq * rows
    tile = min(SEQ_TILE, n_tok)
    assert n_tok % tile == 0 and tile % rows == 0
    cos, sin = _rope_tables(past_len + jnp.arange(rows))
    cos, sin = jnp.tile(cos, (tile // rows, 1)), jnp.tile(sin, (tile // rows, 1))

    def piece(name):
        lo, hi = _DST_COLS[name]
        return pl.BlockSpec((tile, hi - lo), lambda i, c=_col_block(name): (i, c))

    width = 2 * V7X_LANES
    return pl.pallas_call(
        _mla_sample_project_kernel,
        grid=(n_tok // tile,),
        in_specs=[piece('c_q'), piece('c_kv'), pl.BlockSpec((tile, V7X_LANES), lambda i: (i, SMALL_BLOCK)),
                  _resident((tile, V7X_LANES)), _resident((tile, V7X_LANES)),
                  _resident((1, MLA_Q_RANK)), _resident((1, MLA_KV_RANK)),
                  _resident(w_nope.shape), _resident(w_pe.shape), _resident(w_uk.shape)],
        out_specs=[pl.BlockSpec((MLA_HEADS, tile, width), lambda i: (0, i, 0)),
                   pl.BlockSpec((tile, width), lambda i: (i, 0)),
                   pl.BlockSpec((tile, MLA_KV_RANK), lambda i: (i, 0)),
                   pl.BlockSpec((tile, MLA_ROPE), lambda i: (i, 0))],
        out_shape=[jax.ShapeDtypeStruct((MLA_HEADS, n_tok, width), F32),
                   jax.ShapeDtypeStruct((n_tok, width), F32),
                   jax.ShapeDtypeStruct((n_tok, MLA_KV_RANK), F32),
                   jax.ShapeDtypeStruct((n_tok, MLA_ROPE), F32)],
        compiler_params=_cparams("arbitrary"),
        name="mla_sample_project",
    )(p, p, p, cos, sin, q_norm_g.reshape(1, MLA_Q_RANK), kv_norm_g.reshape(1, MLA_KV_RANK), w_nope, w_pe, w_uk)


def _mla_decode_kernel(pt_ref, q_ref, knew_ref, wuv_ref, ckv_hbm, kpe_hbm, o_ref,
                       lat_buf, pe_buf, sems, s_ref, latb_ref, *, rows, layer, n_pages, pages_per_chunk):
    seq = pl.program_id(0)
    slot = lax.rem(seq, 2)
    qrows = MLA_HEADS * rows
    chunk = pages_per_chunk * PAGE_SIZE
    n_chunks = n_pages // pages_per_chunk

    def page_copies(src_seq, dst_slot, g):
        page = pt_ref[src_seq, g]
        return (pltpu.make_async_copy(ckv_hbm.at[layer, page], lat_buf.at[dst_slot, g], sems.at[0, dst_slot]),
                pltpu.make_async_copy(kpe_hbm.at[layer, page],
                                      pe_buf.at[dst_slot, :, pl.ds(pl.multiple_of(g * PAGE_SIZE, PAGE_SIZE), PAGE_SIZE)],
                                      sems.at[1, dst_slot]))

    def fetch(src_seq, dst_slot):
        def body(g, carry):
            for thread, cp in enumerate(page_copies(src_seq, dst_slot, g)):
                cp.start(priority=thread)
            return carry
        lax.fori_loop(0, n_pages, body, 0, unroll=8)

    @pl.when(seq == 0)
    def _():
        fetch(0, 0)

    @pl.when(seq + 1 < pl.num_programs(0))
    def _():
        fetch(seq + 1, 1 - slot)

    knew = knew_ref[...].astype(BF16)
    q = jnp.concatenate([q_ref[h] for h in range(MLA_HEADS)], axis=0).astype(BF16)
    q_lat, q_pe = q[:, :MLA_KV_RANK], q[:, MLA_KV_RANK:MLA_KV_RANK + MLA_ROPE]
    causal = _iota2(qrows, rows, 1) <= (_iota2(qrows, rows, 0) & (rows - 1))
    s_new = jnp.where(causal, lax.dot_general(q, knew, _NT, preferred_element_type=F32), NEG_BIG)

    pltpu.make_async_copy(lat_buf.at[slot], lat_buf.at[slot], sems.at[0, slot]).wait()
    pltpu.make_async_copy(pe_buf.at[slot], pe_buf.at[slot], sems.at[1, slot]).wait()

    def to_bf16(g, carry):
        latb_ref[pl.ds(pl.multiple_of(g * PAGE_SIZE, PAGE_SIZE), PAGE_SIZE), :] = lat_buf[slot, g].astype(BF16)
        return carry
    lax.fori_loop(0, n_pages, to_bf16, 0, unroll=8)

    m = jnp.max(s_new, axis=-1, keepdims=True)
    for c in range(n_chunks):
        s_lat = lax.dot_general(q_lat, latb_ref[c * chunk:(c + 1) * chunk, :], _NT, preferred_element_type=F32)
        s_pe = jnp.dot(q_pe, pe_buf[slot, :, c * chunk:(c + 1) * chunk].astype(BF16), preferred_element_type=F32)
        s = s_lat + s_pe
        s_ref[:, c * chunk:(c + 1) * chunk] = s
        m = jnp.maximum(m, jnp.max(s, axis=-1, keepdims=True))

    p_new = jnp.exp(s_new - m)
    l = jnp.sum(p_new, axis=-1, keepdims=True)
    acc = jnp.dot(p_new.astype(BF16), knew[:, :MLA_KV_RANK], preferred_element_type=F32)
    for c in range(n_chunks):
        pr = jnp.exp(s_ref[:, c * chunk:(c + 1) * chunk] - m)
        l = l + jnp.sum(pr, axis=-1, keepdims=True)
        acc = acc + jnp.dot(pr.astype(BF16), latb_ref[c * chunk:(c + 1) * chunk, :], preferred_element_type=F32)
    o_lat = acc / l
    outs = [_mm(o_lat[h * rows:(h + 1) * rows], wuv_ref[h]) for h in range(MLA_HEADS)]
    o_ref[...] = jnp.concatenate(outs, axis=1)


def _mla_decode(p, cache_ckv, cache_kpe, layer, page_table, q_norm_g, kv_norm_g, w_nope, w_pe, w_uk, w_uv,
                n_seq, rows, pages_per_chunk=16):
    n_pages = page_table.shape[1]
    pages_per_chunk = min(pages_per_chunk, n_pages)
    assert n_pages % pages_per_chunk == 0
    past_len = n_pages * PAGE_SIZE
    q_new, k_new, c_kv, k_pe = _mla_sample_project(p, q_norm_g, kv_norm_g, w_nope, w_pe, w_uk, n_seq, rows, past_len)
    width = 2 * V7X_LANES
    in_specs = [pl.BlockSpec((MLA_HEADS, rows, width), lambda s, pt: (0, s, 0)),
                pl.BlockSpec((rows, width), lambda s, pt: (s, 0)),
                pl.BlockSpec(w_uv.shape, lambda s, pt: (0,) * w_uv.ndim),
                pl.BlockSpec(memory_space=pl.ANY), pl.BlockSpec(memory_space=pl.ANY)]
    qrows = MLA_HEADS * rows
    out_d, = pl.pallas_call(
        functools.partial(_mla_decode_kernel, rows=rows, layer=layer, n_pages=n_pages,
                          pages_per_chunk=pages_per_chunk),
        grid_spec=pltpu.PrefetchScalarGridSpec(
            num_scalar_prefetch=1,
            grid=(n_seq,),
            in_specs=in_specs,
            out_specs=[pl.BlockSpec((rows, GROUP_W), lambda s, pt: (s, 0))],
            scratch_shapes=[pltpu.VMEM((2, n_pages, PAGE_SIZE, MLA_KV_RANK), F32),
                            pltpu.VMEM((2, MLA_ROPE, past_len), F32),
                            pltpu.SemaphoreType.DMA((2, 2)),
                            pltpu.VMEM((qrows, past_len), F32),
                            pltpu.VMEM((past_len, MLA_KV_RANK), BF16)]),
        out_shape=[jax.ShapeDtypeStruct((n_seq * rows, GROUP_W), F32)],
        compiler_params=_cparams("arbitrary"),
        name="mla_decode",
    )(page_table, q_new, k_new, w_uv, cache_ckv, cache_kpe)
    return out_d, c_kv.reshape(n_seq, rows, MLA_KV_RANK), k_pe.reshape(n_seq, rows, MLA_ROPE)


def _trunk_layer(x, w, layer, n_seq, seq_len, rows, n_valid, gdn_conv, gdn_s, ssm_conv, ssm_h, paged):
    g = w['norm_g']
    assert n_valid == seq_len or seq_len == rows
    step_valid = min(n_valid, rows)
    x = _ffn(x, g[0:1], g[1:2], w['ffn_w_in'][0], w['ffn_w_out'][0])
    p = _proj(x, g[2:3], w['w_in'])
    if seq_len == rows:
        gdn_shape = (rows * SAMPLE_SEQS_PER_STEP, rows, n_valid)
        mlp_shape = (1, n_seq * seq_len, min(SEQ_TILE, n_seq * seq_len), rows)
        ssd_rows = rows * SAMPLE_SEQS_PER_STEP
    else:
        gdn_shape = (rows, GDN_CHUNK, GDN_CHUNK)
        mlp_shape = (n_seq, seq_len, rows, MLP_CHUNK)
        ssd_rows = rows
    out_a, gdn_s_new, gdn_conv_new = _gdn(p, gdn_conv, gdn_s, w['gdn_conv_w'], w['gdn_a_log'], w['gdn_dt_bias'],
                            w['gdn_norm_g'], n_seq, seq_len, *gdn_shape)
    out_b, v_b = _gmlp(p, w['mlp_ln_g'], w['mlp_ln_b'], w['mlp_ws'], w['mlp_bs'], *mlp_shape)
    out_c, ssm_h_new, ssm_conv_new = _ssd(p, ssm_conv, ssm_h, w['ssm_conv_w'], w['ssm_conv_b'], w['ssm_a_log'],
                            w['ssm_dt_bias'], w['ssm_d'], w['ssm_norm_g'], n_seq, seq_len, ssd_rows, step_valid)
    mla_w = (w['mla_q_norm_g'], w['mla_kv_norm_g'], w['mla_w_nope'], w['mla_w_pe'], w['mla_w_uk'], w['mla_w_uv'])
    if paged is None:
        out_d, c_kv, k_pe = _mla_prompt(p, *mla_w, n_seq, seq_len, MLA_Q_TILE, MLA_KV_BLOCK)
    else:
        cache_ckv, cache_kpe, page_table = paged
        out_d, c_kv, k_pe = _mla_decode(p, cache_ckv, cache_kpe, layer, page_table, *mla_w, n_seq, rows)
    x = _ffn(x, g[4:5], g[5:6], w['ffn_w_in'][1], w['ffn_w_out'][1],
             mix=(out_a, out_b, out_c, out_d), w_o=w['w_out'], g_mix=g[3:4])
    v_b = v_b.reshape(n_seq, seq_len, GROUP_W)[:, :n_valid]
    return x, (c_kv[:, :n_valid], k_pe[:, :n_valid], gdn_s_new, gdn_conv_new, ssm_h_new, ssm_conv_new, v_b)


def kernel(x_prompt, x_sample, cache_ckv, cache_kpe, page_table, state_gdn_s, state_gdn_conv,
           state_ssm_h, state_ssm_conv, norm_g, ffn_w_in, ffn_w_out, w_in, w_out,
           gdn_conv_w, gdn_a_log, gdn_dt_bias, gdn_norm_g, mlp_ln_g, mlp_ln_b, mlp_ws, mlp_bs,
           ssm_conv_w, ssm_conv_b, ssm_a_log, ssm_dt_bias, ssm_d, ssm_norm_g,
           mla_q_norm_g, mla_w_uq, mla_kv_norm_g, mla_w_uk, mla_w_uv):
    bp, lp, _ = x_prompt.shape
    bs, ls, _ = x_sample.shape
    assert lp % SEQ_TILE == 0 and CONV_K - 1 <= ls <= SAMPLE_ROWS
    w_in_p = jnp.concatenate(
        [w_in[:, :, _SRC_COLS[name][0]:_SRC_COLS[name][1]] for name in _DST_ORDER]
        + [jnp.zeros(w_in.shape[:2] + (PROJ_COLS - PROJ_USED,), w_in.dtype)], axis=2)
    cache_kpe_t = jnp.swapaxes(cache_kpe, 2, 3)
    weights = dict(norm_g=norm_g, ffn_w_in=ffn_w_in.astype(BF16), ffn_w_out=ffn_w_out.astype(BF16),
                   w_in=w_in_p.astype(BF16), w_out=w_out.astype(BF16),
                   gdn_conv_w=gdn_conv_w, gdn_a_log=gdn_a_log, gdn_dt_bias=gdn_dt_bias, gdn_norm_g=gdn_norm_g,
                   mlp_ln_g=mlp_ln_g, mlp_ln_b=mlp_ln_b, mlp_ws=mlp_ws, mlp_bs=mlp_bs,
                   ssm_conv_w=ssm_conv_w, ssm_conv_b=ssm_conv_b, ssm_a_log=ssm_a_log,
                   ssm_dt_bias=ssm_dt_bias, ssm_d=ssm_d, ssm_norm_g=ssm_norm_g,
                   mla_q_norm_g=mla_q_norm_g, mla_kv_norm_g=mla_kv_norm_g)
    xp = x_prompt.reshape(bp * lp, D_MODEL)
    xs = jnp.pad(x_sample, ((0, 0), (0, SAMPLE_ROWS - ls), (0, 0))).reshape(bs * SAMPLE_ROWS, D_MODEL)
    zeros_p = (jnp.zeros((bp, CONV_K - 1, 3 * GROUP_W), F32), jnp.zeros((bp, GDN_HEADS, GDN_DK, GDN_DV), F32),
               jnp.zeros((bp, CONV_K - 1, SSM_CONV_W), F32), jnp.zeros((bp, SSM_HEADS, SSM_HEAD_DIM, SSM_STATE), F32))
    st_p, st_s = [], []
    for l in range(DEPTH):
        wl = {name: arr[l] for name, arr in weights.items()}
        wl['mla_w_nope'], wl['mla_w_pe'], wl['mla_w_uk'], wl['mla_w_uv'] = _mla_weights(
            mla_w_uq[l], mla_w_uk[l], mla_w_uv[l])
        xp, st = _trunk_layer(xp, wl, l, bp, lp, SEQ_TILE, lp, *zeros_p, None)
        st_p.append(st)
        xs, st = _trunk_layer(xs, wl, l, bs, SAMPLE_ROWS, SAMPLE_ROWS, ls,
                              state_gdn_conv[l], state_gdn_s[l], state_ssm_conv[l], state_ssm_h[l],
                              (cache_ckv, cache_kpe_t, page_table))
        st_s.append(st)

    def stack(states, i):
        return jnp.stack([s[i] for s in states])

    y_prompt = xp.reshape(bp, lp, D_MODEL)
    y_sample = xs.reshape(bs, SAMPLE_ROWS, D_MODEL)[:, :ls]
    return (y_prompt, y_sample,
            stack(st_p, 0), stack(st_p, 1), stack(st_p, 2), stack(st_p, 3), stack(st_p, 4), stack(st_p, 5),
            stack(st_s, 0), stack(st_s, 1), stack(st_s, 2), stack(st_s, 3), stack(st_s, 4), stack(st_s, 5),
            stack(st_s, 6))
```
